```python
import math
import jax
import jax.numpy as jnp
from jax import lax
import numpy as np

D_MODEL = 1024
BATCH = 4
SEQ = 4096
DEPTH = 4
DEC_BATCH = 128
DEC_SEQ = 1
PAST_LEN = 2048
PAGE_SIZE = 128

N_MIXERS = 4
N_FOX_LAYERS = (DEPTH + 3) // N_MIXERS
N_SHORT_LAYERS = (DEPTH + 2) // N_MIXERS
N_GLA_LAYERS = (DEPTH + 1) // N_MIXERS
N_SSD_LAYERS = DEPTH // N_MIXERS

DEEPNORM_ALPHA = (2 * DEPTH) ** 0.25
DEEPNORM_BETA = (8 * DEPTH) ** -0.25
LN_EPS = 1e-5
RMS_EPS = 1e-6

FOX_HEADS = 8
FOX_HEAD_DIM = D_MODEL // FOX_HEADS
FOX_Q_BLOCK = 128

SHORT_WIDTH = 3

GLA_HEADS = 4
GLA_DK = D_MODEL // 2
GLA_DV = D_MODEL
GLA_DK_HEAD = GLA_DK // GLA_HEADS
GLA_DV_HEAD = GLA_DV // GLA_HEADS
GLA_GATE_RANK = 16
GLA_TAU = 16.0
GLA_CHUNK = 64

SSD_INNER = 2 * D_MODEL
SSD_HEAD_DIM = 64
SSD_HEADS = SSD_INNER // SSD_HEAD_DIM
SSD_STATE = 128
SSD_GROUPS = 4
SSD_HEADS_PER_GROUP = SSD_HEADS // SSD_GROUPS
SSD_CONV = 4
SSD_CONV_CH = SSD_INNER + 2 * SSD_GROUPS * SSD_STATE
SSD_CHUNK = 64

N_EXPERTS = 32
TOP_K = 4
D_FF = D_MODEL
SWIGLU_LIMIT = 7.0
SWIGLU_ALPHA = 1.702
MOE_BLOCK = 128

kernel_name = 'hybrid_fox_shortconv_gla_ssd_moe_step'


def layer_norm(x, g, b):
    xf = x.astype(jnp.float32)
    mu = jnp.mean(xf, axis=-1, keepdims=True)
    xc = xf - mu
    var = jnp.mean(xc * xc, axis=-1, keepdims=True)
    y = xc * lax.rsqrt(var + LN_EPS) * g.astype(jnp.float32) + b.astype(jnp.float32)
    return y.astype(x.dtype)


def rms_norm(x, g):
    xf = x.astype(jnp.float32)
    y = xf * lax.rsqrt(jnp.mean(xf * xf, axis=-1, keepdims=True) + RMS_EPS)
    return (y * g.astype(jnp.float32)).astype(x.dtype)


def depthwise_conv(xp, w):
    return lax.conv_general_dilated(xp, w[:, None, :].astype(xp.dtype), (1,), 'VALID',
                                    dimension_numbers=('NWC', 'WIO', 'NWC'),
                                    feature_group_count=w.shape[-1])


def fox_project(x, w_in, b_f):
    bsz, n, _ = x.shape
    hd = FOX_HEADS * FOX_HEAD_DIM
    q, k, v, f = jnp.split(x @ w_in, [hd, 2 * hd, 3 * hd], axis=-1)
    shp = (bsz, n, FOX_HEADS, FOX_HEAD_DIM)
    logf = jax.nn.log_sigmoid((f + b_f).astype(jnp.float32))
    return q.reshape(shp) * FOX_HEAD_DIM ** -0.5, k.reshape(shp), v.reshape(shp), logf


def fox_prompt(x, w_in, b_f, w_out):
    bsz, n, _ = x.shape
    q, k, v, logf = fox_project(x, w_in, b_f)
    c_k = jnp.cumsum(logf, axis=1).transpose(0, 2, 1)
    nb = n // FOX_Q_BLOCK
    q_blocks = q.reshape(bsz, nb, FOX_Q_BLOCK, FOX_HEADS, FOX_HEAD_DIM).transpose(1, 0, 2, 3, 4)
    c_blocks = c_k.reshape(bsz, FOX_HEADS, nb, FOX_Q_BLOCK).transpose(2, 0, 1, 3)
    k_pos = jnp.arange(n)

    def attend(args):
        blk, q_b, c_b = args
        s = jnp.einsum('bqhd,bkhd->bhqk', q_b, k, preferred_element_type=jnp.float32)
        s = s + (c_b[..., :, None] - c_k[:, :, None, :])
        q_pos = blk * FOX_Q_BLOCK + jnp.arange(FOX_Q_BLOCK)
        s = jnp.where(k_pos[None, :] <= q_pos[:, None], s, -jnp.inf)
        p = jax.nn.softmax(s, axis=-1).astype(v.dtype)
        return jnp.einsum('bhqk,bkhd->bqhd', p, v)

    o = lax.map(attend, (jnp.arange(nb), q_blocks, c_blocks))
    o = o.transpose(1, 0, 2, 3, 4).reshape(bsz, n, FOX_HEADS * FOX_HEAD_DIM)
    return o @ w_out, k, v, logf.astype(x.dtype)


def fox_sample(x, k_pool, v_pool, logf_pool, layer, page_table, w_in, b_f, w_out):
    bsz, n, _ = x.shape
    q, k, v, logf = fox_project(x, w_in, b_f)
    past = page_table.shape[1] * PAGE_SIZE
    k_past = k_pool[layer, page_table].reshape(bsz, past, FOX_HEADS, FOX_HEAD_DIM)
    v_past = v_pool[layer, page_table].reshape(bsz, past, FOX_HEADS, FOX_HEAD_DIM)
    logf_past = logf_pool[layer, page_table].reshape(bsz, past, FOX_HEADS).astype(jnp.float32)
    c_all = jnp.cumsum(jnp.concatenate([logf_past, logf], axis=1), axis=1).transpose(0, 2, 1)
    c_q = c_all[:, :, past:]
    s_past = jnp.einsum('bqhd,bkhd->bhqk', q, k_past, preferred_element_type=jnp.float32)
    s_past = s_past + (c_q[..., :, None] - c_all[:, :, None, :past])
    s_new = jnp.einsum('bqhd,bkhd->bhqk', q, k, preferred_element_type=jnp.float32)
    s_new = s_new + (c_q[..., :, None] - c_q[:, :, None, :])
    s_new = jnp.where(jnp.tril(jnp.ones((n, n), bool)), s_new, -jnp.inf)
    p = jax.nn.softmax(jnp.concatenate([s_past, s_new], axis=-1), axis=-1).astype(v.dtype)
    o = (jnp.einsum('bhqk,bkhd->bqhd', p[..., :past], v_past)
         + jnp.einsum('bhqk,bkhd->bqhd', p[..., past:], v))
    return o.reshape(bsz, n, FOX_HEADS * FOX_HEAD_DIM) @ w_out, k, v, logf.astype(x.dtype)


def short_conv(x, buf, w_in, w_conv, w_out):
    b_gate, c_gate, h = jnp.split(x @ w_in, 3, axis=-1)
    u = jnp.concatenate([buf.astype(x.dtype), c_gate * h], axis=1)
    y = (b_gate * depthwise_conv(u, w_conv)) @ w_out
    return y, u[:, u.shape[1] - (SHORT_WIDTH - 1):]


def gla_project(x, w_in, w_gate2, b_gate):
    bsz, n, _ = x.shape
    cuts = [GLA_DK, 2 * GLA_DK, 2 * GLA_DK + GLA_DV, 2 * GLA_DK + 2 * GLA_DV]
    q, k, v, g, r = jnp.split(x @ w_in, cuts, axis=-1)
    log_a = jax.nn.log_sigmoid((r @ w_gate2 + b_gate).astype(jnp.float32)) / GLA_TAU
    hk = (bsz, n, GLA_HEADS, GLA_DK_HEAD)
    hv = (bsz, n, GLA_HEADS, GLA_DV_HEAD)
    f32 = jnp.float32
    return (q.reshape(hk).astype(f32) * GLA_DK_HEAD ** -0.5, k.reshape(hk).astype(f32),
            v.reshape(hv).astype(f32), log_a.reshape(hk), g.reshape(hv))


def gla_chunked(q, k, v, log_a):
    bsz, n, h, dk = q.shape
    dv = v.shape[-1]
    nc = n // GLA_CHUNK

    def chunk(t):
        return t.reshape(bsz, nc, GLA_CHUNK, h, t.shape[-1]).transpose(1, 0, 3, 2, 4)

    qc, kc, vc, gc = chunk(q), chunk(k), chunk(v), chunk(log_a)
    b = jnp.cumsum(gc, axis=3)
    b_last = b[..., -1:, :]
    q_dec = qc * jnp.exp(b)
    k_inv = kc * jnp.exp(-b)
    k_end = kc * jnp.exp(b_last - b)
    causal = jnp.tril(jnp.ones((GLA_CHUNK, GLA_CHUNK), bool))
    scores = jnp.where(causal, jnp.einsum('cbhtk,cbhsk->cbhts', q_dec, k_inv), 0.0)
    o_intra = jnp.einsum('cbhts,cbhsv->cbhtv', scores, vc)

    def step(s, inp):
        q_i, k_i, v_i, bl_i = inp
        o_i = jnp.einsum('bhtk,bhkv->bhtv', q_i, s)
        s = s * jnp.exp(bl_i[..., 0, :])[..., None] + jnp.einsum('bhsk,bhsv->bhkv', k_i, v_i)
        return s, o_i

    s0 = jnp.zeros((bsz, h, dk, dv), jnp.float32)
    s_fin, o_inter = lax.scan(step, s0, (q_dec, k_end, vc, b_last))
    o = (o_intra + o_inter).transpose(1, 0, 3, 2, 4).reshape(bsz, n, h, dv)
    return o, s_fin


def gla_recurrent(q, k, v, log_a, s0):
    def step(s, inp):
        q_t, k_t, v_t, g_t = inp
        s = s * jnp.exp(g_t)[..., None] + k_t[..., :, None] * v_t[..., None, :]
        return s, jnp.einsum('bhk,bhkv->bhv', q_t, s)

    s_fin, o = lax.scan(step, s0, tuple(jnp.moveaxis(t, 1, 0) for t in (q, k, v, log_a)))
    return jnp.moveaxis(o, 0, 1), s_fin


def gla_mixer(x, s0, w_in, w_gate2, b_gate, g_norm, w_out):
    q, k, v, log_a, g = gla_project(x, w_in, w_gate2, b_gate)
    if s0 is None:
        o, s = gla_chunked(q, k, v, log_a)
    else:
        o, s = gla_recurrent(q, k, v, log_a, s0.astype(jnp.float32))
    o = rms_norm(o.astype(x.dtype), g_norm) * jax.nn.silu(g)
    return o.reshape(x.shape[0], x.shape[1], GLA_DV) @ w_out, s.astype(x.dtype)


def ssd_project(x, conv_buf, w_in, w_conv, b_conv, dt_bias):
    bsz, n, _ = x.shape
    z, xbc, dt = jnp.split(x @ w_in, [SSD_INNER, SSD_INNER + SSD_CONV_CH], axis=-1)
    xp = jnp.concatenate([conv_buf.astype(x.dtype), xbc], axis=1)
    act = jax.nn.silu(depthwise_conv(xp, w_conv) + b_conv)
    xs, bm, cm = jnp.split(act, [SSD_INNER, SSD_INNER + SSD_GROUPS * SSD_STATE], axis=-1)
    f32 = jnp.float32
    xs = xs.reshape(bsz, n, SSD_GROUPS, SSD_HEADS_PER_GROUP, SSD_HEAD_DIM).astype(f32)
    bm = bm.reshape(bsz, n, SSD_GROUPS, SSD_STATE).astype(f32)
    cm = cm.reshape(bsz, n, SSD_GROUPS, SSD_STATE).astype(f32)
    dt = jax.nn.softplus((dt + dt_bias).astype(f32)).reshape(bsz, n, SSD_GROUPS, SSD_HEADS_PER_GROUP)
    return z, xs, bm, cm, dt, xp[:, xp.shape[1] - (SSD_CONV - 1):]


def ssd_chunked(xs, bm, cm, dt, a):
    bsz, n = xs.shape[:2]
    nc = n // SSD_CHUNK

    def chunk(t):
        return t.reshape((bsz, nc, SSD_CHUNK) + t.shape[2:])

    xs_c, bm_c, cm_c, dt_c = chunk(xs), chunk(bm), chunk(cm), chunk(dt)
    cum = jnp.cumsum(dt_c * a, axis=2)
    xdt = xs_c * dt_c[..., None]
    cum_t = cum.transpose(0, 1, 3, 4, 2)
    causal = jnp.tril(jnp.ones((SSD_CHUNK, SSD_CHUNK), bool))
    decay = jnp.exp(jnp.where(causal, cum_t[..., :, None] - cum_t[..., None, :], -jnp.inf))
    cb = jnp.einsum('bctgn,bcsgn->bcgts', cm_c, bm_c)
    y_intra = jnp.einsum('bcgts,bcgrts,bcsgrp->bctgrp', cb, decay, xdt)

    def step(h, inp):
        c_i, b_i, xdt_i, cum_i = inp
        y_i = jnp.einsum('btgn,bgrpn->btgrp', c_i, h) * jnp.exp(cum_i)[..., None]
        last = cum_i[:, -1]
        w = jnp.exp(last[:, None] - cum_i)
        h = h * jnp.exp(last)[..., None, None] + jnp.einsum('bsgn,bsgrp->bgrpn', b_i, xdt_i * w[..., None])
        return h, y_i

    h0 = jnp.zeros((bsz, SSD_GROUPS, SSD_HEADS_PER_GROUP, SSD_HEAD_DIM, SSD_STATE), jnp.float32)
    swap = lambda t: jnp.moveaxis(t, 1, 0)
    h_fin, y_inter = lax.scan(step, h0, (swap(cm_c), swap(bm_c), swap(xdt), swap(cum)))
    y = y_intra + swap(y_inter)
    return y.reshape(bsz, n, SSD_HEADS, SSD_HEAD_DIM), h_fin


def ssd_recurrent(xs, bm, cm, dt, a, h0):
    bsz, n = xs.shape[:2]

    def step(h, inp):
        x_t, b_t, c_t, dt_t = inp
        h = h * jnp.exp(dt_t * a)[..., None, None] + jnp.einsum('bgn,bgrp->bgrpn', b_t, x_t * dt_t[..., None])
        return h, jnp.einsum('bgn,bgrpn->bgrp', c_t, h)

    h, y = lax.scan(step, h0, tuple(jnp.moveaxis(t, 1, 0) for t in (xs, bm, cm, dt)))
    return jnp.moveaxis(y, 0, 1).reshape(bsz, n, SSD_HEADS, SSD_HEAD_DIM), h


def ssd_mixer(x, conv_buf, h0, w_in, w_conv, b_conv, dt_bias, a_log, d_skip, g_norm, w_out):
    bsz, n, _ = x.shape
    z, xs, bm, cm, dt, new_buf = ssd_project(x, conv_buf, w_in, w_conv, b_conv, dt_bias)
    a = -jnp.exp(a_log.astype(jnp.float32)).reshape(SSD_GROUPS, SSD_HEADS_PER_GROUP)
    if h0 is None:
        y, h = ssd_chunked(xs, bm, cm, dt, a)
    else:
        h_in = h0.reshape(bsz, SSD_GROUPS, SSD_HEADS_PER_GROUP, SSD_HEAD_DIM, SSD_STATE)
        y, h = ssd_recurrent(xs, bm, cm, dt, a, h_in.astype(jnp.float32))
    y = y + xs.reshape(bsz, n, SSD_HEADS, SSD_HEAD_DIM) * d_skip.astype(jnp.float32)[:, None]
    y = y.reshape(bsz, n, SSD_INNER).astype(x.dtype) * jax.nn.silu(z)
    y = rms_norm(y.reshape(bsz, n, SSD_GROUPS, SSD_INNER // SSD_GROUPS),
                 g_norm.reshape(SSD_GROUPS, SSD_INNER // SSD_GROUPS)).reshape(bsz, n, SSD_INNER)
    h = h.reshape(bsz, SSD_HEADS, SSD_HEAD_DIM, SSD_STATE).astype(x.dtype)
    return y @ w_out, h, new_buf


def moe(x, w_router, b_router, w_gate_up, b_gate_up, w_down, b_down):
    shape = x.shape
    t = x.reshape(-1, shape[-1])
    n_pairs = t.shape[0] * TOP_K
    logits = jnp.matmul(t, w_router, preferred_element_type=jnp.float32) + b_router.astype(jnp.float32)
    top_v, top_i = lax.top_k(logits, TOP_K)
    gates = jax.nn.softmax(top_v, axis=-1).reshape(-1)
    expert = top_i.reshape(-1)
    order = jnp.argsort(expert)
    e_sorted = expert[order]
    counts = jnp.bincount(expert, length=N_EXPERTS)
    starts = jnp.cumsum(counts) - counts
    padded = (counts + MOE_BLOCK - 1) // MOE_BLOCK * MOE_BLOCK
    pad_ends = jnp.cumsum(padded)
    pad_starts = pad_ends - padded
    dest = pad_starts[e_sorted] + jnp.arange(n_pairs) - starts[e_sorted]
    n_blocks = (n_pairs + MOE_BLOCK - 1) // MOE_BLOCK + N_EXPERTS
    n_rows = n_blocks * MOE_BLOCK
    row_tok = jnp.zeros((n_rows,), jnp.int32).at[dest].set((order // TOP_K).astype(jnp.int32))
    row_gate = jnp.zeros((n_rows,), jnp.float32).at[dest].set(gates[order])
    block_expert = jnp.minimum(jnp.searchsorted(pad_ends, jnp.arange(n_blocks) * MOE_BLOCK, side='right'),
                               N_EXPERTS - 1)
    x_rows = t[row_tok].reshape(n_blocks, MOE_BLOCK, shape[-1])

    def expert_block(args):
        e, xe = args
        h = xe @ w_gate_up[e] + b_gate_up[e]
        glu, lin = jnp.split(h, 2, axis=-1)
        glu = jnp.minimum(glu, SWIGLU_LIMIT)
        lin = jnp.clip(lin, -SWIGLU_LIMIT, SWIGLU_LIMIT)
        act = glu * jax.nn.sigmoid(SWIGLU_ALPHA * glu) * (lin + 1.0)
        return act @ w_down[e] + b_down[e]

    y_rows = lax.map(expert_block, (block_expert, x_rows)).reshape(n_rows, shape[-1])
    y = jnp.zeros_like(t).at[row_tok].add(y_rows * row_gate[:, None].astype(t.dtype))
    return y.reshape(shape)


def post_blocks(x, mix, i, ln_mix_g, ln_mix_b, ln_ffn_g, ln_ffn_b,
                w_router, b_router, w_gate_up, b_gate_up, w_down, b_down):
    x = layer_norm(DEEPNORM_ALPHA * x + mix, ln_mix_g[i], ln_mix_b[i])
    ffn = moe(x, w_router[i], b_router[i], w_gate_up[i], b_gate_up[i], w_down[i], b_down[i])
    return layer_norm(DEEPNORM_ALPHA * x + ffn, ln_ffn_g[i], ln_ffn_b[i])


def setup_inputs(seed: int = 0) -> dict:
    key = jax.random.key(seed)
    keys = iter(jax.random.split(key, 64))
    f32 = jnp.float32

    def normal(shape, scale=1.0):
        return jax.random.normal(next(keys), shape, f32) * scale

    n_pages = PAST_LEN // PAGE_SIZE
    n_pool = (5 * DEC_BATCH * n_pages + 3) // 4
    page_table = jax.random.permutation(next(keys), n_pool)[:DEC_BATCH * n_pages]
    page_table = page_table.reshape(DEC_BATCH, n_pages).astype(jnp.int32)
    fox_w = 3 * FOX_HEADS * FOX_HEAD_DIM + FOX_HEADS
    gla_w = 2 * GLA_DK + 2 * GLA_DV + GLA_GATE_RANK
    ssd_w = 2 * SSD_INNER + 2 * SSD_GROUPS * SSD_STATE + SSD_HEADS
    dt0 = jnp.exp(jax.random.uniform(next(keys), (N_SSD_LAYERS, SSD_HEADS), f32,
                                     math.log(1e-3), math.log(1e-1)))
    dt_bias = dt0 + jnp.log(-jnp.expm1(-dt0))
    a_log = jnp.log(jax.random.uniform(next(keys), (N_SSD_LAYERS, SSD_HEADS), f32, 1.0, 16.0))
    beta = DEEPNORM_BETA
    return {
        'x_prompt': normal((BATCH, SEQ, D_MODEL)),
        'x_sample': normal((DEC_BATCH, DEC_SEQ, D_MODEL)),
        'cache_k_fox': normal((N_FOX_LAYERS, n_pool, PAGE_SIZE, FOX_HEADS, FOX_HEAD_DIM)),
        'cache_v_fox': normal((N_FOX_LAYERS, n_pool, PAGE_SIZE, FOX_HEADS, FOX_HEAD_DIM)),
        'cache_logf_fox': jax.nn.log_sigmoid(normal((N_FOX_LAYERS, n_pool, PAGE_SIZE, FOX_HEADS), 0.5) + 2.0),
        'page_table': page_table,
        'state_conv_short': normal((N_SHORT_LAYERS, DEC_BATCH, SHORT_WIDTH - 1, D_MODEL)),
        'state_gla': normal((N_GLA_LAYERS, DEC_BATCH, GLA_HEADS, GLA_DK_HEAD, GLA_DV_HEAD), 0.5),
        'state_ssm': normal((N_SSD_LAYERS, DEC_BATCH, SSD_HEADS, SSD_HEAD_DIM, SSD_STATE), 0.1),
        'state_conv_ssd': normal((N_SSD_LAYERS, DEC_BATCH, SSD_CONV - 1, SSD_CONV_CH)),
        'w_in_fox': normal((N_FOX_LAYERS, D_MODEL, fox_w), D_MODEL ** -0.5),
        'b_forget_fox': 2.0 + normal((N_FOX_LAYERS, FOX_HEADS), 0.5),
        'w_out_fox': normal((N_FOX_LAYERS, FOX_HEADS * FOX_HEAD_DIM, D_MODEL), beta * (FOX_HEADS * FOX_HEAD_DIM) ** -0.5),
        'w_in_short': normal((N_SHORT_LAYERS, D_MODEL, 3 * D_MODEL), D_MODEL ** -0.5),
        'w_conv_short': normal((N_SHORT_LAYERS, SHORT_WIDTH, D_MODEL), SHORT_WIDTH ** -0.5),
        'w_out_short': normal((N_SHORT_LAYERS, D_MODEL, D_MODEL), beta * D_MODEL ** -0.5),
        'w_in_gla': normal((N_GLA_LAYERS, D_MODEL, gla_w), D_MODEL ** -0.5),
        'w_gate2_gla': normal((N_GLA_LAYERS, GLA_GATE_RANK, GLA_DK), GLA_GATE_RANK ** -0.5),
        'b_gate_gla': normal((N_GLA_LAYERS, GLA_DK), 0.1),
        'g_norm_gla': 1.0 + normal((N_GLA_LAYERS, GLA_DV_HEAD), 0.02),
        'w_out_gla': normal((N_GLA_LAYERS, GLA_DV, D_MODEL), beta * GLA_DV ** -0.5),
        'w_in_ssd': normal((N_SSD_LAYERS, D_MODEL, ssd_w), D_MODEL ** -0.5),
        'w_conv_ssd': normal((N_SSD_LAYERS, SSD_CONV, SSD_CONV_CH), SSD_CONV ** -0.5),
        'b_conv_ssd': normal((N_SSD_LAYERS, SSD_CONV_CH), 0.02),
        'dt_bias_ssd': dt_bias,
        'a_log_ssd': a_log,
        'd_skip_ssd': 1.0 + normal((N_SSD_LAYERS, SSD_HEADS), 0.1),
        'g_norm_ssd': 1.0 + normal((N_SSD_LAYERS, SSD_INNER), 0.02),
        'w_out_ssd': normal((N_SSD_LAYERS, SSD_INNER, D_MODEL), beta * SSD_INNER ** -0.5),
        'ln_mix_g': 1.0 + normal((DEPTH, D_MODEL), 0.02),
        'ln_mix_b': normal((DEPTH, D_MODEL), 0.02),
        'ln_ffn_g': 1.0 + normal((DEPTH, D_MODEL), 0.02),
        'ln_ffn_b': normal((DEPTH, D_MODEL), 0.02),
        'w_router': normal((DEPTH, D_MODEL, N_EXPERTS), D_MODEL ** -0.5),
        'b_router': normal((DEPTH, N_EXPERTS), 0.01),
        'w_gate_up': normal((DEPTH, N_EXPERTS, D_MODEL, 2 * D_FF), D_MODEL ** -0.5),
        'b_gate_up': normal((DEPTH, N_EXPERTS, 2 * D_FF), 0.02),
        'w_down': normal((DEPTH, N_EXPERTS, D_FF, D_MODEL), beta * D_FF ** -0.5),
        'b_down': normal((DEPTH, N_EXPERTS, D_MODEL), 0.01),
    }


def reference(x_prompt, x_sample, cache_k_fox, cache_v_fox, cache_logf_fox, page_table,
              state_conv_short, state_gla, state_ssm, state_conv_ssd,
              w_in_fox, b_forget_fox, w_out_fox,
              w_in_short, w_conv_short, w_out_short,
              w_in_gla, w_gate2_gla, b_gate_gla, g_norm_gla, w_out_gla,
              w_in_ssd, w_conv_ssd, b_conv_ssd, dt_bias_ssd, a_log_ssd, d_skip_ssd, g_norm_ssd, w_out_ssd,
              ln_mix_g, ln_mix_b, ln_ffn_g, ln_ffn_b,
              w_router, b_router, w_gate_up, b_gate_up, w_down, b_down):
    xp, xs = x_prompt, x_sample
    bp = xp.shape[0]
    kf_p, vf_p, lf_p, kf_s, vf_s, lf_s = [], [], [], [], [], []
    cs_p, cs_s, sg_p, sg_s, sm_p, sm_s, cd_p, cd_s = [], [], [], [], [], [], [], []
    for i in range(DEPTH):
        kind, j = i % N_MIXERS, i // N_MIXERS
        if kind == 0:
            mp, k, v, lf = fox_prompt(xp, w_in_fox[j], b_forget_fox[j], w_out_fox[j])
            kf_p.append(k); vf_p.append(v); lf_p.append(lf)
            ms, k, v, lf = fox_sample(xs, cache_k_fox, cache_v_fox, cache_logf_fox, j, page_table,
                                      w_in_fox[j], b_forget_fox[j], w_out_fox[j])
            kf_s.append(k); vf_s.append(v); lf_s.append(lf)
        elif kind == 1:
            zero_buf = jnp.zeros((bp, SHORT_WIDTH - 1, D_MODEL), xp.dtype)
            mp, buf = short_conv(xp, zero_buf, w_in_short[j], w_conv_short[j], w_out_short[j])
            cs_p.append(buf)
            ms, buf = short_conv(xs, state_conv_short[j], w_in_short[j], w_conv_short[j], w_out_short[j])
            cs_s.append(buf)
        elif kind == 2:
            mp, s = gla_mixer(xp, None, w_in_gla[j], w_gate2_gla[j], b_gate_gla[j], g_norm_gla[j], w_out_gla[j])
            sg_p.append(s)
            ms, s = gla_mixer(xs, state_gla[j], w_in_gla[j], w_gate2_gla[j], b_gate_gla[j], g_norm_gla[j], w_out_gla[j])
            sg_s.append(s)
        else:
            zero_buf = jnp.zeros((bp, SSD_CONV - 1, SSD_CONV_CH), xp.dtype)
            mp, h, buf = ssd_mixer(xp, zero_buf, None, w_in_ssd[j], w_conv_ssd[j], b_conv_ssd[j],
                                   dt_bias_ssd[j], a_log_ssd[j], d_skip_ssd[j], g_norm_ssd[j], w_out_ssd[j])
            sm_p.append(h); cd_p.append(buf)
            ms, h, buf = ssd_mixer(xs, state_conv_ssd[j], state_ssm[j], w_in_ssd[j], w_conv_ssd[j], b_conv_ssd[j],
                                   dt_bias_ssd[j], a_log_ssd[j], d_skip_ssd[j], g_norm_ssd[j], w_out_ssd[j])
            sm_s.append(h); cd_s.append(buf)
        xp = post_blocks(xp, mp, i, ln_mix_g, ln_mix_b, ln_ffn_g, ln_ffn_b,
                         w_router, b_router, w_gate_up, b_gate_up, w_down, b_down)
        xs = post_blocks(xs, ms, i, ln_mix_g, ln_mix_b, ln_ffn_g, ln_ffn_b,
                         w_router, b_router, w_gate_up, b_gate_up, w_down, b_down)
    return (xp, xs,
            jnp.stack(kf_p), jnp.stack(vf_p), jnp.stack(lf_p), jnp.stack(cs_p),
            jnp.stack(sg_p), jnp.stack(sm_p), jnp.stack(cd_p),
            jnp.stack(kf_s), jnp.stack(vf_s), jnp.stack(lf_s), jnp.stack(cs_s),
            jnp.stack(sg_s), jnp.stack(sm_s), jnp.stack(cd_s))
```

```python
import functools
import math

import jax
import jax.numpy as jnp
from jax import lax
from jax.experimental import pallas as pl
from jax.experimental.pallas import tpu as pltpu

F32 = jnp.float32
BF16 = jnp.bfloat16
I32 = jnp.int32

LANES = 128
SUBLANES = 8
VMEM_LIMIT = 56 * 1024 * 1024

LN_EPS = 1e-5
RMS_EPS = 1e-6
TOP_K = 4
SWIGLU_LIMIT = 7.0
SWIGLU_ALPHA = 1.702
GLA_TAU = 16.0
GLA_CHUNK = 64
SSD_GROUPS = 4
MOE_BLOCK = 256


def _params(semantics):
    return pltpu.CompilerParams(dimension_semantics=semantics, vmem_limit_bytes=VMEM_LIMIT)


def _tile(n, candidates):
    for c in candidates:
        if n % c == 0:
            return c
    raise ValueError(f"no tile for {n} among {candidates}")


def _round_up(n, m):
    return (n + m - 1) // m * m


def _log_sigmoid(z):
    return jnp.minimum(z, 0.0) - jnp.log(1.0 + jnp.exp(-jnp.abs(z)))


def _softplus(z):
    return jnp.maximum(z, 0.0) + jnp.log(1.0 + jnp.exp(-jnp.abs(z)))


def _silu(z):
    return z * jax.nn.sigmoid(z)


def _split3(x):
    hi = x.astype(BF16)
    r = x - hi.astype(F32)
    mid = r.astype(BF16)
    lo = (r - mid.astype(F32)).astype(BF16)
    return hi, mid, lo


def _dot_exact_lhs(a01, x):
    return sum(jnp.dot(a01, p, preferred_element_type=F32) for p in _split3(x))


def _dot_exact_rhs(x, a01):
    return sum(jnp.dot(p, a01, preferred_element_type=F32) for p in _split3(x))


def _round_bf16(x):
    return x.astype(BF16).astype(F32)


def _nt(a, b):
    return lax.dot_general(a, b, (((1,), (1,)), ((), ())), preferred_element_type=F32)


def _tn(a, b):
    return lax.dot_general(a, b, (((0,), (0,)), ((), ())), preferred_element_type=F32)


def _proj_kernel(x_ref, w_ref, *out_refs, segs, chunk):
    xb = x_ref[...].astype(BF16)
    for o_ref, (start, width) in zip(out_refs, segs):
        for c in range(0, width, chunk):
            cw = min(chunk, width - c)
            o_ref[:, c:c + cw] = jnp.dot(
                xb, w_ref[:, start + c:start + c + cw], preferred_element_type=F32).astype(o_ref.dtype)


def _proj(x, w, segs, tm):
    t, d = x.shape
    n = w.shape[1]
    assert all(s % LANES == 0 and wd % LANES == 0 for s, wd in segs)
    return pl.pallas_call(
        functools.partial(_proj_kernel, segs=tuple(segs), chunk=512),
        grid=(t // tm,),
        in_specs=[pl.BlockSpec((tm, d), lambda i: (i, 0)),
                  pl.BlockSpec((d, n), lambda i: (0, 0))],
        out_specs=[pl.BlockSpec((tm, wd), lambda i: (i, 0)) for _, wd in segs],
        out_shape=[jax.ShapeDtypeStruct((t, wd), F32) for _, wd in segs],
        compiler_params=_params(("parallel",)),
        name="proj",
    )(x, w)


def _pad_cols(w, n):
    return jnp.pad(w, ((0, 0), (0, n - w.shape[1])))


def _layer_norm_rows(z, g, b):
    mu = jnp.mean(z, axis=-1, keepdims=True)
    zc = z - mu
    var = jnp.mean(zc * zc, axis=-1, keepdims=True)
    return zc * lax.rsqrt(var + LN_EPS) * g + b


def _mix_ln_router_kernel(y_ref, w_ref, x_ref, g_ref, b_ref, wr_ref, br_ref,
                          x1_ref, topi_ref, gate_ref, *, alpha, n_experts):
    mix = jnp.dot(y_ref[...].astype(BF16), w_ref[...], preferred_element_type=F32)
    x1 = _layer_norm_rows(alpha * x_ref[...] + mix, g_ref[...], b_ref[...])
    x1_ref[...] = x1
    logits = jnp.dot(x1.astype(BF16), wr_ref[...], preferred_element_type=F32) + br_ref[...]
    lane = lax.broadcasted_iota(I32, logits.shape, 1)
    neg_inf = jnp.float32(-jnp.inf)
    cur = jnp.where(lane < n_experts, logits, neg_inf)
    topi = jnp.zeros(logits.shape, I32)
    vals = []
    for k in range(TOP_K):
        m = jnp.max(cur, axis=1, keepdims=True)
        idx = jnp.min(jnp.where(cur == m, lane, LANES), axis=1, keepdims=True)
        vals.append(m)
        topi = jnp.where(lane == k, idx, topi)
        cur = jnp.where(lane == idx, neg_inf, cur)
    es = [jnp.exp(v - vals[0]) for v in vals]
    inv = 1.0 / sum(es)
    gate = jnp.zeros(logits.shape, F32)
    for k in range(TOP_K):
        gate = jnp.where(lane == k, es[k] * inv, gate)
    topi_ref[...] = topi
    gate_ref[...] = gate


def _mix_ln_router(y, w_out, x, ln_g, ln_b, w_r, b_r, alpha, n_experts, tm):
    t, d = x.shape
    kdim = y.shape[1]
    row = lambda i: (i, 0)
    full = lambda i: (0, 0)
    return pl.pallas_call(
        functools.partial(_mix_ln_router_kernel, alpha=alpha, n_experts=n_experts),
        grid=(t // tm,),
        in_specs=[pl.BlockSpec((tm, kdim), row), pl.BlockSpec((kdim, d), full),
                  pl.BlockSpec((tm, d), row), pl.BlockSpec((1, d), full), pl.BlockSpec((1, d), full),
                  pl.BlockSpec((d, LANES), full), pl.BlockSpec((1, LANES), full)],
        out_specs=[pl.BlockSpec((tm, d), row), pl.BlockSpec((tm, LANES), row),
                   pl.BlockSpec((tm, LANES), row)],
        out_shape=[jax.ShapeDtypeStruct((t, d), F32), jax.ShapeDtypeStruct((t, LANES), I32),
                   jax.ShapeDtypeStruct((t, LANES), F32)],
        compiler_params=_params(("parallel",)),
        name="mix_ln_router",
    )(y, w_out, x, ln_g, ln_b, w_r, b_r)


def _route_tables(topi, n_experts, bm):
    t = topi.shape[0]
    n_pairs = t * TOP_K
    expert = topi[:, :TOP_K].reshape(-1)
    order = jnp.argsort(expert).astype(I32)
    e_sorted = expert[order]
    counts = jnp.sum((expert[:, None] == jnp.arange(n_experts, dtype=I32)[None, :]).astype(I32), axis=0)
    starts = jnp.cumsum(counts) - counts
    padded = (counts + bm - 1) // bm * bm
    pad_ends = jnp.cumsum(padded)
    pad_starts = pad_ends - padded
    dest = (pad_starts[e_sorted] + jnp.arange(n_pairs, dtype=I32) - starts[e_sorted]).astype(I32)
    n_blocks = (n_pairs + bm - 1) // bm + n_experts
    row_tok = jnp.zeros((n_blocks * bm,), I32).at[dest].set(order // TOP_K)
    pos = jnp.zeros((n_pairs,), I32).at[order].set(dest)
    block_expert = jnp.minimum(
        jnp.searchsorted(pad_ends, jnp.arange(n_blocks, dtype=I32) * bm, side="right"),
        n_experts - 1).astype(I32)
    n_used = (pad_ends[-1] // bm).astype(I32).reshape(1)
    return block_expert, row_tok, n_used, pos, n_blocks


def _moe_ffn_kernel(be_ref, rt_ref, nu_ref, x_hbm, wgu_ref, bgu_ref, wd_ref, bd_ref,
                    y_ref, xbuf, sem, wgu_b, wd_b, *, bm, d_ff):
    i = pl.program_id(0)
    nb = pl.num_programs(0)
    n_used = nu_ref[0]
    slot = i % 2

    def gather_rows(blk, dst_slot):
        def body(r, carry):
            tok = rt_ref[blk * bm + r]
            pltpu.make_async_copy(x_hbm.at[pl.ds(tok, 1), :],
                                  xbuf.at[dst_slot, pl.ds(r, 1), :], sem.at[dst_slot]).start()
            return carry
        lax.fori_loop(0, bm, body, 0)

    @pl.when(jnp.logical_and(i == 0, n_used > 0))
    def _():
        gather_rows(0, 0)

    @pl.when(i + 1 < jnp.minimum(nb, n_used))
    def _():
        gather_rows(i + 1, 1 - slot)

    expert_changed = jnp.logical_or(i == 0, be_ref[jnp.maximum(i - 1, 0)] != be_ref[i])

    @pl.when(jnp.logical_and(i < n_used, expert_changed))
    def _():
        wgu_b[...] = wgu_ref[0, 0].astype(BF16)
        wd_b[...] = wd_ref[0, 0].astype(BF16)

    @pl.when(i < n_used)
    def _():
        pltpu.make_async_copy(x_hbm.at[pl.ds(0, bm), :], xbuf.at[slot], sem.at[slot]).wait()
        xb = xbuf[slot].astype(BF16)
        h = jnp.dot(xb, wgu_b[...], preferred_element_type=F32) + bgu_ref[0, 0]
        glu = jnp.minimum(h[:, :d_ff], SWIGLU_LIMIT)
        lin = jnp.clip(h[:, d_ff:], -SWIGLU_LIMIT, SWIGLU_LIMIT)
        act = glu * jax.nn.sigmoid(SWIGLU_ALPHA * glu) * (lin + 1.0)
        y_ref[...] = jnp.dot(act.astype(BF16), wd_b[...], preferred_element_type=F32) + bd_ref[0, 0]

    @pl.when(i >= n_used)
    def _():
        y_ref[...] = jnp.zeros(y_ref.shape, y_ref.dtype)


def _moe_ffn(x1, block_expert, row_tok, n_used, n_blocks, layer, w_gate_up, b_gate_up, w_down, b_down, bm):
    t, d = x1.shape
    depth, n_e, _, two_f = w_gate_up.shape
    d_ff = two_f // 2
    wmap = lambda i, be, rt, nu: (layer, be[i], 0, 0)
    grid_spec = pltpu.PrefetchScalarGridSpec(
        num_scalar_prefetch=3,
        grid=(n_blocks,),
        in_specs=[pl.BlockSpec(memory_space=pl.ANY),
                  pl.BlockSpec((1, 1, d, two_f), wmap),
                  pl.BlockSpec((1, 1, 1, two_f), wmap),
                  pl.BlockSpec((1, 1, d_ff, d), wmap),
                  pl.BlockSpec((1, 1, 1, d), wmap)],
        out_specs=pl.BlockSpec((bm, d), lambda i, be, rt, nu: (i, 0)),
        scratch_shapes=[pltpu.VMEM((2, bm, d), F32), pltpu.SemaphoreType.DMA((2,)),
                        pltpu.VMEM((d, two_f), BF16), pltpu.VMEM((d_ff, d), BF16)],
    )
    return pl.pallas_call(
        functools.partial(_moe_ffn_kernel, bm=bm, d_ff=d_ff),
        grid_spec=grid_spec,
        out_shape=jax.ShapeDtypeStruct((n_blocks * bm, d), F32),
        compiler_params=_params(("arbitrary",)),
        name="moe_ffn",
    )(block_expert, row_tok, n_used, x1, w_gate_up, b_gate_up.reshape(depth, n_e, 1, two_f),
      w_down, b_down.reshape(depth, n_e, 1, d))


def _moe_combine_kernel(pos_ref, y_hbm, gate_ref, x_ref, g_ref, b_ref, o_ref, buf, sem, *, tm, alpha):
    i = pl.program_id(0)
    nb = pl.num_programs(0)
    slot = i % 2

    def gather_rows(tile, dst_slot):
        def body(r, carry):
            for k in range(TOP_K):
                p = pos_ref[(tile * tm + r) * TOP_K + k]
                pltpu.make_async_copy(y_hbm.at[pl.ds(p, 1), :],
                                      buf.at[dst_slot, pl.ds(k * tm + r, 1), :], sem.at[dst_slot]).start()
            return carry
        lax.fori_loop(0, tm, body, 0)

    @pl.when(i == 0)
    def _():
        gather_rows(0, 0)

    @pl.when(i + 1 < nb)
    def _():
        gather_rows(i + 1, 1 - slot)

    pltpu.make_async_copy(y_hbm.at[pl.ds(0, TOP_K * tm), :], buf.at[slot], sem.at[slot]).wait()
    gate = gate_ref[...]
    ffn = gate[:, 0:1] * buf[slot, pl.ds(0, tm), :]
    for k in range(1, TOP_K):
        ffn = ffn + gate[:, k:k + 1] * buf[slot, pl.ds(k * tm, tm), :]
    o_ref[...] = _layer_norm_rows(alpha * x_ref[...] + ffn, g_ref[...], b_ref[...])


def _moe_combine(y_rows, pos, gates, x1, ln_g, ln_b, alpha, tm):
    t, d = x1.shape
    row = lambda i, p: (i, 0)
    full = lambda i, p: (0, 0)
    grid_spec = pltpu.PrefetchScalarGridSpec(
        num_scalar_prefetch=1,
        grid=(t // tm,),
        in_specs=[pl.BlockSpec(memory_space=pl.ANY), pl.BlockSpec((tm, LANES), row),
                  pl.BlockSpec((tm, d), row), pl.BlockSpec((1, d), full), pl.BlockSpec((1, d), full)],
        out_specs=pl.BlockSpec((tm, d), row),
        scratch_shapes=[pltpu.VMEM((2, TOP_K * tm, d), F32), pltpu.SemaphoreType.DMA((2,))],
    )
    return pl.pallas_call(
        functools.partial(_moe_combine_kernel, tm=tm, alpha=alpha),
        grid_spec=grid_spec,
        out_shape=jax.ShapeDtypeStruct((t, d), F32),
        compiler_params=_params(("arbitrary",)),
        name="moe_combine",
    )(pos, y_rows, gates, x1, ln_g, ln_b)


def _post_blocks(x, y_mix, w_out, i, alpha, p):
    n_experts = p["w_router"].shape[-1]
    d = x.shape[1]
    tm = _tile(x.shape[0], (384, 256, 128))
    w_r = _pad_cols(p["w_router"][i], LANES).astype(BF16)
    b_r = _pad_cols(p["b_router"][i][None, :].astype(F32), LANES)
    x1, topi, gates = _mix_ln_router(
        y_mix, w_out.astype(BF16), x, p["ln_mix_g"][i].reshape(1, d), p["ln_mix_b"][i].reshape(1, d),
        w_r, b_r, alpha, n_experts, tm)
    block_expert, row_tok, n_used, pos, n_blocks = _route_tables(topi, n_experts, MOE_BLOCK)
    y_rows = _moe_ffn(x1, block_expert, row_tok, n_used, n_blocks, i, p["w_gate_up"], p["b_gate_up"],
                      p["w_down"], p["b_down"], MOE_BLOCK)
    return _moe_combine(y_rows, pos, gates, x1, p["ln_ffn_g"][i].reshape(1, d),
                        p["ln_ffn_b"][i].reshape(1, d), alpha, _tile(x.shape[0], (128,)))


def _fox_gate_kernel(f_ref, bf_ref, lf_ref, c_ref, carry, *, tl):
    @pl.when(pl.program_id(1) == 0)
    def _():
        carry[...] = jnp.zeros(carry.shape, F32)

    lf = _log_sigmoid(f_ref[0] + bf_ref[...])
    lf_ref[0] = lf
    row = lax.broadcasted_iota(I32, (tl, tl), 0)
    col = lax.broadcasted_iota(I32, (tl, tl), 1)
    tri = jnp.where(row >= col, 1.0, 0.0).astype(BF16)
    c = _dot_exact_lhs(tri, lf) + carry[0:1, :]
    c_ref[0] = c
    carry[0:1, :] = c[tl - 1:tl, :]


def _fox_gate(f_raw, b_f, tl):
    bsz, n, _ = f_raw.shape
    blk = pl.BlockSpec((1, tl, LANES), lambda b, l: (b, l, 0))
    return pl.pallas_call(
        functools.partial(_fox_gate_kernel, tl=tl),
        grid=(bsz, n // tl),
        in_specs=[blk, pl.BlockSpec((1, LANES), lambda b, l: (0, 0))],
        out_specs=[blk, blk],
        out_shape=[jax.ShapeDtypeStruct(f_raw.shape, F32)] * 2,
        scratch_shapes=[pltpu.VMEM((SUBLANES, LANES), F32)],
        compiler_params=_params(("parallel", "arbitrary")),
        name="fox_gate",
    )(f_raw, b_f)


def _fox_attn_kernel(q_ref, k_ref, v_ref, cc_ref, cr_ref, o_ref, kb, vb, s_buf, m_s, l_s, acc, *, scale, tq):
    head = pl.program_id(1)
    qi = pl.program_id(2)

    @pl.when(qi == 0)
    def _():
        kb[...] = k_ref[...].astype(BF16)
        vb[...] = v_ref[...].astype(BF16)

    qs = (q_ref[...] * scale).astype(BF16)
    lane = lax.broadcasted_iota(I32, (tq, LANES), 1)
    cq = jnp.sum(jnp.where(lane == head, cc_ref[0], 0.0), axis=1, keepdims=True)
    ahead = lax.broadcasted_iota(I32, (tq, tq), 1) - lax.broadcasted_iota(I32, (tq, tq), 0)
    m_s[...] = jnp.full(m_s.shape, -jnp.inf, F32)
    l_s[...] = jnp.zeros(l_s.shape, F32)
    acc[...] = jnp.zeros(acc.shape, F32)

    def scores(kj, carry):
        rows = pl.ds(pl.multiple_of(kj * tq, tq), tq)
        s = _nt(qs, kb[rows, :]) + (cq - cr_ref[0, 0, pl.ds(kj, 1), :])
        s = jnp.where(ahead <= (qi - kj) * tq, s, -jnp.inf)
        s_buf[kj] = s
        m_prev = m_s[...]
        m_new = jnp.maximum(m_prev, jnp.max(s, axis=1, keepdims=True))
        l_s[...] = jnp.exp(m_prev - m_new) * l_s[...] + jnp.sum(jnp.exp(s - m_new), axis=1, keepdims=True)
        m_s[...] = m_new
        return carry

    lax.fori_loop(0, qi + 1, scores, 0)
    m_fin = m_s[...]
    inv_l = 1.0 / l_s[...]

    def values(kj, carry):
        rows = pl.ds(pl.multiple_of(kj * tq, tq), tq)
        p = (jnp.exp(s_buf[kj] - m_fin) * inv_l).astype(BF16)
        acc[...] += jnp.dot(p, vb[rows, :], preferred_element_type=F32)
        return carry

    lax.fori_loop(0, qi + 1, values, 0)
    o_ref[...] = acc[...].astype(o_ref.dtype)


def _fox_attn_prompt(q, k, v, c_col, c_row, bsz, n, heads, t_all, tq):
    dh = q.shape[1] // heads
    nq = n // tq
    qmap = lambda b, h, i: (b * nq + i, h)
    kmap = lambda b, h, i: (b, h)
    return pl.pallas_call(
        functools.partial(_fox_attn_kernel, scale=dh ** -0.5, tq=tq),
        grid=(bsz, heads, nq),
        in_specs=[pl.BlockSpec((tq, dh), qmap), pl.BlockSpec((n, dh), kmap), pl.BlockSpec((n, dh), kmap),
                  pl.BlockSpec((1, tq, LANES), lambda b, h, i: (b, i, 0)),
                  pl.BlockSpec((1, 1, nq, tq), lambda b, h, i: (b, h, 0, 0))],
        out_specs=pl.BlockSpec((tq, dh), qmap),
        out_shape=jax.ShapeDtypeStruct((t_all, heads * dh), BF16),
        scratch_shapes=[pltpu.VMEM((n, dh), BF16), pltpu.VMEM((n, dh), BF16), pltpu.VMEM((nq, tq, tq), F32),
                        pltpu.VMEM((tq, 1), F32), pltpu.VMEM((tq, 1), F32), pltpu.VMEM((tq, dh), F32)],
        compiler_params=_params(("parallel", "parallel", "arbitrary")),
        name="fox_attn_prompt",
    )(q, k, v, c_col, c_row)


def _fox_decode_kernel(pt_ref, q_ref, kn_ref, vn_ref, f_ref, bf_ref, kp_ref, vp_ref, lfp_ref,
                       o_ref, lfo_ref, qbd, carry, m_s, l_s, acc, *, scale, heads, dh, page):
    j = pl.program_id(1)
    n_pages = pl.num_programs(1)
    width = heads * dh
    sub = lax.broadcasted_iota(I32, (heads, width), 0)
    lane = lax.broadcasted_iota(I32, (heads, width), 1)
    own = (lane // dh) == sub

    @pl.when(j == 0)
    def _():
        lf_row = _log_sigmoid(f_ref[0] + bf_ref[...])
        lfo_ref[0] = lf_row
        sub_s = lax.broadcasted_iota(I32, (heads, LANES), 0)
        lane_s = lax.broadcasted_iota(I32, (heads, LANES), 1)
        carry[...] = jnp.sum(jnp.where(lane_s == sub_s, lf_row, 0.0), axis=1, keepdims=True)
        q_row = q_ref[0] * scale
        qbd[...] = jnp.where(own, q_row, 0.0).astype(BF16)
        m_s[...] = jnp.sum(jnp.where(own, _round_bf16(q_row) * _round_bf16(kn_ref[0]), 0.0), axis=1, keepdims=True)
        l_s[...] = jnp.ones(l_s.shape, F32)
        acc[...] = jnp.broadcast_to(_round_bf16(vn_ref[0]), acc.shape)

    lf = lfp_ref[0, 0]
    r_i = lax.broadcasted_iota(I32, (page, page), 0)
    c_i = lax.broadcasted_iota(I32, (page, page), 1)
    newer = jnp.where(r_i > c_i, 1.0, 0.0).astype(BF16)
    s = _nt(qbd[...], kp_ref[0, 0].astype(BF16)) + carry[...] + _dot_exact_rhs(lf, newer)
    carry[...] = carry[...] + jnp.sum(lf, axis=1, keepdims=True)
    m_prev = m_s[...]
    m_new = jnp.maximum(m_prev, jnp.max(s, axis=1, keepdims=True))
    alpha = jnp.exp(m_prev - m_new)
    p = jnp.exp(s - m_new)
    l_s[...] = alpha * l_s[...] + jnp.sum(p, axis=1, keepdims=True)
    acc[...] = alpha * acc[...] + jnp.dot(p.astype(BF16), vp_ref[0, 0].astype(BF16),
                                          preferred_element_type=F32)
    m_s[...] = m_new

    @pl.when(j == n_pages - 1)
    def _():
        o = jnp.sum(jnp.where(own, acc[...] / l_s[...], 0.0), axis=0, keepdims=True)
        o_ref[0] = o.astype(o_ref.dtype)


def _fox_decode(q, k_new, v_new, f_raw, b_f, k_pool, v_pool, lf_pool_t, page_table, layer, heads):
    n_seq, _, width = q.shape
    dh = width // heads
    page = k_pool.shape[2]
    n_pages = page_table.shape[1]
    row = lambda b, j, pt: (b, 0, 0)
    pmap = lambda b, j, pt: (layer, pt[b * n_pages + (n_pages - 1 - j)], 0, 0)
    grid_spec = pltpu.PrefetchScalarGridSpec(
        num_scalar_prefetch=1,
        grid=(n_seq, n_pages),
        in_specs=[pl.BlockSpec((1, 1, width), row), pl.BlockSpec((1, 1, width), row),
                  pl.BlockSpec((1, 1, width), row), pl.BlockSpec((1, 1, LANES), row),
                  pl.BlockSpec((1, LANES), lambda b, j, pt: (0, 0)),
                  pl.BlockSpec((1, 1, page, width), pmap), pl.BlockSpec((1, 1, page, width), pmap),
                  pl.BlockSpec((1, 1, heads, page), pmap)],
        out_specs=[pl.BlockSpec((1, 1, width), row), pl.BlockSpec((1, 1, LANES), row)],
        scratch_shapes=[pltpu.VMEM((heads, width), BF16), pltpu.VMEM((heads, 1), F32),
                        pltpu.VMEM((heads, 1), F32), pltpu.VMEM((heads, 1), F32),
                        pltpu.VMEM((heads, width), F32)],
    )
    return pl.pallas_call(
        functools.partial(_fox_decode_kernel, scale=dh ** -0.5, heads=heads, dh=dh, page=page),
        grid_spec=grid_spec,
        out_shape=[jax.ShapeDtypeStruct((n_seq, 1, width), BF16), jax.ShapeDtypeStruct((n_seq, 1, LANES), F32)],
        compiler_params=_params(("parallel", "arbitrary")),
        name="fox_decode",
    )(page_table.reshape(-1), q, k_new, v_new, f_raw, b_f, k_pool, v_pool, lf_pool_t)


def _fox_layer(x, j, bsz, n, tp, p):
    t_all, d = x.shape
    heads = p["cache_k_fox"].shape[3]
    dh = p["cache_k_fox"].shape[4]
    hd = heads * dh
    page = p["cache_k_fox"].shape[2]
    n_layers, n_pool = p["cache_k_fox"].shape[:2]
    w = _pad_cols(p["w_in_fox"][j], 3 * hd + LANES).astype(BF16)
    q, k, v, f_raw = _proj(x, w, [(0, hd), (hd, hd), (2 * hd, hd), (3 * hd, LANES)],
                           _tile(t_all, (384, 256, 128)))
    b_f = _pad_cols(p["b_forget_fox"][j][None, :], LANES)
    lf_p, c_p = _fox_gate(f_raw[:tp].reshape(bsz, n, LANES), b_f, _tile(n, (512, 256, 128)))
    tq = _tile(n, (512, 256, 128))
    c_row = c_p[:, :, :heads].transpose(0, 2, 1).reshape(bsz, heads, n // tq, tq)
    o = _fox_attn_prompt(q, k, v, c_p, c_row, bsz, n, heads, t_all, tq)
    n_seq = t_all - tp
    s3 = lambda a: a[tp:].reshape(n_seq, 1, a.shape[1])
    lf_pool_t = p["cache_logf_fox"].transpose(0, 1, 3, 2)
    o_s, lf_s = _fox_decode(s3(q), s3(k), s3(v), s3(f_raw), b_f,
                            p["cache_k_fox"].reshape(n_layers, n_pool, page, hd),
                            p["cache_v_fox"].reshape(n_layers, n_pool, page, hd),
                            lf_pool_t, p["page_table"], j, heads)
    o = lax.dynamic_update_slice(o, o_s.reshape(n_seq, hd), (tp, 0))
    outs = dict(
        k_p=k[:tp].reshape(bsz, n, heads, dh), v_p=v[:tp].reshape(bsz, n, heads, dh),
        lf_p=lf_p[:, :, :heads],
        k_s=k[tp:].reshape(n_seq, 1, heads, dh), v_s=v[tp:].reshape(n_seq, 1, heads, dh),
        lf_s=lf_s[:, :, :heads])
    return o, outs


HALO = SUBLANES


def _causal_taps(hist, u, w_ref, first_tile, tl):
    width = w_ref.shape[0]
    u = _round_bf16(u)

    @pl.when(first_tile)
    def _():
        hist[0:HALO, :] = jnp.zeros((HALO, hist.shape[1]), F32)

    @pl.when(jnp.logical_not(first_tile))
    def _():
        hist[0:HALO, :] = hist[tl:tl + HALO, :]

    hist[HALO:HALO + tl, :] = u
    out = w_ref[width - 1:width, :] * u
    for k in range(width - 1):
        shift = width - 1 - k
        out = out + w_ref[k:k + 1, :] * hist[HALO - shift:HALO - shift + tl, :]
    return out


def _short_prompt_kernel(b_ref, c_ref, h_ref, w_ref, y_ref, st_ref, hist, *, tl):
    u = c_ref[...] * h_ref[...]
    conv = _causal_taps(hist, u, w_ref, pl.program_id(1) == 0, tl)
    y_ref[...] = (b_ref[...] * conv).astype(y_ref.dtype)
    keep = w_ref.shape[0] - 1
    st_ref[0] = u[tl - keep:tl, :]


def _short_prompt(bg, cg, hg, w_conv, bsz, n, t_all, tl):
    d = bg.shape[1]
    width = w_conv.shape[0]
    nl = n // tl
    row = lambda b, l: (b * nl + l, 0)
    return pl.pallas_call(
        functools.partial(_short_prompt_kernel, tl=tl),
        grid=(bsz, nl),
        in_specs=[pl.BlockSpec((tl, d), row)] * 3 + [pl.BlockSpec((width, d), lambda b, l: (0, 0))],
        out_specs=[pl.BlockSpec((tl, d), row), pl.BlockSpec((1, width - 1, d), lambda b, l: (b, 0, 0))],
        out_shape=[jax.ShapeDtypeStruct((t_all, d), BF16), jax.ShapeDtypeStruct((bsz, width - 1, d), F32)],
        scratch_shapes=[pltpu.VMEM((HALO + tl, d), F32)],
        compiler_params=_params(("parallel", "arbitrary")),
        name="short_prompt",
    )(bg, cg, hg, w_conv)


def _short_decode_kernel(b_ref, c_ref, h_ref, s_ref, w_ref, y_ref, u_ref):
    u = c_ref[...] * h_ref[...]
    width = w_ref.shape[0]
    conv = w_ref[width - 1:width, :] * _round_bf16(u)
    for k in range(width - 1):
        conv = conv + w_ref[k:k + 1, :] * _round_bf16(s_ref[k])
    y_ref[...] = (b_ref[...] * conv).astype(y_ref.dtype)
    u_ref[...] = u


def _short_decode(bg, cg, hg, state_t, w_conv):
    n_seq, d = bg.shape
    return pl.pallas_call(
        _short_decode_kernel,
        out_shape=[jax.ShapeDtypeStruct((n_seq, d), BF16), jax.ShapeDtypeStruct((n_seq, d), F32)],
        compiler_params=pltpu.CompilerParams(vmem_limit_bytes=VMEM_LIMIT),
        name="short_decode",
    )(bg, cg, hg, state_t, w_conv)


def _short_layer(x, j, bsz, n, tp, p):
    t_all, d = x.shape
    w = p["w_in_short"][j].astype(BF16)
    bg, cg, hg = _proj(x, w, [(0, d), (d, d), (2 * d, d)], _tile(t_all, (384, 256, 128)))
    w_conv = p["w_conv_short"][j]
    y, st_p = _short_prompt(bg, cg, hg, w_conv, bsz, n, t_all, _tile(n, (512, 256, 128)))
    state = p["state_conv_short"][j]
    y_s, u_s = _short_decode(bg[tp:], cg[tp:], hg[tp:], state.transpose(1, 0, 2), w_conv)
    y = lax.dynamic_update_slice(y, y_s, (tp, 0))
    st_s = jnp.concatenate([state[:, 1:], u_s[:, None, :]], axis=1)
    return y, dict(conv_p=st_p, conv_s=st_s)


def _gla_gate_kernel(r_ref, w_ref, b_ref, o_ref):
    z = jnp.dot(r_ref[...].astype(BF16), w_ref[...], preferred_element_type=F32) + b_ref[...]
    o_ref[...] = _log_sigmoid(z) * (1.0 / GLA_TAU)


def _gla_gate(r, w_gate, b_gate, tm):
    t = r.shape[0]
    dk = w_gate.shape[1]
    row = lambda i: (i, 0)
    full = lambda i: (0, 0)
    return pl.pallas_call(
        _gla_gate_kernel,
        grid=(t // tm,),
        in_specs=[pl.BlockSpec((tm, LANES), row), pl.BlockSpec((LANES, dk), full), pl.BlockSpec((1, dk), full)],
        out_specs=pl.BlockSpec((tm, dk), row),
        out_shape=jax.ShapeDtypeStruct((t, dk), F32),
        compiler_params=_params(("parallel",)),
        name="gla_gate",
    )(r, w_gate, b_gate)


def _rms_gate(o, g_norm, gate):
    on = o * lax.rsqrt(jnp.mean(o * o, axis=-1, keepdims=True) + RMS_EPS) * g_norm
    return on * _silu(gate)


def _gla_prompt_kernel(q_ref, k_ref, v_ref, g_ref, la_ref, gn_ref, o_ref, st_ref, state,
                       *, tl, heads, dk, dv, scale):
    l_idx = pl.program_id(1)

    @pl.when(l_idx == 0)
    def _():
        state[...] = jnp.zeros(state.shape, F32)

    cs = GLA_CHUNK
    row = lax.broadcasted_iota(I32, (cs, cs), 0)
    col = lax.broadcasted_iota(I32, (cs, cs), 1)
    causal = row >= col
    tri = jnp.where(causal, 1.0, 0.0).astype(BF16)
    for c in range(tl // cs):
        rows = slice(c * cs, (c + 1) * cs)
        for h in range(heads):
            kcols = slice(h * dk, (h + 1) * dk)
            vcols = slice(h * dv, (h + 1) * dv)
            b = _dot_exact_lhs(tri, la_ref[rows, kcols])
            b_last = b[cs - 1:cs, :]
            kh = k_ref[rows, kcols]
            q_dec = (q_ref[rows, kcols] * scale * jnp.exp(b)).astype(BF16)
            k_inv = (kh * jnp.exp(-b)).astype(BF16)
            k_end = (kh * jnp.exp(b_last - b)).astype(BF16)
            vh = v_ref[rows, vcols].astype(BF16)
            scores = jnp.where(causal, _nt(q_dec, k_inv), 0.0).astype(BF16)
            s_t = state[h]
            o = jnp.dot(scores, vh, preferred_element_type=F32) + _nt(q_dec, s_t.astype(BF16))
            state[h] = s_t * jnp.exp(b_last) + _tn(vh, k_end)
            o_ref[rows, vcols] = _rms_gate(o, gn_ref[...], g_ref[rows, vcols]).astype(o_ref.dtype)

    @pl.when(l_idx == pl.num_programs(1) - 1)
    def _():
        st_ref[0] = state[...]


def _gla_prompt(q, k, v, g, log_a, g_norm, bsz, n, heads, t_all, tl):
    dk = q.shape[1] // heads
    dv = v.shape[1] // heads
    nl = n // tl
    row = lambda b, l: (b * nl + l, 0)
    return pl.pallas_call(
        functools.partial(_gla_prompt_kernel, tl=tl, heads=heads, dk=dk, dv=dv, scale=dk ** -0.5),
        grid=(bsz, nl),
        in_specs=[pl.BlockSpec((tl, heads * dk), row), pl.BlockSpec((tl, heads * dk), row),
                  pl.BlockSpec((tl, heads * dv), row), pl.BlockSpec((tl, heads * dv), row),
                  pl.BlockSpec((tl, heads * dk), row), pl.BlockSpec((1, dv), lambda b, l: (0, 0))],
        out_specs=[pl.BlockSpec((tl, heads * dv), row),
                   pl.BlockSpec((1, heads, dv, dk), lambda b, l: (b, 0, 0, 0))],
        out_shape=[jax.ShapeDtypeStruct((t_all, heads * dv), BF16),
                   jax.ShapeDtypeStruct((bsz, heads, dv, dk), F32)],
        scratch_shapes=[pltpu.VMEM((heads, dv, dk), F32)],
        compiler_params=_params(("parallel", "arbitrary")),
        name="gla_prompt",
    )(q, k, v, g, log_a, g_norm)


def _gla_decode_kernel(s_ref, cp_ref, v_ref, g_ref, gn_ref, so_ref, o_ref, *, heads, dv, scale):
    for h in range(heads):
        cols = cp_ref[0, h]
        decay = jnp.exp(cols[:, 0:1])
        vcols = slice(h * dv, (h + 1) * dv)
        s_new = s_ref[0, h] * decay + cols[:, 1:2] * v_ref[0, :, vcols]
        so_ref[0, h] = s_new
        o = jnp.sum(_round_bf16(s_new) * _round_bf16(cols[:, 2:3] * scale), axis=0, keepdims=True)
        o_ref[0, :, vcols] = _rms_gate(o, gn_ref[...], g_ref[0, :, vcols]).astype(o_ref.dtype)


def _gla_decode(state, colpack, v, g, g_norm):
    n_seq, heads, dk, dv = state.shape
    sblk = pl.BlockSpec((1, heads, dk, dv), lambda b: (b, 0, 0, 0))
    vblk = pl.BlockSpec((1, 1, heads * dv), lambda b: (b, 0, 0))
    return pl.pallas_call(
        functools.partial(_gla_decode_kernel, heads=heads, dv=dv, scale=dk ** -0.5),
        grid=(n_seq,),
        in_specs=[sblk, pl.BlockSpec((1, heads, dk, SUBLANES), lambda b: (b, 0, 0, 0)), vblk, vblk,
                  pl.BlockSpec((1, dv), lambda b: (0, 0))],
        out_specs=[sblk, vblk],
        out_shape=[jax.ShapeDtypeStruct(state.shape, F32), jax.ShapeDtypeStruct((n_seq, 1, heads * dv), BF16)],
        compiler_params=_params(("parallel",)),
        name="gla_decode",
    )(state, colpack, v, g, g_norm)


def _gla_layer(x, j, bsz, n, tp, p):
    t_all, d = x.shape
    n_seq, heads, dk, dv = p["state_gla"].shape[1:]
    hk, hv = heads * dk, heads * dv
    rank = p["w_gate2_gla"].shape[1]
    w = _pad_cols(p["w_in_gla"][j], 2 * hk + 2 * hv + LANES).astype(BF16)
    tm = _tile(t_all, (384, 256, 128))
    q, k, v, g, r = _proj(x, w, [(0, hk), (hk, hk), (2 * hk, hv), (2 * hk + hv, hv), (2 * hk + 2 * hv, LANES)], tm)
    w_gate = jnp.pad(p["w_gate2_gla"][j], ((0, LANES - rank), (0, 0))).astype(BF16)
    log_a = _gla_gate(r, w_gate, p["b_gate_gla"][j][None, :], tm)
    g_norm = p["g_norm_gla"][j][None, :]
    o, st_p = _gla_prompt(q, k, v, g, log_a, g_norm, bsz, n, heads, t_all, _tile(n, (256, 128, 64)))
    colpack = jnp.stack([log_a[tp:], k[tp:], q[tp:]], axis=-1).reshape(n_seq, heads, dk, 3)
    colpack = jnp.pad(colpack, ((0, 0), (0, 0), (0, 0), (0, SUBLANES - 3)))
    st_s, o_s = _gla_decode(p["state_gla"][j], colpack, v[tp:].reshape(n_seq, 1, hv),
                            g[tp:].reshape(n_seq, 1, hv), g_norm)
    o = lax.dynamic_update_slice(o, o_s.reshape(n_seq, hv), (tp, 0))
    return o, dict(gla_p=st_p.transpose(0, 1, 3, 2), gla_s=st_s)


def _ssd_conv_prompt_kernel(x_ref, w_ref, b_ref, a_ref, st_ref, hist, *, tl):
    xbc = x_ref[...]
    conv = _causal_taps(hist, xbc, w_ref, pl.program_id(1) == 0, tl)
    a_ref[...] = _silu(conv + b_ref[...])
    keep = w_ref.shape[0] - 1
    st_ref[0] = xbc[tl - keep:tl, :]


def _ssd_conv_prompt(xbc, w_conv, b_conv, bsz, n, t_all, tl):
    ch = xbc.shape[1]
    width = w_conv.shape[0]
    nl = n // tl
    row = lambda b, l: (b * nl + l, 0)
    full = lambda b, l: (0, 0)
    return pl.pallas_call(
        functools.partial(_ssd_conv_prompt_kernel, tl=tl),
        grid=(bsz, nl),
        in_specs=[pl.BlockSpec((tl, ch), row), pl.BlockSpec((width, ch), full), pl.BlockSpec((1, ch), full)],
        out_specs=[pl.BlockSpec((tl, ch), row), pl.BlockSpec((1, width - 1, ch), lambda b, l: (b, 0, 0))],
        out_shape=[jax.ShapeDtypeStruct((t_all, ch), F32), jax.ShapeDtypeStruct((bsz, width - 1, ch), F32)],
        scratch_shapes=[pltpu.VMEM((HALO + tl, ch), F32)],
        compiler_params=_params(("parallel", "arbitrary")),
        name="ssd_conv_prompt",
    )(xbc, w_conv, b_conv)


def _ssd_conv_decode_kernel(x_ref, s_ref, w_ref, b_ref, a_ref):
    width = w_ref.shape[0]
    conv = w_ref[width - 1:width, :] * _round_bf16(x_ref[...])
    for k in range(width - 1):
        conv = conv + w_ref[k:k + 1, :] * _round_bf16(s_ref[k])
    a_ref[...] = _silu(conv + b_ref[...])


def _ssd_conv_decode(xbc, state_t, w_conv, b_conv):
    return pl.pallas_call(
        _ssd_conv_decode_kernel,
        out_shape=jax.ShapeDtypeStruct(xbc.shape, F32),
        compiler_params=pltpu.CompilerParams(vmem_limit_bytes=VMEM_LIMIT),
        name="ssd_conv_decode",
    )(xbc, state_t, w_conv, b_conv)


def _group_rms_gate(y, z, g_norm):
    yg = y * _silu(z)
    return yg * lax.rsqrt(jnp.mean(yg * yg, axis=-1, keepdims=True) + RMS_EPS) * g_norm


def _ssd_prompt_kernel(xs_ref, bm_ref, cm_ref, dt_ref, z_ref, dtb_ref, alog_ref, dsk_ref, gn_ref,
                       y_ref, st_ref, state, ybuf, *, q, heads, hd, ns, groups):
    l_idx = pl.program_id(1)

    @pl.when(l_idx == 0)
    def _():
        state[...] = jnp.zeros(state.shape, F32)

    pw = 2 * hd
    dt = _softplus(dt_ref[...] + dtb_ref[...])
    a = -jnp.exp(alog_ref[...])
    row = lax.broadcasted_iota(I32, (q, q), 0)
    col = lax.broadcasted_iota(I32, (q, q), 1)
    causal = row >= col
    tri = jnp.where(causal, 1.0, 0.0).astype(BF16)
    cum = _dot_exact_lhs(tri, dt * a)
    cum_t = cum.T
    e_cum = jnp.exp(cum)
    w_end = jnp.exp(cum[q - 1:q, :] - cum)
    dsk = dsk_ref[...]
    lo = lax.broadcasted_iota(I32, (q, pw), 1) < hd
    lo_rows = lax.broadcasted_iota(I32, (pw, ns), 0) < hd
    pairs_per_group = heads // groups // 2

    def pick(mat, h0):
        return jnp.where(lo, mat[:, h0:h0 + 1], mat[:, h0 + 1:h0 + 2])

    for g in range(groups):
        cmb = cm_ref[:, g * ns:(g + 1) * ns].astype(BF16)
        bmb = bm_ref[:, g * ns:(g + 1) * ns].astype(BF16)
        cb = _nt(cmb, bmb)
        for pp in range(pairs_per_group):
            pr = g * pairs_per_group + pp
            h0 = 2 * pr
            cols = slice(pr * pw, (pr + 1) * pw)
            x_pair = xs_ref[:, cols]
            xdt = x_pair * pick(dt, h0)

            def decay_mix(h):
                return (cb * jnp.exp(jnp.where(causal, cum[:, h:h + 1] - cum_t[h:h + 1, :], -jnp.inf))).astype(BF16)

            y = (jnp.dot(decay_mix(h0), jnp.where(lo, xdt, 0.0).astype(BF16), preferred_element_type=F32)
                 + jnp.dot(decay_mix(h0 + 1), jnp.where(lo, 0.0, xdt).astype(BF16), preferred_element_type=F32))
            s_pair = state[pr]
            y = y + _nt(cmb, s_pair.astype(BF16)) * pick(e_cum, h0)
            dec = jnp.where(lo_rows, jnp.exp(cum_t[h0:h0 + 1, q - 1:q]), jnp.exp(cum_t[h0 + 1:h0 + 2, q - 1:q]))
            state[pr] = s_pair * dec + _tn((xdt * pick(w_end, h0)).astype(BF16), bmb)
            ybuf[:, cols] = y + x_pair * pick(dsk, h0)

    gw = heads * hd // groups
    for g in range(groups):
        cols = slice(g * gw, (g + 1) * gw)
        y_ref[:, cols] = _group_rms_gate(ybuf[:, cols], z_ref[:, cols], gn_ref[:, cols]).astype(y_ref.dtype)

    @pl.when(l_idx == pl.num_programs(1) - 1)
    def _():
        st_ref[0] = state[...]


def _ssd_prompt(act, dt_raw, z, dt_bias, a_log, d_skip, g_norm, bsz, n, heads, hd, ns, groups, t_all, q):
    inner = heads * hd
    gn = groups * ns
    nl = n // q
    row = lambda b, l: (b * nl + l, 0)
    full = lambda b, l: (0, 0)
    n_pairs = heads // 2
    return pl.pallas_call(
        functools.partial(_ssd_prompt_kernel, q=q, heads=heads, hd=hd, ns=ns, groups=groups),
        grid=(bsz, nl),
        in_specs=[pl.BlockSpec((q, inner), row),
                  pl.BlockSpec((q, gn), lambda b, l: (b * nl + l, inner // gn)),
                  pl.BlockSpec((q, gn), lambda b, l: (b * nl + l, inner // gn + 1)),
                  pl.BlockSpec((q, LANES), row), pl.BlockSpec((q, inner), row),
                  pl.BlockSpec((1, LANES), full), pl.BlockSpec((1, LANES), full), pl.BlockSpec((1, LANES), full),
                  pl.BlockSpec((1, inner), full)],
        out_specs=[pl.BlockSpec((q, inner), row),
                   pl.BlockSpec((1, n_pairs, 2 * hd, ns), lambda b, l: (b, 0, 0, 0))],
        out_shape=[jax.ShapeDtypeStruct((t_all, inner), BF16),
                   jax.ShapeDtypeStruct((bsz, n_pairs, 2 * hd, ns), F32)],
        scratch_shapes=[pltpu.VMEM((n_pairs, 2 * hd, ns), F32), pltpu.VMEM((q, inner), F32)],
        compiler_params=_params(("parallel", "arbitrary")),
        name="ssd_prompt",
    )(act, act, act, dt_raw, z, dt_bias, a_log, d_skip, g_norm)


SSD_PACK_ROWS = 16


def _ssd_decode_kernel(s_ref, pack_ref, b_ref, c_ref, z_ref, gn_ref, so_ref, y_ref, *, n_pairs, ns, groups):
    pt = pack_ref[0].T
    pw = s_ref.shape[2]
    lane = lax.broadcasted_iota(I32, (pw, LANES), 1)
    y_cols = jnp.zeros((pw, LANES), F32)
    pairs_per_group = n_pairs // groups
    for pr in range(n_pairs):
        g = pr // pairs_per_group
        col = lambda k: pt[:, k * SSD_PACK_ROWS + pr:k * SSD_PACK_ROWS + pr + 1]
        x_col = col(0)
        dt = _softplus(col(1) + col(2))
        dec = jnp.exp(dt * -jnp.exp(col(3)))
        b_row = b_ref[0, :, g * ns:(g + 1) * ns]
        c_row = c_ref[0, :, g * ns:(g + 1) * ns]
        s_new = s_ref[0, pr] * dec + (x_col * dt) * b_row
        so_ref[0, pr] = s_new
        y_col = jnp.sum(_round_bf16(s_new) * _round_bf16(c_row), axis=1, keepdims=True) + x_col * col(4)
        y_cols = jnp.where(lane == pr, y_col, y_cols)
    y_rows = y_cols.T[0:n_pairs, :]
    yg = y_rows * _silu(z_ref[0])
    sq = jnp.sum(yg * yg, axis=1, keepdims=True)
    out = jnp.zeros(yg.shape, F32)
    sub = lax.broadcasted_iota(I32, yg.shape, 0)
    sub1 = lax.broadcasted_iota(I32, sq.shape, 0)
    for g in range(groups):
        lo_r, hi_r = g * pairs_per_group, (g + 1) * pairs_per_group
        in_g1 = jnp.logical_and(sub1 >= lo_r, sub1 < hi_r)
        ms = jnp.sum(jnp.where(in_g1, sq, 0.0), axis=0, keepdims=True) / (pairs_per_group * pw)
        out = jnp.where(jnp.logical_and(sub >= lo_r, sub < hi_r), yg * lax.rsqrt(ms + RMS_EPS), out)
    y_ref[0] = (out * gn_ref[...]).astype(y_ref.dtype)


def _ssd_decode(state, pack, b_rows, c_rows, z, g_norm, groups):
    n_seq, n_pairs, pw, ns = state.shape
    sblk = pl.BlockSpec((1, n_pairs, pw, ns), lambda b: (b, 0, 0, 0))
    rblk = pl.BlockSpec((1, 1, groups * ns), lambda b: (b, 0, 0))
    zblk = pl.BlockSpec((1, n_pairs, pw), lambda b: (b, 0, 0))
    return pl.pallas_call(
        functools.partial(_ssd_decode_kernel, n_pairs=n_pairs, ns=ns, groups=groups),
        grid=(n_seq,),
        in_specs=[sblk, pl.BlockSpec((1, LANES, LANES), lambda b: (b, 0, 0)), rblk, rblk, zblk,
                  pl.BlockSpec((n_pairs, pw), lambda b: (0, 0))],
        out_specs=[sblk, zblk],
        out_shape=[jax.ShapeDtypeStruct(state.shape, F32), jax.ShapeDtypeStruct((n_seq, n_pairs, pw), BF16)],
        compiler_params=_params(("parallel",)),
        name="ssd_decode",
    )(state, pack, b_rows, c_rows, z, g_norm)


def _ssd_layer(x, j, bsz, n, tp, p):
    t_all, d = x.shape
    n_seq, heads, hd, ns = p["state_ssm"].shape[1:]
    groups = SSD_GROUPS
    inner = heads * hd
    gn = groups * ns
    ch = inner + 2 * gn
    n_pairs = heads // 2
    pw = 2 * hd
    assert pw == LANES and ns == LANES and n_pairs == SSD_PACK_ROWS and inner % gn == 0
    w = _pad_cols(p["w_in_ssd"][j], inner + ch + LANES).astype(BF16)
    z, xbc, dt_raw = _proj(x, w, [(0, inner), (inner, ch), (inner + ch, LANES)], _tile(t_all, (256, 128)))
    w_conv = p["w_conv_ssd"][j]
    b_conv = p["b_conv_ssd"][j][None, :]
    lane_row = lambda v: _pad_cols(v[None, :], LANES)
    g_norm = p["g_norm_ssd"][j]
    act, conv_p = _ssd_conv_prompt(xbc, w_conv, b_conv, bsz, n, t_all, _tile(n, (256, 128)))
    y, ssm_p = _ssd_prompt(act, dt_raw, z, lane_row(p["dt_bias_ssd"][j]), lane_row(p["a_log_ssd"][j]),
                           lane_row(p["d_skip_ssd"][j]), g_norm[None, :], bsz, n, heads, hd, ns, groups,
                           t_all, _tile(n, (128,)))
    conv_state = p["state_conv_ssd"][j]
    xbc_s = xbc[tp:]
    act_s = _ssd_conv_decode(xbc_s, conv_state.transpose(1, 0, 2), w_conv, b_conv)
    per_row = lambda v: jnp.broadcast_to(jnp.repeat(v, hd, axis=-1).reshape(-1, n_pairs, pw), (n_seq, n_pairs, pw))
    pack = jnp.concatenate([
        act_s[:, :inner].reshape(n_seq, n_pairs, pw),
        per_row(dt_raw[tp:, :heads]), per_row(p["dt_bias_ssd"][j][None, :]),
        per_row(p["a_log_ssd"][j][None, :]), per_row(p["d_skip_ssd"][j][None, :])], axis=1)
    pack = jnp.pad(pack, ((0, 0), (0, LANES - pack.shape[1]), (0, 0)))
    ssm_s, y_s = _ssd_decode(p["state_ssm"][j].reshape(n_seq, n_pairs, pw, ns), pack,
                             act_s[:, inner:inner + gn].reshape(n_seq, 1, gn),
                             act_s[:, inner + gn:].reshape(n_seq, 1, gn),
                             z[tp:].reshape(n_seq, n_pairs, pw), g_norm.reshape(n_pairs, pw), groups)
    y = lax.dynamic_update_slice(y, y_s.reshape(n_seq, inner), (tp, 0))
    conv_s = jnp.concatenate([conv_state[:, 1:], xbc_s[:, None, :]], axis=1)
    return y, dict(ssm_p=ssm_p.reshape(bsz, heads, hd, ns), ssm_s=ssm_s.reshape(n_seq, heads, hd, ns),
                   conv_p=conv_p, conv_s=conv_s)


_LAYERS = (_fox_layer, _short_layer, _gla_layer, _ssd_layer)
_OUT_ORDER = ("k", "v", "lf", "conv_short", "gla", "ssm", "conv_ssd")


def kernel(x_prompt, x_sample, cache_k_fox, cache_v_fox, cache_logf_fox, page_table, state_conv_short, state_gla, state_ssm, state_conv_ssd, w_in_fox, b_forget_fox, w_out_fox, w_in_short, w_conv_short, w_out_short, w_in_gla, w_gate2_gla, b_gate_gla, g_norm_gla, w_out_gla, w_in_ssd, w_conv_ssd, b_conv_ssd, dt_bias_ssd, a_log_ssd, d_skip_ssd, g_norm_ssd, w_out_ssd, ln_mix_g, ln_mix_b, ln_ffn_g, ln_ffn_b, w_router, b_router, w_gate_up, b_gate_up, w_down, b_down):
    p = dict(locals())
    bsz, n, d = x_prompt.shape
    n_seq, n_dec, _ = x_sample.shape
    assert n_dec == 1, "the sample group decodes one token per sequence"
    depth = ln_mix_g.shape[0]
    alpha = (2 * depth) ** 0.25
    tp = bsz * n
    x = jnp.concatenate([x_prompt.reshape(tp, d), x_sample.reshape(n_seq, d)], axis=0)
    w_outs = (w_out_fox, w_out_short, w_out_gla, w_out_ssd)
    acc = {name + sfx: [] for name in _OUT_ORDER for sfx in ("_p", "_s")}
    for i in range(depth):
        kind, j = i % len(_LAYERS), i // len(_LAYERS)
        y_mix, outs = _LAYERS[kind](x, j, bsz, n, tp, p)
        rename = {"conv_p": ("conv_short_p" if kind == 1 else "conv_ssd_p"),
                  "conv_s": ("conv_short_s" if kind == 1 else "conv_ssd_s")}
        for key, val in outs.items():
            acc[rename.get(key, key)].append(val)
        x = _post_blocks(x, y_mix, w_outs[kind][j], i, alpha, p)
    stack = lambda name: jnp.stack(acc[name])
    return (x[:tp].reshape(bsz, n, d), x[tp:].reshape(n_seq, n_dec, d),
            *[stack(name + "_p") for name in _OUT_ORDER],
            *[stack(name + "_s") for name in _OUT_ORDER])
```

```python
import functools
import math

import jax
import jax.numpy as jnp
from jax import lax
from jax.experimental import pallas as pl
from jax.experimental.pallas import tpu as pltpu

F32 = jnp.float32
BF16 = jnp.bfloat16
I32 = jnp.int32

LANES = 128
SUBLANES = 8
VMEM_LIMIT = 56 * 1024 * 1024

LN_EPS = 1e-5
RMS_EPS = 1e-6
TOP_K = 4
SWIGLU_LIMIT = 7.0
SWIGLU_ALPHA = 1.702
GLA_TAU = 16.0
GLA_CHUNK = 64
SSD_GROUPS = 4
MOE_BLOCK = 256


def _params(semantics):
    return pltpu.CompilerParams(dimension_semantics=semantics, vmem_limit_bytes=VMEM_LIMIT)


def _tile(n, candidates):
    for c in candidates:
        if n % c == 0:
            return c
    raise ValueError(f"no tile for {n} among {candidates}")


def _round_up(n, m):
    return (n + m - 1) // m * m


def _log_sigmoid(z):
    return jnp.minimum(z, 0.0) - jnp.log(1.0 + jnp.exp(-jnp.abs(z)))


def _softplus(z):
    return jnp.maximum(z, 0.0) + jnp.log(1.0 + jnp.exp(-jnp.abs(z)))


def _silu(z):
    return z * jax.nn.sigmoid(z)


def _split3(x):
    hi = x.astype(BF16)
    r = x - hi.astype(F32)
    mid = r.astype(BF16)
    lo = (r - mid.astype(F32)).astype(BF16)
    return hi, mid, lo


def _dot_exact_lhs(a01, x):
    return sum(jnp.dot(a01, p, preferred_element_type=F32) for p in _split3(x))


def _dot_exact_rhs(x, a01):
    return sum(jnp.dot(p, a01, preferred_element_type=F32) for p in _split3(x))


def _round_bf16(x):
    return x.astype(BF16).astype(F32)


def _nt(a, b):
    return lax.dot_general(a, b, (((1,), (1,)), ((), ())), preferred_element_type=F32)


def _tn(a, b):
    return lax.dot_general(a, b, (((0,), (0,)), ((), ())), preferred_element_type=F32)


def _proj_kernel(x_ref, w_ref, *out_refs, segs, chunk):
    xb = x_ref[...].astype(BF16)
    for o_ref, (start, width) in zip(out_refs, segs):
        for c in range(0, width, chunk):
            cw = min(chunk, width - c)
            o_ref[:, c:c + cw] = jnp.dot(
                xb, w_ref[:, start + c:start + c + cw], preferred_element_type=F32).astype(o_ref.dtype)


def _proj(x, w, segs, tm):
    t, d = x.shape
    n = w.shape[1]
    assert all(s % LANES == 0 and wd % LANES == 0 for s, wd in segs)
    return pl.pallas_call(
        functools.partial(_proj_kernel, segs=tuple(segs), chunk=512),
        grid=(t // tm,),
        in_specs=[pl.BlockSpec((tm, d), lambda i: (i, 0)),
                  pl.BlockSpec((d, n), lambda i: (0, 0))],
        out_specs=[pl.BlockSpec((tm, wd), lambda i: (i, 0)) for _, wd in segs],
        out_shape=[jax.ShapeDtypeStruct((t, wd), F32) for _, wd in segs],
        compiler_params=_params(("parallel",)),
        name="proj",
    )(x, w)


def _pad_cols(w, n):
    return jnp.pad(w, ((0, 0), (0, n - w.shape[1])))


def _layer_norm_rows(z, g, b):
    mu = jnp.mean(z, axis=-1, keepdims=True)
    zc = z - mu
    var = jnp.mean(zc * zc, axis=-1, keepdims=True)
    return zc * lax.rsqrt(var + LN_EPS) * g + b


def _mix_ln_router_kernel(y_ref, w_ref, x_ref, g_ref, b_ref, wr_ref, br_ref,
                          x1_ref, topi_ref, gate_ref, rank_ref, cnt_ref, *, alpha, n_experts):
    tm = x_ref.shape[0]
    mix = jnp.dot(y_ref[...].astype(BF16), w_ref[...], preferred_element_type=F32)
    x1 = _layer_norm_rows(alpha * x_ref[...] + mix, g_ref[...], b_ref[...])
    x1_ref[...] = x1
    logits = jnp.dot(x1.astype(BF16), wr_ref[...], preferred_element_type=F32) + br_ref[...]
    lane = lax.broadcasted_iota(I32, logits.shape, 1)
    neg_inf = jnp.float32(-jnp.inf)
    cur = jnp.where(lane < n_experts, logits, neg_inf)
    topi = jnp.zeros(logits.shape, I32)
    chosen = jnp.zeros(logits.shape, F32)
    vals, ids = [], []
    for k in range(TOP_K):
        m = jnp.max(cur, axis=1, keepdims=True)
        idx = jnp.min(jnp.where(cur == m, lane, LANES), axis=1, keepdims=True)
        vals.append(m)
        ids.append(idx)
        topi = jnp.where(lane == k, idx, topi)
        chosen = jnp.where(lane == idx, 1.0, chosen)
        cur = jnp.where(lane == idx, neg_inf, cur)
    es = [jnp.exp(v - vals[0]) for v in vals]
    inv = 1.0 / sum(es)
    gate = jnp.zeros(logits.shape, F32)
    for k in range(TOP_K):
        gate = jnp.where(lane == k, es[k] * inv, gate)
    topi_ref[...] = topi
    gate_ref[...] = gate
    earlier = jnp.where(lax.broadcasted_iota(I32, (tm, tm), 0) > lax.broadcasted_iota(I32, (tm, tm), 1),
                        1.0, 0.0).astype(BF16)
    before = jnp.dot(earlier, chosen.astype(BF16), preferred_element_type=F32)
    rank = jnp.zeros(logits.shape, I32)
    for k in range(TOP_K):
        r_k = jnp.sum(jnp.where(lane == ids[k], before, 0.0), axis=1, keepdims=True)
        rank = jnp.where(lane == k, r_k.astype(I32), rank)
    rank_ref[...] = rank
    cnt_ref[0] = jnp.broadcast_to(jnp.sum(chosen, axis=0, keepdims=True), (SUBLANES, LANES)).astype(I32)


def _mix_ln_router(y, w_out, x, ln_g, ln_b, w_r, b_r, alpha, n_experts, tm):
    t, d = x.shape
    kdim = y.shape[1]
    row = lambda i: (i, 0)
    full = lambda i: (0, 0)
    return pl.pallas_call(
        functools.partial(_mix_ln_router_kernel, alpha=alpha, n_experts=n_experts),
        grid=(t // tm,),
        in_specs=[pl.BlockSpec((tm, kdim), row), pl.BlockSpec((kdim, d), full),
                  pl.BlockSpec((tm, d), row), pl.BlockSpec((1, d), full), pl.BlockSpec((1, d), full),
                  pl.BlockSpec((d, LANES), full), pl.BlockSpec((1, LANES), full)],
        out_specs=[pl.BlockSpec((tm, d), row), pl.BlockSpec((tm, LANES), row),
                   pl.BlockSpec((tm, LANES), row), pl.BlockSpec((tm, LANES), row),
                   pl.BlockSpec((1, SUBLANES, LANES), lambda i: (i, 0, 0))],
        out_shape=[jax.ShapeDtypeStruct((t, d), F32), jax.ShapeDtypeStruct((t, LANES), I32),
                   jax.ShapeDtypeStruct((t, LANES), F32), jax.ShapeDtypeStruct((t, LANES), I32),
                   jax.ShapeDtypeStruct((t // tm, SUBLANES, LANES), I32)],
        compiler_params=_params(("parallel",)),
        name="mix_ln_router",
    )(y, w_out, x, ln_g, ln_b, w_r, b_r)


def _route_tables(topi, rank, tile_cnt, n_experts, bm):
    t = topi.shape[0]
    n_pairs = t * TOP_K
    tm = t // tile_cnt.shape[0]
    e_ids = jnp.arange(n_experts, dtype=I32)
    cnt = tile_cnt[:, 0, :n_experts]
    tile_off = jnp.cumsum(cnt, axis=0) - cnt
    counts = jnp.sum(cnt, axis=0)
    starts = jnp.cumsum(counts) - counts
    padded = (counts + bm - 1) // bm * bm
    pad_ends = jnp.cumsum(padded)
    pad_starts = pad_ends - padded
    expert = topi[:, :TOP_K]
    base = jnp.repeat(pad_starts[None, :] + tile_off, tm, axis=0)
    hit = expert[:, :, None] == e_ids[None, None, :]
    pos = (jnp.sum(jnp.where(hit, base[:, None, :], 0), axis=2) + rank[:, :TOP_K]).reshape(-1).astype(I32)
    n_blocks = (n_pairs + bm - 1) // bm + n_experts + 1
    block_start = jnp.arange(n_blocks, dtype=I32) * bm
    block_expert = jnp.minimum(jnp.sum((pad_ends[None, :] <= block_start[:, None]).astype(I32), axis=1),
                               n_experts - 1).astype(I32)
    n_used = (pad_ends[-1] // bm).astype(I32).reshape(1)
    order = jnp.argsort(expert.reshape(-1)).astype(I32)
    e_row = jnp.repeat(block_expert, bm)
    off = jnp.arange(n_blocks * bm, dtype=I32) - pad_starts[e_row]
    src = jnp.clip(starts[e_row] + off, 0, n_pairs - 1)
    row_tok = jnp.where(off < counts[e_row], order[src] // TOP_K, 0).astype(I32)
    return block_expert, row_tok, n_used, pos, n_blocks


def _moe_ffn_kernel(be_ref, rt_ref, nu_ref, x_hbm, wgu_ref, bgu_ref, wd_ref, bd_ref,
                    y_ref, xbuf, sem, wgu_b, wd_b, *, bm, d_ff):
    i = pl.program_id(0)
    n_used = nu_ref[0]
    slot = i % 2

    def row_copy(blk, r, dst_slot):
        tok = rt_ref[blk * bm + r]
        return pltpu.make_async_copy(x_hbm.at[pl.ds(tok, 1), :],
                                     xbuf.at[dst_slot, pl.ds(r, 1), :], sem.at[dst_slot])

    def wait_rows(dst_slot):
        pltpu.make_async_copy(x_hbm.at[pl.ds(0, bm), :], xbuf.at[dst_slot], sem.at[dst_slot]).wait()

    @pl.when(i == 0)
    def _():
        def body(r, carry):
            row_copy(0, r, 0).start()
            return carry
        lax.fori_loop(0, bm, body, 0)

    expert_changed = jnp.logical_or(i == 0, be_ref[jnp.maximum(i - 1, 0)] != be_ref[i])

    @pl.when(jnp.logical_and(i < n_used, expert_changed))
    def _():
        wgu_b[...] = wgu_ref[0, 0].astype(BF16)
        wd_b[...] = wd_ref[0, 0].astype(BF16)

    @pl.when(i == n_used)
    def _():
        wait_rows(slot)

    @pl.when(i < n_used)
    def _():
        wait_rows(slot)
        for r in range(bm):
            row_copy(i + 1, r, 1 - slot).start()
        xb = xbuf[slot].astype(BF16)
        h = jnp.dot(xb, wgu_b[...], preferred_element_type=F32) + bgu_ref[0, 0]
        glu = jnp.minimum(h[:, :d_ff], SWIGLU_LIMIT)
        lin = jnp.clip(h[:, d_ff:], -SWIGLU_LIMIT, SWIGLU_LIMIT)
        act = glu * jax.nn.sigmoid(SWIGLU_ALPHA * glu) * (lin + 1.0)
        y_ref[...] = jnp.dot(act.astype(BF16), wd_b[...], preferred_element_type=F32) + bd_ref[0, 0]

    @pl.when(i >= n_used)
    def _():
        y_ref[...] = jnp.zeros(y_ref.shape, y_ref.dtype)


def _moe_ffn(x1, block_expert, row_tok, n_used, n_blocks, layer, w_gate_up, b_gate_up, w_down, b_down, bm):
    t, d = x1.shape
    depth, n_e, _, two_f = w_gate_up.shape
    d_ff = two_f // 2
    wmap = lambda i, be, rt, nu: (layer, be[i], 0, 0)
    grid_spec = pltpu.PrefetchScalarGridSpec(
        num_scalar_prefetch=3,
        grid=(n_blocks,),
        in_specs=[pl.BlockSpec(memory_space=pl.ANY),
                  pl.BlockSpec((1, 1, d, two_f), wmap),
                  pl.BlockSpec((1, 1, 1, two_f), wmap),
                  pl.BlockSpec((1, 1, d_ff, d), wmap),
                  pl.BlockSpec((1, 1, 1, d), wmap)],
        out_specs=pl.BlockSpec((bm, d), lambda i, be, rt, nu: (i, 0)),
        scratch_shapes=[pltpu.VMEM((2, bm, d), F32), pltpu.SemaphoreType.DMA((2,)),
                        pltpu.VMEM((d, two_f), BF16), pltpu.VMEM((d_ff, d), BF16)],
    )
    return pl.pallas_call(
        functools.partial(_moe_ffn_kernel, bm=bm, d_ff=d_ff),
        grid_spec=grid_spec,
        out_shape=jax.ShapeDtypeStruct((n_blocks * bm, d), F32),
        compiler_params=_params(("arbitrary",)),
        name="moe_ffn",
    )(block_expert, row_tok, n_used, x1, w_gate_up, b_gate_up.reshape(depth, n_e, 1, two_f),
      w_down, b_down.reshape(depth, n_e, 1, d))


def _moe_combine_kernel(pos_ref, y_hbm, gate_ref, x_ref, g_ref, b_ref, o_ref, buf, sem, *, tm, alpha):
    i = pl.program_id(0)
    nb = pl.num_programs(0)
    slot = i % 2

    def gather_rows(tile, dst_slot):
        def body(r, carry):
            for k in range(TOP_K):
                p = pos_ref[(tile * tm + r) * TOP_K + k]
                pltpu.make_async_copy(y_hbm.at[pl.ds(p, 1), :],
                                      buf.at[dst_slot, pl.ds(k * tm + r, 1), :], sem.at[dst_slot]).start()
            return carry
        lax.fori_loop(0, tm, body, 0, unroll=8)

    @pl.when(i == 0)
    def _():
        gather_rows(0, 0)

    @pl.when(i + 1 < nb)
    def _():
        gather_rows(i + 1, 1 - slot)

    pltpu.make_async_copy(y_hbm.at[pl.ds(0, TOP_K * tm), :], buf.at[slot], sem.at[slot]).wait()
    gate = gate_ref[...]
    ffn = gate[:, 0:1] * buf[slot, pl.ds(0, tm), :]
    for k in range(1, TOP_K):
        ffn = ffn + gate[:, k:k + 1] * buf[slot, pl.ds(k * tm, tm), :]
    o_ref[...] = _layer_norm_rows(alpha * x_ref[...] + ffn, g_ref[...], b_ref[...])


def _moe_combine(y_rows, pos, gates, x1, ln_g, ln_b, alpha, tm):
    t, d = x1.shape
    row = lambda i, p: (i, 0)
    full = lambda i, p: (0, 0)
    grid_spec = pltpu.PrefetchScalarGridSpec(
        num_scalar_prefetch=1,
        grid=(t // tm,),
        in_specs=[pl.BlockSpec(memory_space=pl.ANY), pl.BlockSpec((tm, LANES), row),
                  pl.BlockSpec((tm, d), row), pl.BlockSpec((1, d), full), pl.BlockSpec((1, d), full)],
        out_specs=pl.BlockSpec((tm, d), row),
        scratch_shapes=[pltpu.VMEM((2, TOP_K * tm, d), F32), pltpu.SemaphoreType.DMA((2,))],
    )
    return pl.pallas_call(
        functools.partial(_moe_combine_kernel, tm=tm, alpha=alpha),
        grid_spec=grid_spec,
        out_shape=jax.ShapeDtypeStruct((t, d), F32),
        compiler_params=_params(("arbitrary",)),
        name="moe_combine",
    )(pos, y_rows, gates, x1, ln_g, ln_b)


def _post_blocks(x, y_mix, w_out, i, alpha, p):
    n_experts = p["w_router"].shape[-1]
    d = x.shape[1]
    tm = _tile(x.shape[0], (384, 256, 128))
    w_r = _pad_cols(p["w_router"][i], LANES).astype(BF16)
    b_r = _pad_cols(p["b_router"][i][None, :].astype(F32), LANES)
    x1, topi, gates, rank, tile_cnt = _mix_ln_router(
        y_mix, w_out.astype(BF16), x, p["ln_mix_g"][i].reshape(1, d), p["ln_mix_b"][i].reshape(1, d),
        w_r, b_r, alpha, n_experts, tm)
    block_expert, row_tok, n_used, pos, n_blocks = _route_tables(topi, rank, tile_cnt, n_experts, MOE_BLOCK)
    y_rows = _moe_ffn(x1, block_expert, row_tok, n_used, n_blocks, i, p["w_gate_up"], p["b_gate_up"],
                      p["w_down"], p["b_down"], MOE_BLOCK)
    return _moe_combine(y_rows, pos, gates, x1, p["ln_ffn_g"][i].reshape(1, d),
                        p["ln_ffn_b"][i].reshape(1, d), alpha, _tile(x.shape[0], (128,)))


def _fox_gate_kernel(f_ref, bf_ref, lf_ref, c_ref, carry, *, tl):
    @pl.when(pl.program_id(1) == 0)
    def _():
        carry[...] = jnp.zeros(carry.shape, F32)

    lf = _log_sigmoid(f_ref[0] + bf_ref[...])
    lf_ref[0] = lf
    row = lax.broadcasted_iota(I32, (tl, tl), 0)
    col = lax.broadcasted_iota(I32, (tl, tl), 1)
    tri = jnp.where(row >= col, 1.0, 0.0).astype(BF16)
    c = _dot_exact_lhs(tri, lf) + carry[0:1, :]
    c_ref[0] = c
    carry[0:1, :] = c[tl - 1:tl, :]


def _fox_gate(f_raw, b_f, tl):
    bsz, n, _ = f_raw.shape
    blk = pl.BlockSpec((1, tl, LANES), lambda b, l: (b, l, 0))
    return pl.pallas_call(
        functools.partial(_fox_gate_kernel, tl=tl),
        grid=(bsz, n // tl),
        in_specs=[blk, pl.BlockSpec((1, LANES), lambda b, l: (0, 0))],
        out_specs=[blk, blk],
        out_shape=[jax.ShapeDtypeStruct(f_raw.shape, F32)] * 2,
        scratch_shapes=[pltpu.VMEM((SUBLANES, LANES), F32)],
        compiler_params=_params(("parallel", "arbitrary")),
        name="fox_gate",
    )(f_raw, b_f)


def _fox_attn_kernel(q_ref, k_ref, v_ref, cc_ref, cr_ref, o_ref, kb, vb, s_buf, m_s, l_s, acc, *, scale, tq):
    head = pl.program_id(1)
    qi = pl.program_id(2)

    @pl.when(qi == 0)
    def _():
        kb[...] = k_ref[...].astype(BF16)
        vb[...] = v_ref[...].astype(BF16)

    qs = (q_ref[...] * scale).astype(BF16)
    lane = lax.broadcasted_iota(I32, (tq, LANES), 1)
    cq = jnp.sum(jnp.where(lane == head, cc_ref[0], 0.0), axis=1, keepdims=True)
    ahead = lax.broadcasted_iota(I32, (tq, tq), 1) - lax.broadcasted_iota(I32, (tq, tq), 0)
    m_s[...] = jnp.full(m_s.shape, -jnp.inf, F32)
    l_s[...] = jnp.zeros(l_s.shape, F32)
    acc[...] = jnp.zeros(acc.shape, F32)

    def scores(kj, carry):
        rows = pl.ds(pl.multiple_of(kj * tq, tq), tq)
        s = _nt(qs, kb[rows, :]) + (cq - cr_ref[0, 0, pl.ds(kj, 1), :])
        s = jnp.where(ahead <= (qi - kj) * tq, s, -jnp.inf)
        s_buf[kj] = s
        m_prev = m_s[...]
        m_new = jnp.maximum(m_prev, jnp.max(s, axis=1, keepdims=True))
        l_s[...] = jnp.exp(m_prev - m_new) * l_s[...] + jnp.sum(jnp.exp(s - m_new), axis=1, keepdims=True)
        m_s[...] = m_new
        return carry

    lax.fori_loop(0, qi + 1, scores, 0)
    m_fin = m_s[...]
    inv_l = 1.0 / l_s[...]

    def values(kj, carry):
        rows = pl.ds(pl.multiple_of(kj * tq, tq), tq)
        p = (jnp.exp(s_buf[kj] - m_fin) * inv_l).astype(BF16)
        acc[...] += jnp.dot(p, vb[rows, :], preferred_element_type=F32)
        return carry

    lax.fori_loop(0, qi + 1, values, 0)
    o_ref[...] = acc[...].astype(o_ref.dtype)


def _fox_attn_prompt(q, k, v, c_col, c_row, bsz, n, heads, tq):
    dh = q.shape[1] // heads
    nq = n // tq
    qmap = lambda b, h, i: (b * nq + i, h)
    kmap = lambda b, h, i: (b, h)
    return pl.pallas_call(
        functools.partial(_fox_attn_kernel, scale=dh ** -0.5, tq=tq),
        grid=(bsz, heads, nq),
        in_specs=[pl.BlockSpec((tq, dh), qmap), pl.BlockSpec((n, dh), kmap), pl.BlockSpec((n, dh), kmap),
                  pl.BlockSpec((1, tq, LANES), lambda b, h, i: (b, i, 0)),
                  pl.BlockSpec((1, 1, nq, tq), lambda b, h, i: (b, h, 0, 0))],
        out_specs=pl.BlockSpec((tq, dh), qmap),
        out_shape=jax.ShapeDtypeStruct((bsz * n, heads * dh), BF16),
        scratch_shapes=[pltpu.VMEM((n, dh), BF16), pltpu.VMEM((n, dh), BF16), pltpu.VMEM((nq, tq, tq), F32),
                        pltpu.VMEM((tq, 1), F32), pltpu.VMEM((tq, 1), F32), pltpu.VMEM((tq, dh), F32)],
        compiler_params=_params(("parallel", "parallel", "arbitrary")),
        name="fox_attn_prompt",
    )(q, k, v, c_col, c_row)


DECODE_PAGES_PER_STEP = 4


def _fox_decode_kernel(pt_ref, q_ref, kn_ref, vn_ref, f_ref, bf_ref, *rest, scale, heads, dh, page, pps):
    kp_refs, vp_refs, lf_refs = rest[:pps], rest[pps:2 * pps], rest[2 * pps:3 * pps]
    o_ref, lfo_ref, carry, m_s, l_s, acc = rest[3 * pps:]
    g = pl.program_id(1)
    sub_s = lax.broadcasted_iota(I32, (heads, LANES), 0)
    lane_s = lax.broadcasted_iota(I32, (heads, LANES), 1)

    @pl.when(g == 0)
    def _():
        lf_row = _log_sigmoid(f_ref[0] + bf_ref[...])
        lfo_ref[0] = lf_row
        lf_col = jnp.sum(jnp.where(lane_s == sub_s, lf_row, 0.0), axis=1, keepdims=True)
        carry[...] = jnp.broadcast_to(lf_col, carry.shape)
        s_new = jnp.sum(q_ref[0] * scale * kn_ref[0], axis=1, keepdims=True)
        m_s[...] = jnp.broadcast_to(s_new, m_s.shape)
        l_s[...] = jnp.ones(l_s.shape, F32)
        acc[...] = vn_ref[0]

    q3 = (q_ref[0] * scale)[None]
    r_i = lax.broadcasted_iota(I32, (page, page), 0)
    c_i = lax.broadcasted_iota(I32, (page, page), 1)
    newer = jnp.where(r_i > c_i, 1.0, 0.0).astype(BF16)
    tok3 = lax.broadcasted_iota(I32, (page, heads, dh), 0)
    lane3 = lax.broadcasted_iota(I32, (page, heads, dh), 2)
    on_diag = lane3 == tok3
    for kp_ref, vp_ref, lf_ref in zip(kp_refs, vp_refs, lf_refs):
        lf = lf_ref[0, 0]
        bias = carry[...] + _dot_exact_rhs(lf, newer)
        carry[...] = carry[...] + jnp.sum(lf, axis=1, keepdims=True)
        prod = kp_ref[0, 0] * q3 + jnp.where(on_diag, bias[None], 0.0)
        s3 = jnp.broadcast_to(jnp.sum(prod, axis=2, keepdims=True), prod.shape)
        m_prev = m_s[...]
        m_new = jnp.maximum(m_prev, jnp.max(s3, axis=0))
        alpha = jnp.exp(m_prev - m_new)
        p3 = jnp.exp(s3 - m_new[None])
        l_s[...] = alpha * l_s[...] + jnp.sum(p3, axis=0)
        acc[...] = alpha * acc[...] + jnp.sum(p3 * vp_ref[0, 0], axis=0)
        m_s[...] = m_new

    @pl.when(g == pl.num_programs(1) - 1)
    def _():
        o_ref[0] = (acc[...] / l_s[...]).astype(o_ref.dtype)


def _fox_decode(q, k_new, v_new, f_raw, b_f, k_pool, v_pool, lf_pool_t, page_table, layer, heads):
    n_seq, _, dh = q.shape
    page = k_pool.shape[2]
    n_pages = page_table.shape[1]
    pps = math.gcd(DECODE_PAGES_PER_STEP, n_pages)
    assert page == LANES and dh == LANES, "one vreg per cached token; token index doubles as a lane index"
    row = lambda b, g, pt: (b, 0, 0)

    def page_map(j, ndim):
        def index(b, g, pt):
            return (layer, pt[b * n_pages + (n_pages - 1 - (g * pps + j))]) + (0,) * (ndim - 2)
        return index

    kv_specs = [pl.BlockSpec((1, 1, page, heads, dh), page_map(j, 5)) for j in range(pps)]
    lf_specs = [pl.BlockSpec((1, 1, heads, page), page_map(j, 4)) for j in range(pps)]
    hblk = pl.BlockSpec((1, heads, dh), row)
    grid_spec = pltpu.PrefetchScalarGridSpec(
        num_scalar_prefetch=1,
        grid=(n_seq, n_pages // pps),
        in_specs=[hblk, hblk, hblk, pl.BlockSpec((1, 1, LANES), row),
                  pl.BlockSpec((1, LANES), lambda b, g, pt: (0, 0))] + kv_specs + kv_specs + lf_specs,
        out_specs=[hblk, pl.BlockSpec((1, 1, LANES), row)],
        scratch_shapes=[pltpu.VMEM((heads, LANES), F32), pltpu.VMEM((heads, dh), F32),
                        pltpu.VMEM((heads, dh), F32), pltpu.VMEM((heads, dh), F32)],
    )
    return pl.pallas_call(
        functools.partial(_fox_decode_kernel, scale=dh ** -0.5, heads=heads, dh=dh, page=page, pps=pps),
        grid_spec=grid_spec,
        out_shape=[jax.ShapeDtypeStruct((n_seq, heads, dh), BF16), jax.ShapeDtypeStruct((n_seq, 1, LANES), F32)],
        compiler_params=_params(("parallel", "arbitrary")),
        name="fox_decode",
    )(page_table.reshape(-1), q, k_new, v_new, f_raw, b_f, *([k_pool] * pps), *([v_pool] * pps),
      *([lf_pool_t] * pps))


def _fox_layer(x, j, bsz, n, tp, p):
    t_all, d = x.shape
    heads = p["cache_k_fox"].shape[3]
    dh = p["cache_k_fox"].shape[4]
    hd = heads * dh
    page = p["cache_k_fox"].shape[2]
    n_layers, n_pool = p["cache_k_fox"].shape[:2]
    w = _pad_cols(p["w_in_fox"][j], 3 * hd + LANES).astype(BF16)
    q, k, v, f_raw = _proj(x, w, [(0, hd), (hd, hd), (2 * hd, hd), (3 * hd, LANES)],
                           _tile(t_all, (384, 256, 128)))
    b_f = _pad_cols(p["b_forget_fox"][j][None, :], LANES)
    lf_p, c_p = _fox_gate(f_raw[:tp].reshape(bsz, n, LANES), b_f, _tile(n, (512, 256, 128)))
    tq = _tile(n, (512, 256, 128))
    c_row = c_p[:, :, :heads].transpose(0, 2, 1).reshape(bsz, heads, n // tq, tq)
    o_p = _fox_attn_prompt(q, k, v, c_p, c_row, bsz, n, heads, tq)
    n_seq = t_all - tp
    s3 = lambda a: a[tp:].reshape(n_seq, heads, dh)
    lf_pool_t = p["cache_logf_fox"].transpose(0, 1, 3, 2)
    o_s, lf_s = _fox_decode(s3(q), s3(k), s3(v), f_raw[tp:].reshape(n_seq, 1, LANES), b_f,
                            p["cache_k_fox"], p["cache_v_fox"], lf_pool_t, p["page_table"], j, heads)
    o = jnp.concatenate([o_p, o_s.reshape(n_seq, hd)], axis=0)
    outs = dict(
        k_p=k[:tp].reshape(bsz, n, heads, dh), v_p=v[:tp].reshape(bsz, n, heads, dh),
        lf_p=lf_p[:, :, :heads],
        k_s=k[tp:].reshape(n_seq, 1, heads, dh), v_s=v[tp:].reshape(n_seq, 1, heads, dh),
        lf_s=lf_s[:, :, :heads])
    return o, outs


HALO = SUBLANES


def _causal_taps(hist, u, w_ref, first_tile, tl):
    width = w_ref.shape[0]
    u = _round_bf16(u)

    @pl.when(first_tile)
    def _():
        hist[0:HALO, :] = jnp.zeros((HALO, hist.shape[1]), F32)

    @pl.when(jnp.logical_not(first_tile))
    def _():
        hist[0:HALO, :] = hist[tl:tl + HALO, :]

    hist[HALO:HALO + tl, :] = u
    out = w_ref[width - 1:width, :] * u
    for k in range(width - 1):
        shift = width - 1 - k
        out = out + w_ref[k:k + 1, :] * hist[HALO - shift:HALO - shift + tl, :]
    return out


def _short_prompt_kernel(b_ref, c_ref, h_ref, w_ref, y_ref, st_ref, hist, *, tl):
    u = c_ref[...] * h_ref[...]
    conv = _causal_taps(hist, u, w_ref, pl.program_id(1) == 0, tl)
    y_ref[...] = (b_ref[...] * conv).astype(y_ref.dtype)
    keep = w_ref.shape[0] - 1
    st_ref[0] = u[tl - keep:tl, :]


def _short_prompt(bg, cg, hg, w_conv, bsz, n, t_all, tl):
    d = bg.shape[1]
    width = w_conv.shape[0]
    nl = n // tl
    row = lambda b, l: (b * nl + l, 0)
    return pl.pallas_call(
        functools.partial(_short_prompt_kernel, tl=tl),
        grid=(bsz, nl),
        in_specs=[pl.BlockSpec((tl, d), row)] * 3 + [pl.BlockSpec((width, d), lambda b, l: (0, 0))],
        out_specs=[pl.BlockSpec((tl, d), row), pl.BlockSpec((1, width - 1, d), lambda b, l: (b, 0, 0))],
        out_shape=[jax.ShapeDtypeStruct((t_all, d), BF16), jax.ShapeDtypeStruct((bsz, width - 1, d), F32)],
        scratch_shapes=[pltpu.VMEM((HALO + tl, d), F32)],
        compiler_params=_params(("parallel", "arbitrary")),
        name="short_prompt",
    )(bg, cg, hg, w_conv)


def _short_decode_kernel(b_ref, c_ref, h_ref, s_ref, w_ref, y_ref, u_ref):
    u = c_ref[...] * h_ref[...]
    width = w_ref.shape[0]
    conv = w_ref[width - 1:width, :] * _round_bf16(u)
    for k in range(width - 1):
        conv = conv + w_ref[k:k + 1, :] * _round_bf16(s_ref[k])
    y_ref[...] = (b_ref[...] * conv).astype(y_ref.dtype)
    u_ref[...] = u


def _short_decode(bg, cg, hg, state_t, w_conv):
    n_seq, d = bg.shape
    return pl.pallas_call(
        _short_decode_kernel,
        out_shape=[jax.ShapeDtypeStruct((n_seq, d), BF16), jax.ShapeDtypeStruct((n_seq, d), F32)],
        compiler_params=pltpu.CompilerParams(vmem_limit_bytes=VMEM_LIMIT),
        name="short_decode",
    )(bg, cg, hg, state_t, w_conv)


def _short_layer(x, j, bsz, n, tp, p):
    t_all, d = x.shape
    w = p["w_in_short"][j].astype(BF16)
    bg, cg, hg = _proj(x, w, [(0, d), (d, d), (2 * d, d)], _tile(t_all, (384, 256, 128)))
    w_conv = p["w_conv_short"][j]
    y_p, st_p = _short_prompt(bg, cg, hg, w_conv, bsz, n, tp, _tile(n, (512, 256, 128)))
    state = p["state_conv_short"][j]
    y_s, u_s = _short_decode(bg[tp:], cg[tp:], hg[tp:], state.transpose(1, 0, 2), w_conv)
    y = jnp.concatenate([y_p, y_s], axis=0)
    st_s = jnp.concatenate([state[:, 1:], u_s[:, None, :]], axis=1)
    return y, dict(conv_p=st_p, conv_s=st_s)


def _gla_gate_kernel(r_ref, w_ref, b_ref, o_ref):
    z = jnp.dot(r_ref[...].astype(BF16), w_ref[...], preferred_element_type=F32) + b_ref[...]
    o_ref[...] = _log_sigmoid(z) * (1.0 / GLA_TAU)


def _gla_gate(r, w_gate, b_gate, tm):
    t = r.shape[0]
    dk = w_gate.shape[1]
    row = lambda i: (i, 0)
    full = lambda i: (0, 0)
    return pl.pallas_call(
        _gla_gate_kernel,
        grid=(t // tm,),
        in_specs=[pl.BlockSpec((tm, LANES), row), pl.BlockSpec((LANES, dk), full), pl.BlockSpec((1, dk), full)],
        out_specs=pl.BlockSpec((tm, dk), row),
        out_shape=jax.ShapeDtypeStruct((t, dk), F32),
        compiler_params=_params(("parallel",)),
        name="gla_gate",
    )(r, w_gate, b_gate)


def _rms_gate(o, g_norm, gate):
    on = o * lax.rsqrt(jnp.mean(o * o, axis=-1, keepdims=True) + RMS_EPS) * g_norm
    return on * _silu(gate)


def _gla_prompt_kernel(q_ref, k_ref, v_ref, g_ref, la_ref, gn_ref, o_ref, st_ref, state,
                       *, tl, heads, dk, dv, scale):
    l_idx = pl.program_id(1)

    @pl.when(l_idx == 0)
    def _():
        state[...] = jnp.zeros(state.shape, F32)

    cs = GLA_CHUNK
    row = lax.broadcasted_iota(I32, (cs, cs), 0)
    col = lax.broadcasted_iota(I32, (cs, cs), 1)
    causal = row >= col
    tri = jnp.where(causal, 1.0, 0.0).astype(BF16)
    for c in range(tl // cs):
        rows = slice(c * cs, (c + 1) * cs)
        for h in range(heads):
            kcols = slice(h * dk, (h + 1) * dk)
            vcols = slice(h * dv, (h + 1) * dv)
            b = _dot_exact_lhs(tri, la_ref[rows, kcols])
            b_last = b[cs - 1:cs, :]
            kh = k_ref[rows, kcols]
            q_dec = (q_ref[rows, kcols] * scale * jnp.exp(b)).astype(BF16)
            k_inv = (kh * jnp.exp(-b)).astype(BF16)
            k_end = (kh * jnp.exp(b_last - b)).astype(BF16)
            vh = v_ref[rows, vcols].astype(BF16)
            scores = jnp.where(causal, _nt(q_dec, k_inv), 0.0).astype(BF16)
            s_t = state[h]
            o = jnp.dot(scores, vh, preferred_element_type=F32) + _nt(q_dec, s_t.astype(BF16))
            state[h] = s_t * jnp.exp(b_last) + _tn(vh, k_end)
            o_ref[rows, vcols] = _rms_gate(o, gn_ref[...], g_ref[rows, vcols]).astype(o_ref.dtype)

    @pl.when(l_idx == pl.num_programs(1) - 1)
    def _():
        st_ref[0] = state[...]


def _gla_prompt(q, k, v, g, log_a, g_norm, bsz, n, heads, t_all, tl):
    dk = q.shape[1] // heads
    dv = v.shape[1] // heads
    nl = n // tl
    row = lambda b, l: (b * nl + l, 0)
    return pl.pallas_call(
        functools.partial(_gla_prompt_kernel, tl=tl, heads=heads, dk=dk, dv=dv, scale=dk ** -0.5),
        grid=(bsz, nl),
        in_specs=[pl.BlockSpec((tl, heads * dk), row), pl.BlockSpec((tl, heads * dk), row),
                  pl.BlockSpec((tl, heads * dv), row), pl.BlockSpec((tl, heads * dv), row),
                  pl.BlockSpec((tl, heads * dk), row), pl.BlockSpec((1, dv), lambda b, l: (0, 0))],
        out_specs=[pl.BlockSpec((tl, heads * dv), row),
                   pl.BlockSpec((1, heads, dv, dk), lambda b, l: (b, 0, 0, 0))],
        out_shape=[jax.ShapeDtypeStruct((t_all, heads * dv), BF16),
                   jax.ShapeDtypeStruct((bsz, heads, dv, dk), F32)],
        scratch_shapes=[pltpu.VMEM((heads, dv, dk), F32)],
        compiler_params=_params(("parallel", "arbitrary")),
        name="gla_prompt",
    )(q, k, v, g, log_a, g_norm)


def _gla_decode_kernel(s_ref, cp_ref, v_ref, g_ref, gn_ref, so_ref, o_ref, *, heads, dv, scale):
    for h in range(heads):
        cols = cp_ref[0, h]
        decay = jnp.exp(cols[:, 0:1])
        vcols = slice(h * dv, (h + 1) * dv)
        s_new = s_ref[0, h] * decay + cols[:, 1:2] * v_ref[0, :, vcols]
        so_ref[0, h] = s_new
        o = jnp.sum(_round_bf16(s_new) * _round_bf16(cols[:, 2:3] * scale), axis=0, keepdims=True)
        o_ref[0, :, vcols] = _rms_gate(o, gn_ref[...], g_ref[0, :, vcols]).astype(o_ref.dtype)


def _gla_decode(state, colpack, v, g, g_norm):
    n_seq, heads, dk, dv = state.shape
    sblk = pl.BlockSpec((1, heads, dk, dv), lambda b: (b, 0, 0, 0))
    vblk = pl.BlockSpec((1, 1, heads * dv), lambda b: (b, 0, 0))
    return pl.pallas_call(
        functools.partial(_gla_decode_kernel, heads=heads, dv=dv, scale=dk ** -0.5),
        grid=(n_seq,),
        in_specs=[sblk, pl.BlockSpec((1, heads, dk, SUBLANES), lambda b: (b, 0, 0, 0)), vblk, vblk,
                  pl.BlockSpec((1, dv), lambda b: (0, 0))],
        out_specs=[sblk, vblk],
        out_shape=[jax.ShapeDtypeStruct(state.shape, F32), jax.ShapeDtypeStruct((n_seq, 1, heads * dv), BF16)],
        compiler_params=_params(("parallel",)),
        name="gla_decode",
    )(state, colpack, v, g, g_norm)


def _gla_layer(x, j, bsz, n, tp, p):
    t_all, d = x.shape
    n_seq, heads, dk, dv = p["state_gla"].shape[1:]
    hk, hv = heads * dk, heads * dv
    rank = p["w_gate2_gla"].shape[1]
    w = _pad_cols(p["w_in_gla"][j], 2 * hk + 2 * hv + LANES).astype(BF16)
    tm = _tile(t_all, (384, 256, 128))
    q, k, v, g, r = _proj(x, w, [(0, hk), (hk, hk), (2 * hk, hv), (2 * hk + hv, hv), (2 * hk + 2 * hv, LANES)], tm)
    w_gate = jnp.pad(p["w_gate2_gla"][j], ((0, LANES - rank), (0, 0))).astype(BF16)
    log_a = _gla_gate(r, w_gate, p["b_gate_gla"][j][None, :], tm)
    g_norm = p["g_norm_gla"][j][None, :]
    o_p, st_p = _gla_prompt(q, k, v, g, log_a, g_norm, bsz, n, heads, tp, _tile(n, (256, 128, 64)))
    colpack = jnp.stack([log_a[tp:], k[tp:], q[tp:]], axis=-1).reshape(n_seq, heads, dk, 3)
    colpack = jnp.pad(colpack, ((0, 0), (0, 0), (0, 0), (0, SUBLANES - 3)))
    st_s, o_s = _gla_decode(p["state_gla"][j], colpack, v[tp:].reshape(n_seq, 1, hv),
                            g[tp:].reshape(n_seq, 1, hv), g_norm)
    o = jnp.concatenate([o_p, o_s.reshape(n_seq, hv)], axis=0)
    return o, dict(gla_p=st_p.transpose(0, 1, 3, 2), gla_s=st_s)


def _ssd_conv_prompt_kernel(x_ref, w_ref, b_ref, a_ref, st_ref, hist, *, tl):
    xbc = x_ref[...]
    conv = _causal_taps(hist, xbc, w_ref, pl.program_id(1) == 0, tl)
    a_ref[...] = _silu(conv + b_ref[...])
    keep = w_ref.shape[0] - 1
    st_ref[0] = xbc[tl - keep:tl, :]


def _ssd_conv_prompt(xbc, w_conv, b_conv, bsz, n, t_all, tl):
    ch = xbc.shape[1]
    width = w_conv.shape[0]
    nl = n // tl
    row = lambda b, l: (b * nl + l, 0)
    full = lambda b, l: (0, 0)
    return pl.pallas_call(
        functools.partial(_ssd_conv_prompt_kernel, tl=tl),
        grid=(bsz, nl),
        in_specs=[pl.BlockSpec((tl, ch), row), pl.BlockSpec((width, ch), full), pl.BlockSpec((1, ch), full)],
        out_specs=[pl.BlockSpec((tl, ch), row), pl.BlockSpec((1, width - 1, ch), lambda b, l: (b, 0, 0))],
        out_shape=[jax.ShapeDtypeStruct((t_all, ch), F32), jax.ShapeDtypeStruct((bsz, width - 1, ch), F32)],
        scratch_shapes=[pltpu.VMEM((HALO + tl, ch), F32)],
        compiler_params=_params(("parallel", "arbitrary")),
        name="ssd_conv_prompt",
    )(xbc, w_conv, b_conv)


def _ssd_conv_decode_kernel(x_ref, s_ref, w_ref, b_ref, a_ref):
    width = w_ref.shape[0]
    conv = w_ref[width - 1:width, :] * _round_bf16(x_ref[...])
    for k in range(width - 1):
        conv = conv + w_ref[k:k + 1, :] * _round_bf16(s_ref[k])
    a_ref[...] = _silu(conv + b_ref[...])


def _ssd_conv_decode(xbc, state_t, w_conv, b_conv):
    return pl.pallas_call(
        _ssd_conv_decode_kernel,
        out_shape=jax.ShapeDtypeStruct(xbc.shape, F32),
        compiler_params=pltpu.CompilerParams(vmem_limit_bytes=VMEM_LIMIT),
        name="ssd_conv_decode",
    )(xbc, state_t, w_conv, b_conv)


def _group_rms_gate(y, z, g_norm):
    yg = y * _silu(z)
    return yg * lax.rsqrt(jnp.mean(yg * yg, axis=-1, keepdims=True) + RMS_EPS) * g_norm


def _ssd_prompt_kernel(xs_ref, bm_ref, cm_ref, dt_ref, z_ref, dtb_ref, alog_ref, dsk_ref, gn_ref,
                       y_ref, st_ref, state, ybuf, *, q, heads, hd, ns, groups):
    l_idx = pl.program_id(1)

    @pl.when(l_idx == 0)
    def _():
        state[...] = jnp.zeros(state.shape, F32)

    pw = 2 * hd
    dt = _softplus(dt_ref[...] + dtb_ref[...])
    a = -jnp.exp(alog_ref[...])
    row = lax.broadcasted_iota(I32, (q, q), 0)
    col = lax.broadcasted_iota(I32, (q, q), 1)
    causal = row >= col
    tri = jnp.where(causal, 1.0, 0.0).astype(BF16)
    cum = _dot_exact_lhs(tri, dt * a)
    cum_t = cum.T
    e_cum = jnp.exp(cum)
    w_end = jnp.exp(cum[q - 1:q, :] - cum)
    dsk = dsk_ref[...]
    lo = lax.broadcasted_iota(I32, (q, pw), 1) < hd
    lo_rows = lax.broadcasted_iota(I32, (pw, ns), 0) < hd
    pairs_per_group = heads // groups // 2

    def pick(mat, h0):
        return jnp.where(lo, mat[:, h0:h0 + 1], mat[:, h0 + 1:h0 + 2])

    for g in range(groups):
        cmb = cm_ref[:, g * ns:(g + 1) * ns].astype(BF16)
        bmb = bm_ref[:, g * ns:(g + 1) * ns].astype(BF16)
        cb = _nt(cmb, bmb)
        for pp in range(pairs_per_group):
            pr = g * pairs_per_group + pp
            h0 = 2 * pr
            cols = slice(pr * pw, (pr + 1) * pw)
            x_pair = xs_ref[:, cols]
            xdt = x_pair * pick(dt, h0)

            def decay_mix(h):
                return (cb * jnp.exp(jnp.where(causal, cum[:, h:h + 1] - cum_t[h:h + 1, :], -jnp.inf))).astype(BF16)

            y = (jnp.dot(decay_mix(h0), jnp.where(lo, xdt, 0.0).astype(BF16), preferred_element_type=F32)
                 + jnp.dot(decay_mix(h0 + 1), jnp.where(lo, 0.0, xdt).astype(BF16), preferred_element_type=F32))
            s_pair = state[pr]
            y = y + _nt(cmb, s_pair.astype(BF16)) * pick(e_cum, h0)
            dec = jnp.where(lo_rows, jnp.exp(cum_t[h0:h0 + 1, q - 1:q]), jnp.exp(cum_t[h0 + 1:h0 + 2, q - 1:q]))
            state[pr] = s_pair * dec + _tn((xdt * pick(w_end, h0)).astype(BF16), bmb)
            ybuf[:, cols] = y + x_pair * pick(dsk, h0)

    gw = heads * hd // groups
    for g in range(groups):
        cols = slice(g * gw, (g + 1) * gw)
        y_ref[:, cols] = _group_rms_gate(ybuf[:, cols], z_ref[:, cols], gn_ref[:, cols]).astype(y_ref.dtype)

    @pl.when(l_idx == pl.num_programs(1) - 1)
    def _():
        st_ref[0] = state[...]


def _ssd_prompt(act, dt_raw, z, dt_bias, a_log, d_skip, g_norm, bsz, n, heads, hd, ns, groups, t_all, q):
    inner = heads * hd
    gn = groups * ns
    nl = n // q
    row = lambda b, l: (b * nl + l, 0)
    full = lambda b, l: (0, 0)
    n_pairs = heads // 2
    return pl.pallas_call(
        functools.partial(_ssd_prompt_kernel, q=q, heads=heads, hd=hd, ns=ns, groups=groups),
        grid=(bsz, nl),
        in_specs=[pl.BlockSpec((q, inner), row),
                  pl.BlockSpec((q, gn), lambda b, l: (b * nl + l, inner // gn)),
                  pl.BlockSpec((q, gn), lambda b, l: (b * nl + l, inner // gn + 1)),
                  pl.BlockSpec((q, LANES), row), pl.BlockSpec((q, inner), row),
                  pl.BlockSpec((1, LANES), full), pl.BlockSpec((1, LANES), full), pl.BlockSpec((1, LANES), full),
                  pl.BlockSpec((1, inner), full)],
        out_specs=[pl.BlockSpec((q, inner), row),
                   pl.BlockSpec((1, n_pairs, 2 * hd, ns), lambda b, l: (b, 0, 0, 0))],
        out_shape=[jax.ShapeDtypeStruct((t_all, inner), BF16),
                   jax.ShapeDtypeStruct((bsz, n_pairs, 2 * hd, ns), F32)],
        scratch_shapes=[pltpu.VMEM((n_pairs, 2 * hd, ns), F32), pltpu.VMEM((q, inner), F32)],
        compiler_params=_params(("parallel", "arbitrary")),
        name="ssd_prompt",
    )(act, act, act, dt_raw, z, dt_bias, a_log, d_skip, g_norm)


SSD_PACK_ROWS = 16


def _ssd_decode_kernel(s_ref, pack_ref, b_ref, c_ref, z_ref, gn_ref, so_ref, y_ref, *, n_pairs, ns, groups):
    pt = pack_ref[0].T
    pw = s_ref.shape[2]
    lane = lax.broadcasted_iota(I32, (pw, LANES), 1)
    y_cols = jnp.zeros((pw, LANES), F32)
    pairs_per_group = n_pairs // groups
    for pr in range(n_pairs):
        g = pr // pairs_per_group
        col = lambda k: pt[:, k * SSD_PACK_ROWS + pr:k * SSD_PACK_ROWS + pr + 1]
        x_col = col(0)
        dt = _softplus(col(1) + col(2))
        dec = jnp.exp(dt * -jnp.exp(col(3)))
        b_row = b_ref[0, :, g * ns:(g + 1) * ns]
        c_row = c_ref[0, :, g * ns:(g + 1) * ns]
        s_new = s_ref[0, pr] * dec + (x_col * dt) * b_row
        so_ref[0, pr] = s_new
        y_col = jnp.sum(_round_bf16(s_new) * _round_bf16(c_row), axis=1, keepdims=True) + x_col * col(4)
        y_cols = jnp.where(lane == pr, y_col, y_cols)
    y_rows = y_cols.T[0:n_pairs, :]
    yg = y_rows * _silu(z_ref[0])
    sq = jnp.sum(yg * yg, axis=1, keepdims=True)
    out = jnp.zeros(yg.shape, F32)
    sub = lax.broadcasted_iota(I32, yg.shape, 0)
    sub1 = lax.broadcasted_iota(I32, sq.shape, 0)
    for g in range(groups):
        lo_r, hi_r = g * pairs_per_group, (g + 1) * pairs_per_group
        in_g1 = jnp.logical_and(sub1 >= lo_r, sub1 < hi_r)
        ms = jnp.sum(jnp.where(in_g1, sq, 0.0), axis=0, keepdims=True) / (pairs_per_group * pw)
        out = jnp.where(jnp.logical_and(sub >= lo_r, sub < hi_r), yg * lax.rsqrt(ms + RMS_EPS), out)
    y_ref[0] = (out * gn_ref[...]).astype(y_ref.dtype)


def _ssd_decode(state, pack, b_rows, c_rows, z, g_norm, groups):
    n_seq, n_pairs, pw, ns = state.shape
    sblk = pl.BlockSpec((1, n_pairs, pw, ns), lambda b: (b, 0, 0, 0))
    rblk = pl.BlockSpec((1, 1, groups * ns), lambda b: (b, 0, 0))
    zblk = pl.BlockSpec((1, n_pairs, pw), lambda b: (b, 0, 0))
    return pl.pallas_call(
        functools.partial(_ssd_decode_kernel, n_pairs=n_pairs, ns=ns, groups=groups),
        grid=(n_seq,),
        in_specs=[sblk, pl.BlockSpec((1, LANES, LANES), lambda b: (b, 0, 0)), rblk, rblk, zblk,
                  pl.BlockSpec((n_pairs, pw), lambda b: (0, 0))],
        out_specs=[sblk, zblk],
        out_shape=[jax.ShapeDtypeStruct(state.shape, F32), jax.ShapeDtypeStruct((n_seq, n_pairs, pw), BF16)],
        compiler_params=_params(("parallel",)),
        name="ssd_decode",
    )(state, pack, b_rows, c_rows, z, g_norm)


def _ssd_layer(x, j, bsz, n, tp, p):
    t_all, d = x.shape
    n_seq, heads, hd, ns = p["state_ssm"].shape[1:]
    groups = SSD_GROUPS
    inner = heads * hd
    gn = groups * ns
    ch = inner + 2 * gn
    n_pairs = heads // 2
    pw = 2 * hd
    assert pw == LANES and ns == LANES and n_pairs == SSD_PACK_ROWS and inner % gn == 0
    w = _pad_cols(p["w_in_ssd"][j], inner + ch + LANES).astype(BF16)
    z, xbc, dt_raw = _proj(x, w, [(0, inner), (inner, ch), (inner + ch, LANES)], _tile(t_all, (256, 128)))
    w_conv = p["w_conv_ssd"][j]
    b_conv = p["b_conv_ssd"][j][None, :]
    lane_row = lambda v: _pad_cols(v[None, :], LANES)
    g_norm = p["g_norm_ssd"][j]
    act, conv_p = _ssd_conv_prompt(xbc, w_conv, b_conv, bsz, n, tp, _tile(n, (256, 128)))
    y_p, ssm_p = _ssd_prompt(act, dt_raw, z, lane_row(p["dt_bias_ssd"][j]), lane_row(p["a_log_ssd"][j]),
                           lane_row(p["d_skip_ssd"][j]), g_norm[None, :], bsz, n, heads, hd, ns, groups,
                           tp, _tile(n, (128,)))
    conv_state = p["state_conv_ssd"][j]
    xbc_s = xbc[tp:]
    act_s = _ssd_conv_decode(xbc_s, conv_state.transpose(1, 0, 2), w_conv, b_conv)
    per_row = lambda v: jnp.broadcast_to(jnp.repeat(v, hd, axis=-1).reshape(-1, n_pairs, pw), (n_seq, n_pairs, pw))
    pack = jnp.concatenate([
        act_s[:, :inner].reshape(n_seq, n_pairs, pw),
        per_row(dt_raw[tp:, :heads]), per_row(p["dt_bias_ssd"][j][None, :]),
        per_row(p["a_log_ssd"][j][None, :]), per_row(p["d_skip_ssd"][j][None, :])], axis=1)
    pack = jnp.pad(pack, ((0, 0), (0, LANES - pack.shape[1]), (0, 0)))
    ssm_s, y_s = _ssd_decode(p["state_ssm"][j].reshape(n_seq, n_pairs, pw, ns), pack,
                             act_s[:, inner:inner + gn].reshape(n_seq, 1, gn),
                             act_s[:, inner + gn:].reshape(n_seq, 1, gn),
                             z[tp:].reshape(n_seq, n_pairs, pw), g_norm.reshape(n_pairs, pw), groups)
    y = jnp.concatenate([y_p, y_s.reshape(n_seq, inner)], axis=0)
    conv_s = jnp.concatenate([conv_state[:, 1:], xbc_s[:, None, :]], axis=1)
    return y, dict(ssm_p=ssm_p.reshape(bsz, heads, hd, ns), ssm_s=ssm_s.reshape(n_seq, heads, hd, ns),
                   conv_p=conv_p, conv_s=conv_s)


_LAYERS = (_fox_layer, _short_layer, _gla_layer, _ssd_layer)
_OUT_ORDER = ("k", "v", "lf", "conv_short", "gla", "ssm", "conv_ssd")


def kernel(x_prompt, x_sample, cache_k_fox, cache_v_fox, cache_logf_fox, page_table, state_conv_short, state_gla, state_ssm, state_conv_ssd, w_in_fox, b_forget_fox, w_out_fox, w_in_short, w_conv_short, w_out_short, w_in_gla, w_gate2_gla, b_gate_gla, g_norm_gla, w_out_gla, w_in_ssd, w_conv_ssd, b_conv_ssd, dt_bias_ssd, a_log_ssd, d_skip_ssd, g_norm_ssd, w_out_ssd, ln_mix_g, ln_mix_b, ln_ffn_g, ln_ffn_b, w_router, b_router, w_gate_up, b_gate_up, w_down, b_down):
    p = dict(locals())
    bsz, n, d = x_prompt.shape
    n_seq, n_dec, _ = x_sample.shape
    assert n_dec == 1, "the sample group decodes one token per sequence"
    depth = ln_mix_g.shape[0]
    alpha = (2 * depth) ** 0.25
    tp = bsz * n
    x = jnp.concatenate([x_prompt.reshape(tp, d), x_sample.reshape(n_seq, d)], axis=0)
    w_outs = (w_out_fox, w_out_short, w_out_gla, w_out_ssd)
    acc = {name + sfx: [] for name in _OUT_ORDER for sfx in ("_p", "_s")}
    for i in range(depth):
        kind, j = i % len(_LAYERS), i // len(_LAYERS)
        y_mix, outs = _LAYERS[kind](x, j, bsz, n, tp, p)
        rename = {"conv_p": ("conv_short_p" if kind == 1 else "conv_ssd_p"),
                  "conv_s": ("conv_short_s" if kind == 1 else "conv_ssd_s")}
        for key, val in outs.items():
            acc[rename.get(key, key)].append(val)
        x = _post_blocks(x, y_mix, w_outs[kind][j], i, alpha, p)
    stack = lambda name: jnp.stack(acc[name])
    return (x[:tp].reshape(bsz, n, d), x[tp:].reshape(n_seq, n_dec, d),
            *[stack(name + "_p") for name in _OUT_ORDER],
            *[stack(name + "_s") for name in _OUT_ORDER])
```

```python
import functools
import math

import jax
import jax.numpy as jnp
from jax import lax
from jax.experimental import pallas as pl
from jax.experimental.pallas import tpu as pltpu

F32 = jnp.float32
BF16 = jnp.bfloat16
I32 = jnp.int32

LANES = 128
SUBLANES = 8
VMEM_LIMIT = 56 * 1024 * 1024

LN_EPS = 1e-5
RMS_EPS = 1e-6
TOP_K = 4
SWIGLU_LIMIT = 7.0
SWIGLU_ALPHA = 1.702
GLA_TAU = 16.0
GLA_CHUNK = 64
SSD_GROUPS = 4
MOE_BLOCK = 256


def _params(semantics):
    return pltpu.CompilerParams(dimension_semantics=semantics, vmem_limit_bytes=VMEM_LIMIT)


def _tile(n, candidates):
    for c in candidates:
        if n % c == 0:
            return c
    raise ValueError(f"no tile for {n} among {candidates}")


def _round_up(n, m):
    return (n + m - 1) // m * m


def _log_sigmoid(z):
    return jnp.minimum(z, 0.0) - jnp.log(1.0 + jnp.exp(-jnp.abs(z)))


def _softplus(z):
    return jnp.maximum(z, 0.0) + jnp.log(1.0 + jnp.exp(-jnp.abs(z)))


def _silu(z):
    return z * jax.nn.sigmoid(z)


def _split3(x):
    hi = x.astype(BF16)
    r = x - hi.astype(F32)
    mid = r.astype(BF16)
    lo = (r - mid.astype(F32)).astype(BF16)
    return hi, mid, lo


def _dot_exact_lhs(a01, x):
    return sum(jnp.dot(a01, p, preferred_element_type=F32) for p in _split3(x))


def _dot_exact_rhs(x, a01):
    return sum(jnp.dot(p, a01, preferred_element_type=F32) for p in _split3(x))


def _round_bf16(x):
    return x.astype(BF16).astype(F32)


def _nt(a, b):
    return lax.dot_general(a, b, (((1,), (1,)), ((), ())), preferred_element_type=F32)


def _tn(a, b):
    return lax.dot_general(a, b, (((0,), (0,)), ((), ())), preferred_element_type=F32)


def _proj_kernel(x_ref, w_ref, *out_refs, segs, chunk):
    xb = x_ref[...].astype(BF16)
    for o_ref, (start, width) in zip(out_refs, segs):
        for c in range(0, width, chunk):
            cw = min(chunk, width - c)
            o_ref[:, c:c + cw] = jnp.dot(
                xb, w_ref[:, start + c:start + c + cw], preferred_element_type=F32).astype(o_ref.dtype)


def _proj(x, w, segs, tm):
    t, d = x.shape
    n = w.shape[1]
    assert all(s % LANES == 0 and wd % LANES == 0 for s, wd in segs)
    return pl.pallas_call(
        functools.partial(_proj_kernel, segs=tuple(segs), chunk=512),
        grid=(t // tm,),
        in_specs=[pl.BlockSpec((tm, d), lambda i: (i, 0)),
                  pl.BlockSpec((d, n), lambda i: (0, 0))],
        out_specs=[pl.BlockSpec((tm, wd), lambda i: (i, 0)) for _, wd in segs],
        out_shape=[jax.ShapeDtypeStruct((t, wd), F32) for _, wd in segs],
        compiler_params=_params(("parallel",)),
        name="proj",
    )(x, w)


def _pad_cols(w, n):
    return jnp.pad(w, ((0, 0), (0, n - w.shape[1])))


def _layer_norm_rows(z, g, b):
    mu = jnp.mean(z, axis=-1, keepdims=True)
    zc = z - mu
    var = jnp.mean(zc * zc, axis=-1, keepdims=True)
    return zc * lax.rsqrt(var + LN_EPS) * g + b


def _mix_ln_router_kernel(y_ref, w_ref, x_ref, g_ref, b_ref, wr_ref, br_ref,
                          x1_ref, topi_ref, gate_ref, rank_ref, cnt_ref, *, alpha, n_experts):
    tm = x_ref.shape[0]
    mix = jnp.dot(y_ref[...].astype(BF16), w_ref[...], preferred_element_type=F32)
    x1 = _layer_norm_rows(alpha * x_ref[...] + mix, g_ref[...], b_ref[...])
    x1_ref[...] = x1
    logits = jnp.dot(x1.astype(BF16), wr_ref[...], preferred_element_type=F32) + br_ref[...]
    lane = lax.broadcasted_iota(I32, logits.shape, 1)
    neg_inf = jnp.float32(-jnp.inf)
    cur = jnp.where(lane < n_experts, logits, neg_inf)
    topi = jnp.zeros(logits.shape, I32)
    chosen = jnp.zeros(logits.shape, F32)
    vals, ids = [], []
    for k in range(TOP_K):
        m = jnp.max(cur, axis=1, keepdims=True)
        idx = jnp.min(jnp.where(cur == m, lane, LANES), axis=1, keepdims=True)
        vals.append(m)
        ids.append(idx)
        topi = jnp.where(lane == k, idx, topi)
        chosen = jnp.where(lane == idx, 1.0, chosen)
        cur = jnp.where(lane == idx, neg_inf, cur)
    es = [jnp.exp(v - vals[0]) for v in vals]
    inv = 1.0 / sum(es)
    gate = jnp.zeros(logits.shape, F32)
    for k in range(TOP_K):
        gate = jnp.where(lane == k, es[k] * inv, gate)
    topi_ref[...] = topi
    gate_ref[...] = gate
    earlier = jnp.where(lax.broadcasted_iota(I32, (tm, tm), 0) > lax.broadcasted_iota(I32, (tm, tm), 1),
                        1.0, 0.0).astype(BF16)
    before = jnp.dot(earlier, chosen.astype(BF16), preferred_element_type=F32)
    rank = jnp.zeros(logits.shape, I32)
    for k in range(TOP_K):
        r_k = jnp.sum(jnp.where(lane == ids[k], before, 0.0), axis=1, keepdims=True)
        rank = jnp.where(lane == k, r_k.astype(I32), rank)
    rank_ref[...] = rank
    cnt_ref[0] = jnp.broadcast_to(jnp.sum(chosen, axis=0, keepdims=True), (SUBLANES, LANES)).astype(I32)


def _mix_ln_router(y, w_out, x, ln_g, ln_b, w_r, b_r, alpha, n_experts, tm):
    t, d = x.shape
    kdim = y.shape[1]
    row = lambda i: (i, 0)
    full = lambda i: (0, 0)
    return pl.pallas_call(
        functools.partial(_mix_ln_router_kernel, alpha=alpha, n_experts=n_experts),
        grid=(t // tm,),
        in_specs=[pl.BlockSpec((tm, kdim), row), pl.BlockSpec((kdim, d), full),
                  pl.BlockSpec((tm, d), row), pl.BlockSpec((1, d), full), pl.BlockSpec((1, d), full),
                  pl.BlockSpec((d, LANES), full), pl.BlockSpec((1, LANES), full)],
        out_specs=[pl.BlockSpec((tm, d), row), pl.BlockSpec((tm, LANES), row),
                   pl.BlockSpec((tm, LANES), row), pl.BlockSpec((tm, LANES), row),
                   pl.BlockSpec((1, SUBLANES, LANES), lambda i: (i, 0, 0))],
        out_shape=[jax.ShapeDtypeStruct((t, d), F32), jax.ShapeDtypeStruct((t, LANES), I32),
                   jax.ShapeDtypeStruct((t, LANES), F32), jax.ShapeDtypeStruct((t, LANES), I32),
                   jax.ShapeDtypeStruct((t // tm, SUBLANES, LANES), I32)],
        compiler_params=_params(("parallel",)),
        name="mix_ln_router",
    )(y, w_out, x, ln_g, ln_b, w_r, b_r)


def _route_tables(topi, rank, tile_cnt, n_experts, bm):
    t = topi.shape[0]
    n_pairs = t * TOP_K
    tm = t // tile_cnt.shape[0]
    e_ids = jnp.arange(n_experts, dtype=I32)
    cnt = tile_cnt[:, 0, :n_experts]
    tile_off = jnp.cumsum(cnt, axis=0) - cnt
    counts = jnp.sum(cnt, axis=0)
    starts = jnp.cumsum(counts) - counts
    padded = (counts + bm - 1) // bm * bm
    pad_ends = jnp.cumsum(padded)
    pad_starts = pad_ends - padded
    expert = topi[:, :TOP_K]
    base = jnp.repeat(pad_starts[None, :] + tile_off, tm, axis=0)
    hit = expert[:, :, None] == e_ids[None, None, :]
    pos = (jnp.sum(jnp.where(hit, base[:, None, :], 0), axis=2) + rank[:, :TOP_K]).reshape(-1).astype(I32)
    n_blocks = (n_pairs + bm - 1) // bm + n_experts + 1
    block_start = jnp.arange(n_blocks, dtype=I32) * bm
    block_expert = jnp.minimum(jnp.sum((pad_ends[None, :] <= block_start[:, None]).astype(I32), axis=1),
                               n_experts - 1).astype(I32)
    n_used = (pad_ends[-1] // bm).astype(I32).reshape(1)
    pair_bits = max(1, (n_pairs - 1).bit_length())
    assert (n_experts << pair_bits) < 2 ** 31
    keyed = jnp.sort(expert.reshape(-1) * (1 << pair_bits) + jnp.arange(n_pairs, dtype=I32))
    order = keyed & ((1 << pair_bits) - 1)
    e_row = jnp.repeat(block_expert, bm)
    off = jnp.arange(n_blocks * bm, dtype=I32) - pad_starts[e_row]
    src = jnp.clip(starts[e_row] + off, 0, n_pairs - 1)
    row_tok = jnp.where(off < counts[e_row], order[src] // TOP_K, 0).astype(I32)
    return block_expert, row_tok, n_used, pos, n_blocks


def _moe_ffn_kernel(be_ref, rt_ref, nu_ref, x_hbm, wgu_ref, bgu_ref, wd_ref, bd_ref,
                    y_ref, xbuf, sem, wgu_b, wd_b, xb, *, bm, d_ff):
    i = pl.program_id(0)
    n_used = nu_ref[0]
    slot = i % 2

    def row_copy(blk, r, dst_slot):
        tok = rt_ref[blk * bm + r]
        return pltpu.make_async_copy(x_hbm.at[pl.ds(tok, 1), :],
                                     xbuf.at[dst_slot, pl.ds(r, 1), :], sem.at[dst_slot])

    def wait_rows(dst_slot):
        pltpu.make_async_copy(x_hbm.at[pl.ds(0, bm), :], xbuf.at[dst_slot], sem.at[dst_slot]).wait()

    @pl.when(i == 0)
    def _():
        def body(r, carry):
            row_copy(0, r, 0).start()
            return carry
        lax.fori_loop(0, bm, body, 0)

    expert_changed = jnp.logical_or(i == 0, be_ref[jnp.maximum(i - 1, 0)] != be_ref[i])

    @pl.when(jnp.logical_and(i < n_used, expert_changed))
    def _():
        wgu_b[...] = wgu_ref[0, 0].astype(BF16)
        wd_b[...] = wd_ref[0, 0].astype(BF16)

    @pl.when(i == n_used)
    def _():
        wait_rows(slot)

    @pl.when(i < n_used)
    def _():
        wait_rows(slot)
        xb[...] = xbuf[slot].astype(BF16)
        for r in range(bm):
            row_copy(i + 1, r, 1 - slot).start()
        h = jnp.dot(xb[...], wgu_b[...], preferred_element_type=F32) + bgu_ref[0, 0]
        glu = jnp.minimum(h[:, :d_ff], SWIGLU_LIMIT)
        lin = jnp.clip(h[:, d_ff:], -SWIGLU_LIMIT, SWIGLU_LIMIT)
        act = glu * jax.nn.sigmoid(SWIGLU_ALPHA * glu) * (lin + 1.0)
        y_ref[...] = jnp.dot(act.astype(BF16), wd_b[...], preferred_element_type=F32) + bd_ref[0, 0]

    @pl.when(i >= n_used)
    def _():
        y_ref[...] = jnp.zeros(y_ref.shape, y_ref.dtype)


def _moe_ffn(x1, block_expert, row_tok, n_used, n_blocks, layer, w_gate_up, b_gate_up, w_down, b_down, bm):
    t, d = x1.shape
    depth, n_e, _, two_f = w_gate_up.shape
    d_ff = two_f // 2
    wmap = lambda i, be, rt, nu: (layer, be[i], 0, 0)
    grid_spec = pltpu.PrefetchScalarGridSpec(
        num_scalar_prefetch=3,
        grid=(n_blocks,),
        in_specs=[pl.BlockSpec(memory_space=pl.ANY),
                  pl.BlockSpec((1, 1, d, two_f), wmap),
                  pl.BlockSpec((1, 1, 1, two_f), wmap),
                  pl.BlockSpec((1, 1, d_ff, d), wmap),
                  pl.BlockSpec((1, 1, 1, d), wmap)],
        out_specs=pl.BlockSpec((bm, d), lambda i, be, rt, nu: (i, 0)),
        scratch_shapes=[pltpu.VMEM((2, bm, d), F32), pltpu.SemaphoreType.DMA((2,)),
                        pltpu.VMEM((d, two_f), BF16), pltpu.VMEM((d_ff, d), BF16), pltpu.VMEM((bm, d), BF16)],
    )
    return pl.pallas_call(
        functools.partial(_moe_ffn_kernel, bm=bm, d_ff=d_ff),
        grid_spec=grid_spec,
        out_shape=jax.ShapeDtypeStruct((n_blocks * bm, d), F32),
        compiler_params=_params(("arbitrary",)),
        name="moe_ffn",
    )(block_expert, row_tok, n_used, x1, w_gate_up, b_gate_up.reshape(depth, n_e, 1, two_f),
      w_down, b_down.reshape(depth, n_e, 1, d))


def _moe_combine_kernel(pos_ref, y_hbm, gate_ref, x_ref, g_ref, b_ref, o_ref, buf, sem, *, tm, alpha):
    i = pl.program_id(0)
    nb = pl.num_programs(0)
    slot = i % 2

    def gather_rows(tile, dst_slot):
        def body(r, carry):
            for k in range(TOP_K):
                p = pos_ref[(tile * tm + r) * TOP_K + k]
                pltpu.make_async_copy(y_hbm.at[pl.ds(p, 1), :],
                                      buf.at[dst_slot, pl.ds(k * tm + r, 1), :], sem.at[dst_slot]).start()
            return carry
        lax.fori_loop(0, tm, body, 0, unroll=8)

    @pl.when(i == 0)
    def _():
        gather_rows(0, 0)

    @pl.when(i + 1 < nb)
    def _():
        gather_rows(i + 1, 1 - slot)

    pltpu.make_async_copy(y_hbm.at[pl.ds(0, TOP_K * tm), :], buf.at[slot], sem.at[slot]).wait()
    gate = gate_ref[...]
    ffn = gate[:, 0:1] * buf[slot, pl.ds(0, tm), :]
    for k in range(1, TOP_K):
        ffn = ffn + gate[:, k:k + 1] * buf[slot, pl.ds(k * tm, tm), :]
    o_ref[...] = _layer_norm_rows(alpha * x_ref[...] + ffn, g_ref[...], b_ref[...])


def _moe_combine(y_rows, pos, gates, x1, ln_g, ln_b, alpha, tm):
    t, d = x1.shape
    row = lambda i, p: (i, 0)
    full = lambda i, p: (0, 0)
    grid_spec = pltpu.PrefetchScalarGridSpec(
        num_scalar_prefetch=1,
        grid=(t // tm,),
        in_specs=[pl.BlockSpec(memory_space=pl.ANY), pl.BlockSpec((tm, LANES), row),
                  pl.BlockSpec((tm, d), row), pl.BlockSpec((1, d), full), pl.BlockSpec((1, d), full)],
        out_specs=pl.BlockSpec((tm, d), row),
        scratch_shapes=[pltpu.VMEM((2, TOP_K * tm, d), F32), pltpu.SemaphoreType.DMA((2,))],
    )
    return pl.pallas_call(
        functools.partial(_moe_combine_kernel, tm=tm, alpha=alpha),
        grid_spec=grid_spec,
        out_shape=jax.ShapeDtypeStruct((t, d), F32),
        compiler_params=_params(("arbitrary",)),
        name="moe_combine",
    )(pos, y_rows, gates, x1, ln_g, ln_b)


def _post_blocks(x, y_mix, w_out, i, alpha, p):
    n_experts = p["w_router"].shape[-1]
    d = x.shape[1]
    tm = _tile(x.shape[0], (384, 256, 128))
    w_r = _pad_cols(p["w_router"][i], LANES).astype(BF16)
    b_r = _pad_cols(p["b_router"][i][None, :].astype(F32), LANES)
    x1, topi, gates, rank, tile_cnt = _mix_ln_router(
        y_mix, w_out.astype(BF16), x, p["ln_mix_g"][i].reshape(1, d), p["ln_mix_b"][i].reshape(1, d),
        w_r, b_r, alpha, n_experts, tm)
    block_expert, row_tok, n_used, pos, n_blocks = _route_tables(topi, rank, tile_cnt, n_experts, MOE_BLOCK)
    y_rows = _moe_ffn(x1, block_expert, row_tok, n_used, n_blocks, i, p["w_gate_up"], p["b_gate_up"],
                      p["w_down"], p["b_down"], MOE_BLOCK)
    return _moe_combine(y_rows, pos, gates, x1, p["ln_ffn_g"][i].reshape(1, d),
                        p["ln_ffn_b"][i].reshape(1, d), alpha, _tile(x.shape[0], (128,)))


def _fox_gate_kernel(f_ref, bf_ref, lf_ref, c_ref, carry, *, tl):
    @pl.when(pl.program_id(1) == 0)
    def _():
        carry[...] = jnp.zeros(carry.shape, F32)

    lf = _log_sigmoid(f_ref[0] + bf_ref[...])
    lf_ref[0] = lf
    row = lax.broadcasted_iota(I32, (tl, tl), 0)
    col = lax.broadcasted_iota(I32, (tl, tl), 1)
    tri = jnp.where(row >= col, 1.0, 0.0).astype(BF16)
    c = _dot_exact_lhs(tri, lf) + carry[0:1, :]
    c_ref[0] = c
    carry[0:1, :] = c[tl - 1:tl, :]


def _fox_gate(f_raw, b_f, tl):
    bsz, n, _ = f_raw.shape
    blk = pl.BlockSpec((1, tl, LANES), lambda b, l: (b, l, 0))
    return pl.pallas_call(
        functools.partial(_fox_gate_kernel, tl=tl),
        grid=(bsz, n // tl),
        in_specs=[blk, pl.BlockSpec((1, LANES), lambda b, l: (0, 0))],
        out_specs=[blk, blk],
        out_shape=[jax.ShapeDtypeStruct(f_raw.shape, F32)] * 2,
        scratch_shapes=[pltpu.VMEM((SUBLANES, LANES), F32)],
        compiler_params=_params(("parallel", "arbitrary")),
        name="fox_gate",
    )(f_raw, b_f)


def _fox_attn_kernel(q_ref, k_ref, v_ref, cc_ref, cr_ref, o_ref, kb, vb, s_buf, m_s, l_s, acc, *, scale, tq):
    head = pl.program_id(1)
    qi = pl.program_id(2)

    @pl.when(qi == 0)
    def _():
        kb[...] = k_ref[...].astype(BF16)
        vb[...] = v_ref[...].astype(BF16)

    qs = (q_ref[...] * scale).astype(BF16)
    lane = lax.broadcasted_iota(I32, (tq, LANES), 1)
    cq = jnp.sum(jnp.where(lane == head, cc_ref[0], 0.0), axis=1, keepdims=True)
    n_lt = tq // LANES
    lane_tiles = lambda a: [a[:, c * LANES:(c + 1) * LANES] for c in range(n_lt)]

    def score_tile(kj, masked):
        rows = pl.ds(pl.multiple_of(kj * tq, tq), tq)
        s = _nt(qs, kb[rows, :]) + (cq - cr_ref[0, 0, pl.ds(kj, 1), :])
        if masked:
            s = jnp.where(lax.broadcasted_iota(I32, (tq, tq), 1) <= lax.broadcasted_iota(I32, (tq, tq), 0),
                          s, -jnp.inf)
        s_buf[kj] = s
        return functools.reduce(jnp.maximum, lane_tiles(s))

    m_s[...] = score_tile(qi, True)

    def scores(kj, carry):
        m_s[...] = jnp.maximum(m_s[...], score_tile(kj, False))
        return carry

    lax.fori_loop(0, qi, scores, 0)
    m_s[...] = jnp.broadcast_to(jnp.max(m_s[...], axis=1, keepdims=True), m_s.shape)

    l_s[...] = jnp.zeros(l_s.shape, F32)

    def exps(kj, carry):
        m_rep = m_s[...]
        e_tiles = [jnp.exp(t - m_rep) for t in lane_tiles(s_buf[kj])]
        s_buf[kj] = jnp.concatenate(e_tiles, axis=1)
        l_s[...] += functools.reduce(jnp.add, e_tiles)
        return carry

    lax.fori_loop(0, qi + 1, exps, 0)
    l_s[...] = jnp.broadcast_to(1.0 / jnp.sum(l_s[...], axis=1, keepdims=True), l_s.shape)

    acc[...] = jnp.zeros(acc.shape, F32)

    def values(kj, carry):
        rows = pl.ds(pl.multiple_of(kj * tq, tq), tq)
        inv_rep = l_s[...]
        p = jnp.concatenate([(t * inv_rep).astype(BF16) for t in lane_tiles(s_buf[kj])], axis=1)
        acc[...] += jnp.dot(p, vb[rows, :], preferred_element_type=F32)
        return carry

    lax.fori_loop(0, qi + 1, values, 0)
    o_ref[...] = acc[...].astype(o_ref.dtype)


def _fox_attn_prompt(q, k, v, c_col, c_row, bsz, n, heads, tq):
    dh = q.shape[1] // heads
    nq = n // tq
    qmap = lambda b, h, i: (b * nq + i, h)
    kmap = lambda b, h, i: (b, h)
    return pl.pallas_call(
        functools.partial(_fox_attn_kernel, scale=dh ** -0.5, tq=tq),
        grid=(bsz, heads, nq),
        in_specs=[pl.BlockSpec((tq, dh), qmap), pl.BlockSpec((n, dh), kmap), pl.BlockSpec((n, dh), kmap),
                  pl.BlockSpec((1, tq, LANES), lambda b, h, i: (b, i, 0)),
                  pl.BlockSpec((1, 1, nq, tq), lambda b, h, i: (b, h, 0, 0))],
        out_specs=pl.BlockSpec((tq, dh), qmap),
        out_shape=jax.ShapeDtypeStruct((bsz * n, heads * dh), BF16),
        scratch_shapes=[pltpu.VMEM((n, dh), BF16), pltpu.VMEM((n, dh), BF16), pltpu.VMEM((nq, tq, tq), F32),
                        pltpu.VMEM((tq, LANES), F32), pltpu.VMEM((tq, LANES), F32), pltpu.VMEM((tq, dh), F32)],
        compiler_params=_params(("parallel", "parallel", "arbitrary")),
        name="fox_attn_prompt",
    )(q, k, v, c_col, c_row)


DECODE_PAGES_PER_STEP = 4


def _fox_decode_kernel(pt_ref, q_ref, kn_ref, vn_ref, f_ref, bf_ref, *rest, scale, heads, dh, page, pps):
    kp_refs, vp_refs, lf_refs = rest[:pps], rest[pps:2 * pps], rest[2 * pps:3 * pps]
    o_ref, lfo_ref, carry, m_s, l_s, acc = rest[3 * pps:]
    g = pl.program_id(1)
    sub_s = lax.broadcasted_iota(I32, (heads, LANES), 0)
    lane_s = lax.broadcasted_iota(I32, (heads, LANES), 1)

    @pl.when(g == 0)
    def _():
        lf_row = _log_sigmoid(f_ref[0] + bf_ref[...])
        lfo_ref[0] = lf_row
        lf_col = jnp.sum(jnp.where(lane_s == sub_s, lf_row, 0.0), axis=1, keepdims=True)
        carry[...] = jnp.broadcast_to(lf_col, carry.shape)
        s_new = jnp.sum(q_ref[0] * scale * kn_ref[0], axis=1, keepdims=True)
        m_s[...] = jnp.broadcast_to(s_new, m_s.shape)
        l_s[...] = jnp.ones(l_s.shape, F32)
        acc[...] = vn_ref[0]

    q3 = (q_ref[0] * scale)[None]
    r_i = lax.broadcasted_iota(I32, (page, page), 0)
    c_i = lax.broadcasted_iota(I32, (page, page), 1)
    newer = jnp.where(r_i > c_i, 1.0, 0.0).astype(BF16)
    tok3 = lax.broadcasted_iota(I32, (page, heads, dh), 0)
    lane3 = lax.broadcasted_iota(I32, (page, heads, dh), 2)
    on_diag = lane3 == tok3
    for kp_ref, vp_ref, lf_ref in zip(kp_refs, vp_refs, lf_refs):
        lf = lf_ref[0, 0]
        bias = carry[...] + _dot_exact_rhs(lf, newer)
        carry[...] = carry[...] + jnp.sum(lf, axis=1, keepdims=True)
        prod = kp_ref[0, 0] * q3 + jnp.where(on_diag, bias[None], 0.0)
        s3 = jnp.broadcast_to(jnp.sum(prod, axis=2, keepdims=True), prod.shape)
        m_prev = m_s[...]
        m_new = jnp.maximum(m_prev, jnp.max(s3, axis=0))
        alpha = jnp.exp(m_prev - m_new)
        p3 = jnp.exp(s3 - m_new[None])
        l_s[...] = alpha * l_s[...] + jnp.sum(p3, axis=0)
        acc[...] = alpha * acc[...] + jnp.sum(p3 * vp_ref[0, 0], axis=0)
        m_s[...] = m_new

    @pl.when(g == pl.num_programs(1) - 1)
    def _():
        o_ref[0] = (acc[...] / l_s[...]).astype(o_ref.dtype)


def _fox_decode(q, k_new, v_new, f_raw, b_f, k_pool, v_pool, lf_pool_t, page_table, layer, heads):
    n_seq, _, dh = q.shape
    page = k_pool.shape[2]
    n_pages = page_table.shape[1]
    pps = math.gcd(DECODE_PAGES_PER_STEP, n_pages)
    assert page == LANES and dh == LANES, "one vreg per cached token; token index doubles as a lane index"
    row = lambda b, g, pt: (b, 0, 0)

    def page_map(j, ndim):
        def index(b, g, pt):
            return (layer, pt[b * n_pages + (n_pages - 1 - (g * pps + j))]) + (0,) * (ndim - 2)
        return index

    kv_specs = [pl.BlockSpec((1, 1, page, heads, dh), page_map(j, 5)) for j in range(pps)]
    lf_specs = [pl.BlockSpec((1, 1, heads, page), page_map(j, 4)) for j in range(pps)]
    hblk = pl.BlockSpec((1, heads, dh), row)
    grid_spec = pltpu.PrefetchScalarGridSpec(
        num_scalar_prefetch=1,
        grid=(n_seq, n_pages // pps),
        in_specs=[hblk, hblk, hblk, pl.BlockSpec((1, 1, LANES), row),
                  pl.BlockSpec((1, LANES), lambda b, g, pt: (0, 0))] + kv_specs + kv_specs + lf_specs,
        out_specs=[hblk, pl.BlockSpec((1, 1, LANES), row)],
        scratch_shapes=[pltpu.VMEM((heads, LANES), F32), pltpu.VMEM((heads, dh), F32),
                        pltpu.VMEM((heads, dh), F32), pltpu.VMEM((heads, dh), F32)],
    )
    return pl.pallas_call(
        functools.partial(_fox_decode_kernel, scale=dh ** -0.5, heads=heads, dh=dh, page=page, pps=pps),
        grid_spec=grid_spec,
        out_shape=[jax.ShapeDtypeStruct((n_seq, heads, dh), BF16), jax.ShapeDtypeStruct((n_seq, 1, LANES), F32)],
        compiler_params=_params(("parallel", "arbitrary")),
        name="fox_decode",
    )(page_table.reshape(-1), q, k_new, v_new, f_raw, b_f, *([k_pool] * pps), *([v_pool] * pps),
      *([lf_pool_t] * pps))


def _fox_layer(x, j, bsz, n, tp, p):
    t_all, d = x.shape
    heads = p["cache_k_fox"].shape[3]
    dh = p["cache_k_fox"].shape[4]
    hd = heads * dh
    page = p["cache_k_fox"].shape[2]
    n_layers, n_pool = p["cache_k_fox"].shape[:2]
    w = _pad_cols(p["w_in_fox"][j], 3 * hd + LANES).astype(BF16)
    q, k, v, f_raw = _proj(x, w, [(0, hd), (hd, hd), (2 * hd, hd), (3 * hd, LANES)],
                           _tile(t_all, (384, 256, 128)))
    b_f = _pad_cols(p["b_forget_fox"][j][None, :], LANES)
    lf_p, c_p = _fox_gate(f_raw[:tp].reshape(bsz, n, LANES), b_f, _tile(n, (512, 256, 128)))
    tq = _tile(n, (512, 256, 128))
    c_row = c_p[:, :, :heads].transpose(0, 2, 1).reshape(bsz, heads, n // tq, tq)
    o_p = _fox_attn_prompt(q, k, v, c_p, c_row, bsz, n, heads, tq)
    n_seq = t_all - tp
    s3 = lambda a: a[tp:].reshape(n_seq, heads, dh)
    lf_pool_t = p["cache_logf_fox"].transpose(0, 1, 3, 2)
    o_s, lf_s = _fox_decode(s3(q), s3(k), s3(v), f_raw[tp:].reshape(n_seq, 1, LANES), b_f,
                            p["cache_k_fox"], p["cache_v_fox"], lf_pool_t, p["page_table"], j, heads)
    o = jnp.concatenate([o_p, o_s.reshape(n_seq, hd)], axis=0)
    outs = dict(
        k_p=k[:tp].reshape(bsz, n, heads, dh), v_p=v[:tp].reshape(bsz, n, heads, dh),
        lf_p=lf_p[:, :, :heads],
        k_s=k[tp:].reshape(n_seq, 1, heads, dh), v_s=v[tp:].reshape(n_seq, 1, heads, dh),
        lf_s=lf_s[:, :, :heads])
    return o, outs


HALO = SUBLANES


def _causal_taps(hist, u, w_ref, first_tile, tl):
    width = w_ref.shape[0]
    u = _round_bf16(u)

    @pl.when(first_tile)
    def _():
        hist[0:HALO, :] = jnp.zeros((HALO, hist.shape[1]), F32)

    @pl.when(jnp.logical_not(first_tile))
    def _():
        hist[0:HALO, :] = hist[tl:tl + HALO, :]

    hist[HALO:HALO + tl, :] = u
    out = w_ref[width - 1:width, :] * u
    for k in range(width - 1):
        shift = width - 1 - k
        out = out + w_ref[k:k + 1, :] * hist[HALO - shift:HALO - shift + tl, :]
    return out


def _short_prompt_kernel(b_ref, c_ref, h_ref, w_ref, y_ref, st_ref, hist, *, tl):
    u = c_ref[...] * h_ref[...]
    conv = _causal_taps(hist, u, w_ref, pl.program_id(1) == 0, tl)
    y_ref[...] = (b_ref[...] * conv).astype(y_ref.dtype)
    keep = w_ref.shape[0] - 1
    st_ref[0] = u[tl - keep:tl, :]


def _short_prompt(bg, cg, hg, w_conv, bsz, n, t_all, tl):
    d = bg.shape[1]
    width = w_conv.shape[0]
    nl = n // tl
    row = lambda b, l: (b * nl + l, 0)
    return pl.pallas_call(
        functools.partial(_short_prompt_kernel, tl=tl),
        grid=(bsz, nl),
        in_specs=[pl.BlockSpec((tl, d), row)] * 3 + [pl.BlockSpec((width, d), lambda b, l: (0, 0))],
        out_specs=[pl.BlockSpec((tl, d), row), pl.BlockSpec((1, width - 1, d), lambda b, l: (b, 0, 0))],
        out_shape=[jax.ShapeDtypeStruct((t_all, d), BF16), jax.ShapeDtypeStruct((bsz, width - 1, d), F32)],
        scratch_shapes=[pltpu.VMEM((HALO + tl, d), F32)],
        compiler_params=_params(("parallel", "arbitrary")),
        name="short_prompt",
    )(bg, cg, hg, w_conv)


def _short_decode_kernel(b_ref, c_ref, h_ref, s_ref, w_ref, y_ref, u_ref):
    u = c_ref[...] * h_ref[...]
    width = w_ref.shape[0]
    conv = w_ref[width - 1:width, :] * _round_bf16(u)
    for k in range(width - 1):
        conv = conv + w_ref[k:k + 1, :] * _round_bf16(s_ref[k])
    y_ref[...] = (b_ref[...] * conv).astype(y_ref.dtype)
    u_ref[...] = u


def _short_decode(bg, cg, hg, state_t, w_conv):
    n_seq, d = bg.shape
    return pl.pallas_call(
        _short_decode_kernel,
        out_shape=[jax.ShapeDtypeStruct((n_seq, d), BF16), jax.ShapeDtypeStruct((n_seq, d), F32)],
        compiler_params=pltpu.CompilerParams(vmem_limit_bytes=VMEM_LIMIT),
        name="short_decode",
    )(bg, cg, hg, state_t, w_conv)


def _short_layer(x, j, bsz, n, tp, p):
    t_all, d = x.shape
    w = p["w_in_short"][j].astype(BF16)
    bg, cg, hg = _proj(x, w, [(0, d), (d, d), (2 * d, d)], _tile(t_all, (384, 256, 128)))
    w_conv = p["w_conv_short"][j]
    y_p, st_p = _short_prompt(bg, cg, hg, w_conv, bsz, n, tp, _tile(n, (512, 256, 128)))
    state = p["state_conv_short"][j]
    y_s, u_s = _short_decode(bg[tp:], cg[tp:], hg[tp:], state.transpose(1, 0, 2), w_conv)
    y = jnp.concatenate([y_p, y_s], axis=0)
    st_s = jnp.concatenate([state[:, 1:], u_s[:, None, :]], axis=1)
    return y, dict(conv_p=st_p, conv_s=st_s)


def _gla_gate_kernel(r_ref, w_ref, b_ref, o_ref):
    z = jnp.dot(r_ref[...].astype(BF16), w_ref[...], preferred_element_type=F32) + b_ref[...]
    o_ref[...] = _log_sigmoid(z) * (1.0 / GLA_TAU)


def _gla_gate(r, w_gate, b_gate, tm):
    t = r.shape[0]
    dk = w_gate.shape[1]
    row = lambda i: (i, 0)
    full = lambda i: (0, 0)
    return pl.pallas_call(
        _gla_gate_kernel,
        grid=(t // tm,),
        in_specs=[pl.BlockSpec((tm, LANES), row), pl.BlockSpec((LANES, dk), full), pl.BlockSpec((1, dk), full)],
        out_specs=pl.BlockSpec((tm, dk), row),
        out_shape=jax.ShapeDtypeStruct((t, dk), F32),
        compiler_params=_params(("parallel",)),
        name="gla_gate",
    )(r, w_gate, b_gate)


def _rms_gate(o, g_norm, gate):
    on = o * lax.rsqrt(jnp.mean(o * o, axis=-1, keepdims=True) + RMS_EPS) * g_norm
    return on * _silu(gate)


def _gla_prompt_kernel(q_ref, k_ref, v_ref, g_ref, la_ref, gn_ref, o_ref, st_ref, state,
                       *, tl, heads, dk, dv, scale):
    l_idx = pl.program_id(1)

    @pl.when(l_idx == 0)
    def _():
        state[...] = jnp.zeros(state.shape, F32)

    cs = GLA_CHUNK
    row = lax.broadcasted_iota(I32, (cs, cs), 0)
    col = lax.broadcasted_iota(I32, (cs, cs), 1)
    causal = row >= col
    tri = jnp.where(causal, 1.0, 0.0).astype(BF16)
    for c in range(tl // cs):
        rows = slice(c * cs, (c + 1) * cs)
        for h in range(heads):
            kcols = slice(h * dk, (h + 1) * dk)
            vcols = slice(h * dv, (h + 1) * dv)
            b = _dot_exact_lhs(tri, la_ref[rows, kcols])
            b_last = b[cs - 1:cs, :]
            kh = k_ref[rows, kcols]
            q_dec = (q_ref[rows, kcols] * scale * jnp.exp(b)).astype(BF16)
            k_inv = (kh * jnp.exp(-b)).astype(BF16)
            k_end = (kh * jnp.exp(b_last - b)).astype(BF16)
            vh = v_ref[rows, vcols].astype(BF16)
            scores = jnp.where(causal, _nt(q_dec, k_inv), 0.0).astype(BF16)
            s_t = state[h]
            o = jnp.dot(scores, vh, preferred_element_type=F32) + _nt(q_dec, s_t.astype(BF16))
            state[h] = s_t * jnp.exp(b_last) + _tn(vh, k_end)
            o_ref[rows, vcols] = _rms_gate(o, gn_ref[...], g_ref[rows, vcols]).astype(o_ref.dtype)

    @pl.when(l_idx == pl.num_programs(1) - 1)
    def _():
        st_ref[0] = state[...]


def _gla_prompt(q, k, v, g, log_a, g_norm, bsz, n, heads, t_all, tl):
    dk = q.shape[1] // heads
    dv = v.shape[1] // heads
    nl = n // tl
    row = lambda b, l: (b * nl + l, 0)
    return pl.pallas_call(
        functools.partial(_gla_prompt_kernel, tl=tl, heads=heads, dk=dk, dv=dv, scale=dk ** -0.5),
        grid=(bsz, nl),
        in_specs=[pl.BlockSpec((tl, heads * dk), row), pl.BlockSpec((tl, heads * dk), row),
                  pl.BlockSpec((tl, heads * dv), row), pl.BlockSpec((tl, heads * dv), row),
                  pl.BlockSpec((tl, heads * dk), row), pl.BlockSpec((1, dv), lambda b, l: (0, 0))],
        out_specs=[pl.BlockSpec((tl, heads * dv), row),
                   pl.BlockSpec((1, heads, dv, dk), lambda b, l: (b, 0, 0, 0))],
        out_shape=[jax.ShapeDtypeStruct((t_all, heads * dv), BF16),
                   jax.ShapeDtypeStruct((bsz, heads, dv, dk), F32)],
        scratch_shapes=[pltpu.VMEM((heads, dv, dk), F32)],
        compiler_params=_params(("parallel", "arbitrary")),
        name="gla_prompt",
    )(q, k, v, g, log_a, g_norm)


def _gla_decode_kernel(s_ref, cp_ref, v_ref, g_ref, gn_ref, so_ref, o_ref, *, heads, dv, scale):
    for h in range(heads):
        cols = cp_ref[0, h]
        decay = jnp.exp(cols[:, 0:1])
        vcols = slice(h * dv, (h + 1) * dv)
        s_new = s_ref[0, h] * decay + cols[:, 1:2] * v_ref[0, :, vcols]
        so_ref[0, h] = s_new
        o = jnp.sum(_round_bf16(s_new) * _round_bf16(cols[:, 2:3] * scale), axis=0, keepdims=True)
        o_ref[0, :, vcols] = _rms_gate(o, gn_ref[...], g_ref[0, :, vcols]).astype(o_ref.dtype)


def _gla_decode(state, colpack, v, g, g_norm):
    n_seq, heads, dk, dv = state.shape
    sblk = pl.BlockSpec((1, heads, dk, dv), lambda b: (b, 0, 0, 0))
    vblk = pl.BlockSpec((1, 1, heads * dv), lambda b: (b, 0, 0))
    return pl.pallas_call(
        functools.partial(_gla_decode_kernel, heads=heads, dv=dv, scale=dk ** -0.5),
        grid=(n_seq,),
        in_specs=[sblk, pl.BlockSpec((1, heads, dk, SUBLANES), lambda b: (b, 0, 0, 0)), vblk, vblk,
                  pl.BlockSpec((1, dv), lambda b: (0, 0))],
        out_specs=[sblk, vblk],
        out_shape=[jax.ShapeDtypeStruct(state.shape, F32), jax.ShapeDtypeStruct((n_seq, 1, heads * dv), BF16)],
        compiler_params=_params(("parallel",)),
        name="gla_decode",
    )(state, colpack, v, g, g_norm)


def _gla_layer(x, j, bsz, n, tp, p):
    t_all, d = x.shape
    n_seq, heads, dk, dv = p["state_gla"].shape[1:]
    hk, hv = heads * dk, heads * dv
    rank = p["w_gate2_gla"].shape[1]
    w = _pad_cols(p["w_in_gla"][j], 2 * hk + 2 * hv + LANES).astype(BF16)
    tm = _tile(t_all, (384, 256, 128))
    q, k, v, g, r = _proj(x, w, [(0, hk), (hk, hk), (2 * hk, hv), (2 * hk + hv, hv), (2 * hk + 2 * hv, LANES)], tm)
    w_gate = jnp.pad(p["w_gate2_gla"][j], ((0, LANES - rank), (0, 0))).astype(BF16)
    log_a = _gla_gate(r, w_gate, p["b_gate_gla"][j][None, :], tm)
    g_norm = p["g_norm_gla"][j][None, :]
    o_p, st_p = _gla_prompt(q, k, v, g, log_a, g_norm, bsz, n, heads, tp, _tile(n, (256, 128, 64)))
    colpack = jnp.stack([log_a[tp:], k[tp:], q[tp:]], axis=-1).reshape(n_seq, heads, dk, 3)
    colpack = jnp.pad(colpack, ((0, 0), (0, 0), (0, 0), (0, SUBLANES - 3)))
    st_s, o_s = _gla_decode(p["state_gla"][j], colpack, v[tp:].reshape(n_seq, 1, hv),
                            g[tp:].reshape(n_seq, 1, hv), g_norm)
    o = jnp.concatenate([o_p, o_s.reshape(n_seq, hv)], axis=0)
    return o, dict(gla_p=st_p.transpose(0, 1, 3, 2), gla_s=st_s)


def _ssd_conv_prompt_kernel(x_ref, w_ref, b_ref, a_ref, st_ref, hist, *, tl):
    xbc = x_ref[...]
    conv = _causal_taps(hist, xbc, w_ref, pl.program_id(1) == 0, tl)
    a_ref[...] = _silu(conv + b_ref[...])
    keep = w_ref.shape[0] - 1
    st_ref[0] = xbc[tl - keep:tl, :]


def _ssd_conv_prompt(xbc, w_conv, b_conv, bsz, n, t_all, tl):
    ch = xbc.shape[1]
    width = w_conv.shape[0]
    nl = n // tl
    row = lambda b, l: (b * nl + l, 0)
    full = lambda b, l: (0, 0)
    return pl.pallas_call(
        functools.partial(_ssd_conv_prompt_kernel, tl=tl),
        grid=(bsz, nl),
        in_specs=[pl.BlockSpec((tl, ch), row), pl.BlockSpec((width, ch), full), pl.BlockSpec((1, ch), full)],
        out_specs=[pl.BlockSpec((tl, ch), row), pl.BlockSpec((1, width - 1, ch), lambda b, l: (b, 0, 0))],
        out_shape=[jax.ShapeDtypeStruct((t_all, ch), F32), jax.ShapeDtypeStruct((bsz, width - 1, ch), F32)],
        scratch_shapes=[pltpu.VMEM((HALO + tl, ch), F32)],
        compiler_params=_params(("parallel", "arbitrary")),
        name="ssd_conv_prompt",
    )(xbc, w_conv, b_conv)


def _ssd_conv_decode_kernel(x_ref, s_ref, w_ref, b_ref, a_ref):
    width = w_ref.shape[0]
    conv = w_ref[width - 1:width, :] * _round_bf16(x_ref[...])
    for k in range(width - 1):
        conv = conv + w_ref[k:k + 1, :] * _round_bf16(s_ref[k])
    a_ref[...] = _silu(conv + b_ref[...])


def _ssd_conv_decode(xbc, state_t, w_conv, b_conv):
    return pl.pallas_call(
        _ssd_conv_decode_kernel,
        out_shape=jax.ShapeDtypeStruct(xbc.shape, F32),
        compiler_params=pltpu.CompilerParams(vmem_limit_bytes=VMEM_LIMIT),
        name="ssd_conv_decode",
    )(xbc, state_t, w_conv, b_conv)


def _group_rms_gate(y, z, g_norm):
    yg = y * _silu(z)
    return yg * lax.rsqrt(jnp.mean(yg * yg, axis=-1, keepdims=True) + RMS_EPS) * g_norm


def _ssd_prompt_kernel(xs_ref, bm_ref, cm_ref, dt_ref, z_ref, dtb_ref, alog_ref, dsk_ref, gn_ref,
                       y_ref, st_ref, state, ybuf, *, q, heads, hd, ns, groups):
    l_idx = pl.program_id(1)

    @pl.when(l_idx == 0)
    def _():
        state[...] = jnp.zeros(state.shape, F32)

    pw = 2 * hd
    dt = _softplus(dt_ref[...] + dtb_ref[...])
    a = -jnp.exp(alog_ref[...])
    row = lax.broadcasted_iota(I32, (q, q), 0)
    col = lax.broadcasted_iota(I32, (q, q), 1)
    causal = row >= col
    tri = jnp.where(causal, 1.0, 0.0).astype(BF16)
    cum = _dot_exact_lhs(tri, dt * a)
    cum_t = cum.T
    e_cum = jnp.exp(cum)
    w_end = jnp.exp(cum[q - 1:q, :] - cum)
    dsk = dsk_ref[...]
    lo = lax.broadcasted_iota(I32, (q, pw), 1) < hd
    lo_rows = lax.broadcasted_iota(I32, (pw, ns), 0) < hd
    pairs_per_group = heads // groups // 2

    def pick(mat, h0):
        return jnp.where(lo, mat[:, h0:h0 + 1], mat[:, h0 + 1:h0 + 2])

    for g in range(groups):
        cmb = cm_ref[:, g * ns:(g + 1) * ns].astype(BF16)
        bmb = bm_ref[:, g * ns:(g + 1) * ns].astype(BF16)
        cb = _nt(cmb, bmb)
        for pp in range(pairs_per_group):
            pr = g * pairs_per_group + pp
            h0 = 2 * pr
            cols = slice(pr * pw, (pr + 1) * pw)
            x_pair = xs_ref[:, cols]
            xdt = x_pair * pick(dt, h0)

            def decay_mix(h):
                return (cb * jnp.exp(jnp.where(causal, cum[:, h:h + 1] - cum_t[h:h + 1, :], -jnp.inf))).astype(BF16)

            y = (jnp.dot(decay_mix(h0), jnp.where(lo, xdt, 0.0).astype(BF16), preferred_element_type=F32)
                 + jnp.dot(decay_mix(h0 + 1), jnp.where(lo, 0.0, xdt).astype(BF16), preferred_element_type=F32))
            s_pair = state[pr]
            y = y + _nt(cmb, s_pair.astype(BF16)) * pick(e_cum, h0)
            dec = jnp.where(lo_rows, jnp.exp(cum_t[h0:h0 + 1, q - 1:q]), jnp.exp(cum_t[h0 + 1:h0 + 2, q - 1:q]))
            state[pr] = s_pair * dec + _tn((xdt * pick(w_end, h0)).astype(BF16), bmb)
            ybuf[:, cols] = y + x_pair * pick(dsk, h0)

    gw = heads * hd // groups
    for g in range(groups):
        cols = slice(g * gw, (g + 1) * gw)
        y_ref[:, cols] = _group_rms_gate(ybuf[:, cols], z_ref[:, cols], gn_ref[:, cols]).astype(y_ref.dtype)

    @pl.when(l_idx == pl.num_programs(1) - 1)
    def _():
        st_ref[0] = state[...]


def _ssd_prompt(act, dt_raw, z, dt_bias, a_log, d_skip, g_norm, bsz, n, heads, hd, ns, groups, t_all, q):
    inner = heads * hd
    gn = groups * ns
    nl = n // q
    row = lambda b, l: (b * nl + l, 0)
    full = lambda b, l: (0, 0)
    n_pairs = heads // 2
    return pl.pallas_call(
        functools.partial(_ssd_prompt_kernel, q=q, heads=heads, hd=hd, ns=ns, groups=groups),
        grid=(bsz, nl),
        in_specs=[pl.BlockSpec((q, inner), row),
                  pl.BlockSpec((q, gn), lambda b, l: (b * nl + l, inner // gn)),
                  pl.BlockSpec((q, gn), lambda b, l: (b * nl + l, inner // gn + 1)),
                  pl.BlockSpec((q, LANES), row), pl.BlockSpec((q, inner), row),
                  pl.BlockSpec((1, LANES), full), pl.BlockSpec((1, LANES), full), pl.BlockSpec((1, LANES), full),
                  pl.BlockSpec((1, inner), full)],
        out_specs=[pl.BlockSpec((q, inner), row),
                   pl.BlockSpec((1, n_pairs, 2 * hd, ns), lambda b, l: (b, 0, 0, 0))],
        out_shape=[jax.ShapeDtypeStruct((t_all, inner), BF16),
                   jax.ShapeDtypeStruct((bsz, n_pairs, 2 * hd, ns), F32)],
        scratch_shapes=[pltpu.VMEM((n_pairs, 2 * hd, ns), F32), pltpu.VMEM((q, inner), F32)],
        compiler_params=_params(("parallel", "arbitrary")),
        name="ssd_prompt",
    )(act, act, act, dt_raw, z, dt_bias, a_log, d_skip, g_norm)


SSD_PACK_ROWS = 16


def _ssd_decode_kernel(s_ref, pack_ref, b_ref, c_ref, z_ref, gn_ref, so_ref, y_ref, *, n_pairs, ns, groups):
    pt = pack_ref[0].T
    pw = s_ref.shape[2]
    lane = lax.broadcasted_iota(I32, (pw, LANES), 1)
    y_cols = jnp.zeros((pw, LANES), F32)
    pairs_per_group = n_pairs // groups
    for pr in range(n_pairs):
        g = pr // pairs_per_group
        col = lambda k: pt[:, k * SSD_PACK_ROWS + pr:k * SSD_PACK_ROWS + pr + 1]
        x_col = col(0)
        dt = _softplus(col(1) + col(2))
        dec = jnp.exp(dt * -jnp.exp(col(3)))
        b_row = b_ref[0, :, g * ns:(g + 1) * ns]
        c_row = c_ref[0, :, g * ns:(g + 1) * ns]
        s_new = s_ref[0, pr] * dec + (x_col * dt) * b_row
        so_ref[0, pr] = s_new
        y_col = jnp.sum(_round_bf16(s_new) * _round_bf16(c_row), axis=1, keepdims=True) + x_col * col(4)
        y_cols = jnp.where(lane == pr, y_col, y_cols)
    y_rows = y_cols.T[0:n_pairs, :]
    yg = y_rows * _silu(z_ref[0])
    sq = jnp.sum(yg * yg, axis=1, keepdims=True)
    out = jnp.zeros(yg.shape, F32)
    sub = lax.broadcasted_iota(I32, yg.shape, 0)
    sub1 = lax.broadcasted_iota(I32, sq.shape, 0)
    for g in range(groups):
        lo_r, hi_r = g * pairs_per_group, (g + 1) * pairs_per_group
        in_g1 = jnp.logical_and(sub1 >= lo_r, sub1 < hi_r)
        ms = jnp.sum(jnp.where(in_g1, sq, 0.0), axis=0, keepdims=True) / (pairs_per_group * pw)
        out = jnp.where(jnp.logical_and(sub >= lo_r, sub < hi_r), yg * lax.rsqrt(ms + RMS_EPS), out)
    y_ref[0] = (out * gn_ref[...]).astype(y_ref.dtype)


def _ssd_decode(state, pack, b_rows, c_rows, z, g_norm, groups):
    n_seq, n_pairs, pw, ns = state.shape
    sblk = pl.BlockSpec((1, n_pairs, pw, ns), lambda b: (b, 0, 0, 0))
    rblk = pl.BlockSpec((1, 1, groups * ns), lambda b: (b, 0, 0))
    zblk = pl.BlockSpec((1, n_pairs, pw), lambda b: (b, 0, 0))
    return pl.pallas_call(
        functools.partial(_ssd_decode_kernel, n_pairs=n_pairs, ns=ns, groups=groups),
        grid=(n_seq,),
        in_specs=[sblk, pl.BlockSpec((1, LANES, LANES), lambda b: (b, 0, 0)), rblk, rblk, zblk,
                  pl.BlockSpec((n_pairs, pw), lambda b: (0, 0))],
        out_specs=[sblk, zblk],
        out_shape=[jax.ShapeDtypeStruct(state.shape, F32), jax.ShapeDtypeStruct((n_seq, n_pairs, pw), BF16)],
        compiler_params=_params(("parallel",)),
        name="ssd_decode",
    )(state, pack, b_rows, c_rows, z, g_norm)


def _ssd_layer(x, j, bsz, n, tp, p):
    t_all, d = x.shape
    n_seq, heads, hd, ns = p["state_ssm"].shape[1:]
    groups = SSD_GROUPS
    inner = heads * hd
    gn = groups * ns
    ch = inner + 2 * gn
    n_pairs = heads // 2
    pw = 2 * hd
    assert pw == LANES and ns == LANES and n_pairs == SSD_PACK_ROWS and inner % gn == 0
    w = _pad_cols(p["w_in_ssd"][j], inner + ch + LANES).astype(BF16)
    z, xbc, dt_raw = _proj(x, w, [(0, inner), (inner, ch), (inner + ch, LANES)], _tile(t_all, (256, 128)))
    w_conv = p["w_conv_ssd"][j]
    b_conv = p["b_conv_ssd"][j][None, :]
    lane_row = lambda v: _pad_cols(v[None, :], LANES)
    g_norm = p["g_norm_ssd"][j]
    act, conv_p = _ssd_conv_prompt(xbc, w_conv, b_conv, bsz, n, tp, _tile(n, (256, 128)))
    y_p, ssm_p = _ssd_prompt(act, dt_raw, z, lane_row(p["dt_bias_ssd"][j]), lane_row(p["a_log_ssd"][j]),
                           lane_row(p["d_skip_ssd"][j]), g_norm[None, :], bsz, n, heads, hd, ns, groups,
                           tp, _tile(n, (128,)))
    conv_state = p["state_conv_ssd"][j]
    xbc_s = xbc[tp:]
    act_s = _ssd_conv_decode(xbc_s, conv_state.transpose(1, 0, 2), w_conv, b_conv)
    per_row = lambda v: jnp.broadcast_to(jnp.repeat(v, hd, axis=-1).reshape(-1, n_pairs, pw), (n_seq, n_pairs, pw))
    pack = jnp.concatenate([
        act_s[:, :inner].reshape(n_seq, n_pairs, pw),
        per_row(dt_raw[tp:, :heads]), per_row(p["dt_bias_ssd"][j][None, :]),
        per_row(p["a_log_ssd"][j][None, :]), per_row(p["d_skip_ssd"][j][None, :])], axis=1)
    pack = jnp.pad(pack, ((0, 0), (0, LANES - pack.shape[1]), (0, 0)))
    ssm_s, y_s = _ssd_decode(p["state_ssm"][j].reshape(n_seq, n_pairs, pw, ns), pack,
                             act_s[:, inner:inner + gn].reshape(n_seq, 1, gn),
                             act_s[:, inner + gn:].reshape(n_seq, 1, gn),
                             z[tp:].reshape(n_seq, n_pairs, pw), g_norm.reshape(n_pairs, pw), groups)
    y = jnp.concatenate([y_p, y_s.reshape(n_seq, inner)], axis=0)
    conv_s = jnp.concatenate([conv_state[:, 1:], xbc_s[:, None, :]], axis=1)
    return y, dict(ssm_p=ssm_p.reshape(bsz, heads, hd, ns), ssm_s=ssm_s.reshape(n_seq, heads, hd, ns),
                   conv_p=conv_p, conv_s=conv_s)


_LAYERS = (_fox_layer, _short_layer, _gla_layer, _ssd_layer)
_OUT_ORDER = ("k", "v", "lf", "conv_short", "gla", "ssm", "conv_ssd")


def kernel(x_prompt, x_sample, cache_k_fox, cache_v_fox, cache_logf_fox, page_table, state_conv_short, state_gla, state_ssm, state_conv_ssd, w_in_fox, b_forget_fox, w_out_fox, w_in_short, w_conv_short, w_out_short, w_in_gla, w_gate2_gla, b_gate_gla, g_norm_gla, w_out_gla, w_in_ssd, w_conv_ssd, b_conv_ssd, dt_bias_ssd, a_log_ssd, d_skip_ssd, g_norm_ssd, w_out_ssd, ln_mix_g, ln_mix_b, ln_ffn_g, ln_ffn_b, w_router, b_router, w_gate_up, b_gate_up, w_down, b_down):
    p = dict(locals())
    bsz, n, d = x_prompt.shape
    n_seq, n_dec, _ = x_sample.shape
    assert n_dec == 1, "the sample group decodes one token per sequence"
    depth = ln_mix_g.shape[0]
    alpha = (2 * depth) ** 0.25
    tp = bsz * n
    x = jnp.concatenate([x_prompt.reshape(tp, d), x_sample.reshape(n_seq, d)], axis=0)
    w_outs = (w_out_fox, w_out_short, w_out_gla, w_out_ssd)
    acc = {name + sfx: [] for name in _OUT_ORDER for sfx in ("_p", "_s")}
    for i in range(depth):
        kind, j = i % len(_LAYERS), i // len(_LAYERS)
        y_mix, outs = _LAYERS[kind](x, j, bsz, n, tp, p)
        rename = {"conv_p": ("conv_short_p" if kind == 1 else "conv_ssd_p"),
                  "conv_s": ("conv_short_s" if kind == 1 else "conv_ssd_s")}
        for key, val in outs.items():
            acc[rename.get(key, key)].append(val)
        x = _post_blocks(x, y_mix, w_outs[kind][j], i, alpha, p)
    stack = lambda name: jnp.stack(acc[name])
    return (x[:tp].reshape(bsz, n, d), x[tp:].reshape(n_seq, n_dec, d),
            *[stack(name + "_p") for name in _OUT_ORDER],
            *[stack(name + "_s") for name in _OUT_ORDER])
```

```python
import functools
import math

import jax
import jax.numpy as jnp
from jax import lax
from jax.experimental import pallas as pl
from jax.experimental.pallas import tpu as pltpu

F32 = jnp.float32
BF16 = jnp.bfloat16
I32 = jnp.int32

LANES = 128
SUBLANES = 8
VMEM_LIMIT = 56 * 1024 * 1024

LN_EPS = 1e-5
RMS_EPS = 1e-6
TOP_K = 4
SWIGLU_LIMIT = 7.0
SWIGLU_ALPHA = 1.702
GLA_TAU = 16.0
GLA_CHUNK = 64
SSD_GROUPS = 4
MOE_BLOCK = 256
FFN_UP_PHASES = 2
FFN_DOWN_PHASES = 2
GATHER_AHEAD = 2


def _params(semantics):
    return pltpu.CompilerParams(dimension_semantics=semantics, vmem_limit_bytes=VMEM_LIMIT)


def _tile(n, candidates):
    for c in candidates:
        if n % c == 0:
            return c
    raise ValueError(f"no tile for {n} among {candidates}")


def _round_up(n, m):
    return (n + m - 1) // m * m


def _log_sigmoid(z):
    return jnp.minimum(z, 0.0) - jnp.log(1.0 + jnp.exp(-jnp.abs(z)))


def _softplus(z):
    return jnp.maximum(z, 0.0) + jnp.log(1.0 + jnp.exp(-jnp.abs(z)))


def _silu(z):
    return z * jax.nn.sigmoid(z)


def _split3(x):
    hi = x.astype(BF16)
    r = x - hi.astype(F32)
    mid = r.astype(BF16)
    lo = (r - mid.astype(F32)).astype(BF16)
    return hi, mid, lo


def _dot_exact_lhs(a01, x):
    return sum(jnp.dot(a01, p, preferred_element_type=F32) for p in _split3(x))


def _dot_exact_rhs(x, a01):
    return sum(jnp.dot(p, a01, preferred_element_type=F32) for p in _split3(x))


def _round_bf16(x):
    return x.astype(BF16).astype(F32)


def _nt(a, b):
    return lax.dot_general(a, b, (((1,), (1,)), ((), ())), preferred_element_type=F32)


def _tn(a, b):
    return lax.dot_general(a, b, (((0,), (0,)), ((), ())), preferred_element_type=F32)


def _proj_kernel(x_ref, w_ref, *out_refs, segs, chunk):
    xb = x_ref[...].astype(BF16)
    for o_ref, (start, width) in zip(out_refs, segs):
        for c in range(0, width, chunk):
            cw = min(chunk, width - c)
            o_ref[:, c:c + cw] = jnp.dot(
                xb, w_ref[:, start + c:start + c + cw], preferred_element_type=F32).astype(o_ref.dtype)


def _proj(x, w, segs, tm):
    t, d = x.shape
    n = w.shape[1]
    assert all(s % LANES == 0 and wd % LANES == 0 for s, wd in segs)
    return pl.pallas_call(
        functools.partial(_proj_kernel, segs=tuple(segs), chunk=512),
        grid=(t // tm,),
        in_specs=[pl.BlockSpec((tm, d), lambda i: (i, 0)),
                  pl.BlockSpec((d, n), lambda i: (0, 0))],
        out_specs=[pl.BlockSpec((tm, wd), lambda i: (i, 0)) for _, wd in segs],
        out_shape=[jax.ShapeDtypeStruct((t, wd), F32) for _, wd in segs],
        compiler_params=_params(("parallel",)),
        name="proj",
    )(x, w)


def _pad_cols(w, n):
    return jnp.pad(w, ((0, 0), (0, n - w.shape[1])))


def _layer_norm_rows(z, g, b):
    mu = jnp.mean(z, axis=-1, keepdims=True)
    zc = z - mu
    var = jnp.mean(zc * zc, axis=-1, keepdims=True)
    return zc * lax.rsqrt(var + LN_EPS) * g + b


def _mix_ln_router_kernel(y_ref, w_ref, x_ref, g_ref, b_ref, wr_ref, br_ref,
                          x1_ref, topi_ref, gate_ref, rank_ref, cnt_ref, *, alpha, n_experts):
    tm = x_ref.shape[0]
    mix = jnp.dot(y_ref[...].astype(BF16), w_ref[...], preferred_element_type=F32)
    x1 = _layer_norm_rows(alpha * x_ref[...] + mix, g_ref[...], b_ref[...])
    x1_ref[...] = x1
    logits = jnp.dot(x1.astype(BF16), wr_ref[...], preferred_element_type=F32) + br_ref[...]
    lane = lax.broadcasted_iota(I32, logits.shape, 1)
    neg_inf = jnp.float32(-jnp.inf)
    cur = jnp.where(lane < n_experts, logits, neg_inf)
    topi = jnp.zeros(logits.shape, I32)
    chosen = jnp.zeros(logits.shape, F32)
    vals, ids = [], []
    for k in range(TOP_K):
        m = jnp.max(cur, axis=1, keepdims=True)
        idx = jnp.min(jnp.where(cur == m, lane, LANES), axis=1, keepdims=True)
        vals.append(m)
        ids.append(idx)
        topi = jnp.where(lane == k, idx, topi)
        chosen = jnp.where(lane == idx, 1.0, chosen)
        cur = jnp.where(lane == idx, neg_inf, cur)
    es = [jnp.exp(v - vals[0]) for v in vals]
    inv = 1.0 / sum(es)
    gate = jnp.zeros(logits.shape, F32)
    for k in range(TOP_K):
        gate = jnp.where(lane == k, es[k] * inv, gate)
    topi_ref[...] = topi
    gate_ref[...] = gate
    earlier = jnp.where(lax.broadcasted_iota(I32, (tm, tm), 0) > lax.broadcasted_iota(I32, (tm, tm), 1),
                        1.0, 0.0).astype(BF16)
    before = jnp.dot(earlier, chosen.astype(BF16), preferred_element_type=F32)
    rank = jnp.zeros(logits.shape, I32)
    for k in range(TOP_K):
        r_k = jnp.sum(jnp.where(lane == ids[k], before, 0.0), axis=1, keepdims=True)
        rank = jnp.where(lane == k, r_k.astype(I32), rank)
    rank_ref[...] = rank
    cnt_ref[0] = jnp.broadcast_to(jnp.sum(chosen, axis=0, keepdims=True), (SUBLANES, LANES)).astype(I32)


def _mix_ln_router(y, w_out, x, ln_g, ln_b, w_r, b_r, alpha, n_experts, tm):
    t, d = x.shape
    kdim = y.shape[1]
    row = lambda i: (i, 0)
    full = lambda i: (0, 0)
    return pl.pallas_call(
        functools.partial(_mix_ln_router_kernel, alpha=alpha, n_experts=n_experts),
        grid=(t // tm,),
        in_specs=[pl.BlockSpec((tm, kdim), row), pl.BlockSpec((kdim, d), full),
                  pl.BlockSpec((tm, d), row), pl.BlockSpec((1, d), full), pl.BlockSpec((1, d), full),
                  pl.BlockSpec((d, LANES), full), pl.BlockSpec((1, LANES), full)],
        out_specs=[pl.BlockSpec((tm, d), row), pl.BlockSpec((tm, LANES), row),
                   pl.BlockSpec((tm, LANES), row), pl.BlockSpec((tm, LANES), row),
                   pl.BlockSpec((1, SUBLANES, LANES), lambda i: (i, 0, 0))],
        out_shape=[jax.ShapeDtypeStruct((t, d), F32), jax.ShapeDtypeStruct((t, LANES), I32),
                   jax.ShapeDtypeStruct((t, LANES), F32), jax.ShapeDtypeStruct((t, LANES), I32),
                   jax.ShapeDtypeStruct((t // tm, SUBLANES, LANES), I32)],
        compiler_params=_params(("parallel",)),
        name="mix_ln_router",
    )(y, w_out, x, ln_g, ln_b, w_r, b_r)


def _route_tables(topi, rank, tile_cnt, n_experts, bm):
    t = topi.shape[0]
    n_pairs = t * TOP_K
    tm = t // tile_cnt.shape[0]
    e_ids = jnp.arange(n_experts, dtype=I32)
    cnt = tile_cnt[:, 0, :n_experts]
    tile_off = jnp.cumsum(cnt, axis=0) - cnt
    counts = jnp.sum(cnt, axis=0)
    starts = jnp.cumsum(counts) - counts
    padded = (counts + bm - 1) // bm * bm
    pad_ends = jnp.cumsum(padded)
    pad_starts = pad_ends - padded
    expert = topi[:, :TOP_K]
    base = jnp.repeat(pad_starts[None, :] + tile_off, tm, axis=0)
    hit = expert[:, :, None] == e_ids[None, None, :]
    pos = (jnp.sum(jnp.where(hit, base[:, None, :], 0), axis=2) + rank[:, :TOP_K]).reshape(-1).astype(I32)
    n_blocks = (n_pairs + bm - 1) // bm + n_experts + GATHER_AHEAD
    block_start = jnp.arange(n_blocks, dtype=I32) * bm
    block_expert = jnp.minimum(jnp.sum((pad_ends[None, :] <= block_start[:, None]).astype(I32), axis=1),
                               n_experts - 1).astype(I32)
    n_used = (pad_ends[-1] // bm).astype(I32).reshape(1)
    pair_bits = max(1, (n_pairs - 1).bit_length())
    assert (n_experts << pair_bits) < 2 ** 31
    keyed = jnp.sort(expert.reshape(-1) * (1 << pair_bits) + jnp.arange(n_pairs, dtype=I32))
    order = keyed & ((1 << pair_bits) - 1)
    e_row = jnp.repeat(block_expert, bm)
    off = jnp.arange(n_blocks * bm, dtype=I32) - pad_starts[e_row]
    src = jnp.clip(starts[e_row] + off, 0, n_pairs - 1)
    row_tok = jnp.where(off < counts[e_row], order[src] // TOP_K, 0).astype(I32)
    return block_expert, row_tok, n_used, pos, n_blocks


def _moe_ffn_kernel(be_ref, rt_ref, nu_ref, x_hbm, wgu_ref, bgu_ref, wd_ref, bd_ref,
                    y_ref, xbuf, sem, wgu_b, wd_b, xb, act, *, bm, d_ff):
    i = pl.program_id(0)
    n_used = nu_ref[0]
    n_slots = GATHER_AHEAD + 1
    slot = i % n_slots
    ahead_slot = (i + GATHER_AHEAD) % n_slots

    def row_copy(blk, r, dst_slot):
        tok = rt_ref[blk * bm + r]
        return pltpu.make_async_copy(x_hbm.at[pl.ds(tok, 1), :],
                                     xbuf.at[dst_slot, pl.ds(r, 1), :], sem.at[dst_slot])

    def wait_rows(dst_slot):
        pltpu.make_async_copy(x_hbm.at[pl.ds(0, bm), :], xbuf.at[dst_slot], sem.at[dst_slot]).wait()

    @pl.when(i == 0)
    def _():
        for blk in range(GATHER_AHEAD):
            def body(r, carry):
                row_copy(blk, r, blk).start()
                return carry
            lax.fori_loop(0, bm, body, 0)

    expert_changed = jnp.logical_or(i == 0, be_ref[jnp.maximum(i - 1, 0)] != be_ref[i])

    @pl.when(jnp.logical_and(i < n_used, expert_changed))
    def _():
        wgu_b[...] = wgu_ref[0, 0].astype(BF16)
        wd_b[...] = wd_ref[0, 0].astype(BF16)

    @pl.when(jnp.logical_and(i >= n_used, i < n_used + GATHER_AHEAD))
    def _():
        wait_rows(slot)

    @pl.when(i < n_used)
    def _():
        wait_rows(slot)
        xb[...] = xbuf[slot].astype(BF16)
        d = xb.shape[1]
        phases = [("up", c) for c in range(FFN_UP_PHASES)] + [("down", c) for c in range(FFN_DOWN_PHASES)]
        per_group = bm // (len(phases) - 1)
        uw, dw = d_ff // FFN_UP_PHASES, d // FFN_DOWN_PHASES
        zero = None
        for n, (kind, c) in enumerate(phases):
            if kind == "up":
                hg = (jnp.dot(xb[...], wgu_b[:, c * uw:(c + 1) * uw], preferred_element_type=F32)
                      + bgu_ref[0, 0, :, c * uw:(c + 1) * uw])
                hl = (jnp.dot(xb[...], wgu_b[:, d_ff + c * uw:d_ff + (c + 1) * uw], preferred_element_type=F32)
                      + bgu_ref[0, 0, :, d_ff + c * uw:d_ff + (c + 1) * uw])
                if zero is not None:
                    hg = hg + zero
                glu = jnp.minimum(hg, SWIGLU_LIMIT)
                lin = jnp.clip(hl, -SWIGLU_LIMIT, SWIGLU_LIMIT)
                act[:, c * uw:(c + 1) * uw] = (glu * jax.nn.sigmoid(SWIGLU_ALPHA * glu) * (lin + 1.0)).astype(BF16)
            else:
                y = (jnp.dot(act[...], wd_b[:, c * dw:(c + 1) * dw], preferred_element_type=F32)
                     + bd_ref[0, 0, :, c * dw:(c + 1) * dw])
                y_ref[:, c * dw:(c + 1) * dw] = y + zero
            if n < len(phases) - 1:
                for r in range(n * per_group, bm if n == len(phases) - 2 else (n + 1) * per_group):
                    row_copy(i + GATHER_AHEAD, r, ahead_slot).start()
                zero = jnp.minimum(jnp.abs(xbuf[slot, 0:1, 0:LANES]), 0.0)[:, 0:1]

    @pl.when(i >= n_used)
    def _():
        y_ref[...] = jnp.zeros(y_ref.shape, y_ref.dtype)


def _moe_ffn(x1, block_expert, row_tok, n_used, n_blocks, layer, w_gate_up, b_gate_up, w_down, b_down, bm):
    t, d = x1.shape
    depth, n_e, _, two_f = w_gate_up.shape
    d_ff = two_f // 2
    wmap = lambda i, be, rt, nu: (layer, be[i], 0, 0)
    grid_spec = pltpu.PrefetchScalarGridSpec(
        num_scalar_prefetch=3,
        grid=(n_blocks,),
        in_specs=[pl.BlockSpec(memory_space=pl.ANY),
                  pl.BlockSpec((1, 1, d, two_f), wmap),
                  pl.BlockSpec((1, 1, 1, two_f), wmap),
                  pl.BlockSpec((1, 1, d_ff, d), wmap),
                  pl.BlockSpec((1, 1, 1, d), wmap)],
        out_specs=pl.BlockSpec((bm, d), lambda i, be, rt, nu: (i, 0)),
        scratch_shapes=[pltpu.VMEM((GATHER_AHEAD + 1, bm, d), F32), pltpu.SemaphoreType.DMA((GATHER_AHEAD + 1,)),
                        pltpu.VMEM((d, two_f), BF16), pltpu.VMEM((d_ff, d), BF16), pltpu.VMEM((bm, d), BF16),
                        pltpu.VMEM((bm, d_ff), BF16)],
    )
    return pl.pallas_call(
        functools.partial(_moe_ffn_kernel, bm=bm, d_ff=d_ff),
        grid_spec=grid_spec,
        out_shape=jax.ShapeDtypeStruct((n_blocks * bm, d), F32),
        compiler_params=_params(("arbitrary",)),
        name="moe_ffn",
    )(block_expert, row_tok, n_used, x1, w_gate_up, b_gate_up.reshape(depth, n_e, 1, two_f),
      w_down, b_down.reshape(depth, n_e, 1, d))


def _moe_combine_kernel(pos_ref, y_hbm, gate_ref, x_ref, g_ref, b_ref, *rest, tm, alpha, segs, chunk):
    if segs:
        w_ref, o_ref, *proj_refs = rest[:-2]
    else:
        (o_ref,), proj_refs = rest[:-2], []
    buf, sem = rest[-2:]
    i = pl.program_id(0)
    nb = pl.num_programs(0)
    n_slots = GATHER_AHEAD + 1
    slot = i % n_slots
    ahead_slot = (i + GATHER_AHEAD) % n_slots
    n_rows = TOP_K * tm

    def row_copy(tile, j, dst_slot):
        r, k = j // TOP_K, j % TOP_K
        p = pos_ref[(tile * tm + r) * TOP_K + k]
        return pltpu.make_async_copy(y_hbm.at[pl.ds(p, 1), :],
                                     buf.at[dst_slot, pl.ds(k * tm + r, 1), :], sem.at[dst_slot])

    def wait_rows(dst_slot):
        pltpu.make_async_copy(y_hbm.at[pl.ds(0, n_rows), :], buf.at[dst_slot], sem.at[dst_slot]).wait()

    @pl.when(i == 0)
    def _():
        for tile in range(GATHER_AHEAD):
            def body(r, carry):
                for k in range(TOP_K):
                    row_copy(jnp.minimum(tile, nb - 1), r * TOP_K + k, tile).start()
                return carry
            lax.fori_loop(0, tm, body, 0)

    wait_rows(slot)
    gate = gate_ref[...]
    ffn = gate[:, 0:1] * buf[slot, pl.ds(0, tm), :]
    for k in range(1, TOP_K):
        ffn = ffn + gate[:, k:k + 1] * buf[slot, pl.ds(k * tm, tm), :]
    x2 = _layer_norm_rows(alpha * x_ref[...] + ffn, g_ref[...], b_ref[...])
    o_ref[...] = x2

    chunks = [(o, s, c, min(chunk, wd - c)) for o, (s, wd) in zip(proj_refs, segs) for c in range(0, wd, chunk)]
    n_groups = max(1, len(chunks))
    per_group = -(-n_rows // n_groups)
    next_tile = jnp.minimum(i + GATHER_AHEAD, nb - 1)
    xb = x2.astype(BF16)
    zero = None
    for n in range(n_groups):
        if chunks:
            o, s, c, cw = chunks[n]
            res = jnp.dot(xb, w_ref[:, s + c:s + c + cw], preferred_element_type=F32)
            o[:, c:c + cw] = res if zero is None else res + zero
        for j in range(n * per_group, min(n_rows, (n + 1) * per_group)):
            row_copy(next_tile, j, ahead_slot).start()
        zero = jnp.minimum(jnp.abs(buf[slot, 0:1, 0:LANES]), 0.0)[:, 0:1]

    @pl.when(i == nb - 1)
    def _():
        for extra in range(1, GATHER_AHEAD + 1):
            wait_rows((i + extra) % n_slots)


def _moe_combine(y_rows, pos, gates, x1, ln_g, ln_b, alpha, tm, w_next=None, segs=None):
    t, d = x1.shape
    segs = tuple(segs or ())
    row = lambda i, p: (i, 0)
    full = lambda i, p: (0, 0)
    in_specs = [pl.BlockSpec(memory_space=pl.ANY), pl.BlockSpec((tm, LANES), row),
                pl.BlockSpec((tm, d), row), pl.BlockSpec((1, d), full), pl.BlockSpec((1, d), full)]
    args = [pos, y_rows, gates, x1, ln_g, ln_b]
    if segs:
        assert all(s % LANES == 0 and wd % LANES == 0 for s, wd in segs)
        in_specs.append(pl.BlockSpec(w_next.shape, full))
        args.append(w_next)
    grid_spec = pltpu.PrefetchScalarGridSpec(
        num_scalar_prefetch=1,
        grid=(t // tm,),
        in_specs=in_specs,
        out_specs=[pl.BlockSpec((tm, d), row)] + [pl.BlockSpec((tm, wd), row) for _, wd in segs],
        scratch_shapes=[pltpu.VMEM((GATHER_AHEAD + 1, TOP_K * tm, d), F32),
                        pltpu.SemaphoreType.DMA((GATHER_AHEAD + 1,))],
    )
    outs = pl.pallas_call(
        functools.partial(_moe_combine_kernel, tm=tm, alpha=alpha, segs=segs, chunk=512),
        grid_spec=grid_spec,
        out_shape=[jax.ShapeDtypeStruct((t, d), F32)] + [jax.ShapeDtypeStruct((t, wd), F32) for _, wd in segs],
        compiler_params=_params(("arbitrary",)),
        name="moe_combine",
    )(*args)
    return tuple(outs) if segs else outs[0]


def _post_blocks(x, y_mix, w_out, i, alpha, p, next_proj=None):
    n_experts = p["w_router"].shape[-1]
    d = x.shape[1]
    tm = _tile(x.shape[0], (384, 256, 128))
    w_r = _pad_cols(p["w_router"][i], LANES).astype(BF16)
    b_r = _pad_cols(p["b_router"][i][None, :].astype(F32), LANES)
    x1, topi, gates, rank, tile_cnt = _mix_ln_router(
        y_mix, w_out.astype(BF16), x, p["ln_mix_g"][i].reshape(1, d), p["ln_mix_b"][i].reshape(1, d),
        w_r, b_r, alpha, n_experts, tm)
    block_expert, row_tok, n_used, pos, n_blocks = _route_tables(topi, rank, tile_cnt, n_experts, MOE_BLOCK)
    y_rows = _moe_ffn(x1, block_expert, row_tok, n_used, n_blocks, i, p["w_gate_up"], p["b_gate_up"],
                      p["w_down"], p["b_down"], MOE_BLOCK)
    w_next, segs = next_proj if next_proj is not None else (None, None)
    return _moe_combine(y_rows, pos, gates, x1, p["ln_ffn_g"][i].reshape(1, d),
                        p["ln_ffn_b"][i].reshape(1, d), alpha, _tile(x.shape[0], (128,)), w_next, segs)


def _fox_gate_kernel(f_ref, bf_ref, lf_ref, c_ref, carry, *, tl):
    @pl.when(pl.program_id(1) == 0)
    def _():
        carry[...] = jnp.zeros(carry.shape, F32)

    lf = _log_sigmoid(f_ref[0] + bf_ref[...])
    lf_ref[0] = lf
    row = lax.broadcasted_iota(I32, (tl, tl), 0)
    col = lax.broadcasted_iota(I32, (tl, tl), 1)
    tri = jnp.where(row >= col, 1.0, 0.0).astype(BF16)
    c = _dot_exact_lhs(tri, lf) + carry[0:1, :]
    c_ref[0] = c
    carry[0:1, :] = c[tl - 1:tl, :]


def _fox_gate(f_raw, b_f, tl):
    bsz, n, _ = f_raw.shape
    blk = pl.BlockSpec((1, tl, LANES), lambda b, l: (b, l, 0))
    return pl.pallas_call(
        functools.partial(_fox_gate_kernel, tl=tl),
        grid=(bsz, n // tl),
        in_specs=[blk, pl.BlockSpec((1, LANES), lambda b, l: (0, 0))],
        out_specs=[blk, blk],
        out_shape=[jax.ShapeDtypeStruct(f_raw.shape, F32)] * 2,
        scratch_shapes=[pltpu.VMEM((SUBLANES, LANES), F32)],
        compiler_params=_params(("parallel", "arbitrary")),
        name="fox_gate",
    )(f_raw, b_f)


def _fox_attn_kernel(q_ref, k_ref, v_ref, cc_ref, cr_ref, o_ref, kb, vb, s_buf, m_s, l_s, acc, *, scale, tq):
    head = pl.program_id(1)
    qi = pl.program_id(2)

    @pl.when(qi == 0)
    def _():
        kb[...] = k_ref[...].astype(BF16)
        vb[...] = v_ref[...].astype(BF16)

    qs = (q_ref[...] * scale).astype(BF16)
    lane = lax.broadcasted_iota(I32, (tq, LANES), 1)
    cq = jnp.sum(jnp.where(lane == head, cc_ref[0], 0.0), axis=1, keepdims=True)
    n_lt = tq // LANES
    lane_tiles = lambda a: [a[:, c * LANES:(c + 1) * LANES] for c in range(n_lt)]

    def score_tile(kj, masked):
        rows = pl.ds(pl.multiple_of(kj * tq, tq), tq)
        s = _nt(qs, kb[rows, :]) + (cq - cr_ref[0, 0, pl.ds(kj, 1), :])
        if masked:
            s = jnp.where(lax.broadcasted_iota(I32, (tq, tq), 1) <= lax.broadcasted_iota(I32, (tq, tq), 0),
                          s, -jnp.inf)
        s_buf[kj] = s
        return functools.reduce(jnp.maximum, lane_tiles(s))

    m_s[...] = score_tile(qi, True)

    def scores(kj, carry):
        m_s[...] = jnp.maximum(m_s[...], score_tile(kj, False))
        return carry

    lax.fori_loop(0, qi, scores, 0)
    m_s[...] = jnp.broadcast_to(jnp.max(m_s[...], axis=1, keepdims=True), m_s.shape)

    l_s[...] = jnp.zeros(l_s.shape, F32)

    def exps(kj, carry):
        m_rep = m_s[...]
        e_tiles = [jnp.exp(t - m_rep) for t in lane_tiles(s_buf[kj])]
        s_buf[kj] = jnp.concatenate(e_tiles, axis=1)
        l_s[...] += functools.reduce(jnp.add, e_tiles)
        return carry

    lax.fori_loop(0, qi + 1, exps, 0)
    l_s[...] = jnp.broadcast_to(1.0 / jnp.sum(l_s[...], axis=1, keepdims=True), l_s.shape)

    acc[...] = jnp.zeros(acc.shape, F32)

    def values(kj, carry):
        rows = pl.ds(pl.multiple_of(kj * tq, tq), tq)
        inv_rep = l_s[...]
        p = jnp.concatenate([(t * inv_rep).astype(BF16) for t in lane_tiles(s_buf[kj])], axis=1)
        acc[...] += jnp.dot(p, vb[rows, :], preferred_element_type=F32)
        return carry

    lax.fori_loop(0, qi + 1, values, 0)
    o_ref[...] = acc[...].astype(o_ref.dtype)


def _fox_attn_prompt(q, k, v, c_col, c_row, bsz, n, heads, tq):
    dh = q.shape[1] // heads
    nq = n // tq
    qmap = lambda b, h, i: (b * nq + i, h)
    kmap = lambda b, h, i: (b, h)
    return pl.pallas_call(
        functools.partial(_fox_attn_kernel, scale=dh ** -0.5, tq=tq),
        grid=(bsz, heads, nq),
        in_specs=[pl.BlockSpec((tq, dh), qmap), pl.BlockSpec((n, dh), kmap), pl.BlockSpec((n, dh), kmap),
                  pl.BlockSpec((1, tq, LANES), lambda b, h, i: (b, i, 0)),
                  pl.BlockSpec((1, 1, nq, tq), lambda b, h, i: (b, h, 0, 0))],
        out_specs=pl.BlockSpec((tq, dh), qmap),
        out_shape=jax.ShapeDtypeStruct((bsz * n, heads * dh), BF16),
        scratch_shapes=[pltpu.VMEM((n, dh), BF16), pltpu.VMEM((n, dh), BF16), pltpu.VMEM((nq, tq, tq), F32),
                        pltpu.VMEM((tq, LANES), F32), pltpu.VMEM((tq, LANES), F32), pltpu.VMEM((tq, dh), F32)],
        compiler_params=_params(("parallel", "parallel", "arbitrary")),
        name="fox_attn_prompt",
    )(q, k, v, c_col, c_row)


DECODE_PAGES_PER_STEP = 4


def _fox_decode_kernel(pt_ref, q_ref, kn_ref, vn_ref, f_ref, bf_ref, *rest, scale, heads, dh, page, pps):
    kp_refs, vp_refs, lf_refs = rest[:pps], rest[pps:2 * pps], rest[2 * pps:3 * pps]
    o_ref, lfo_ref, carry, m_s, l_s, acc = rest[3 * pps:]
    g = pl.program_id(1)
    sub_s = lax.broadcasted_iota(I32, (heads, LANES), 0)
    lane_s = lax.broadcasted_iota(I32, (heads, LANES), 1)

    @pl.when(g == 0)
    def _():
        lf_row = _log_sigmoid(f_ref[0] + bf_ref[...])
        lfo_ref[0] = lf_row
        lf_col = jnp.sum(jnp.where(lane_s == sub_s, lf_row, 0.0), axis=1, keepdims=True)
        carry[...] = jnp.broadcast_to(lf_col, carry.shape)
        s_new = jnp.sum(q_ref[0] * scale * kn_ref[0], axis=1, keepdims=True)
        m_s[...] = jnp.broadcast_to(s_new, m_s.shape)
        l_s[...] = jnp.ones(l_s.shape, F32)
        acc[...] = vn_ref[0]

    q3 = (q_ref[0] * scale)[None]
    r_i = lax.broadcasted_iota(I32, (page, page), 0)
    c_i = lax.broadcasted_iota(I32, (page, page), 1)
    newer = jnp.where(r_i > c_i, 1.0, 0.0).astype(BF16)
    tok3 = lax.broadcasted_iota(I32, (page, heads, dh), 0)
    lane3 = lax.broadcasted_iota(I32, (page, heads, dh), 2)
    on_diag = lane3 == tok3
    for kp_ref, vp_ref, lf_ref in zip(kp_refs, vp_refs, lf_refs):
        lf = lf_ref[0, 0]
        bias = carry[...] + _dot_exact_rhs(lf, newer)
        carry[...] = carry[...] + jnp.sum(lf, axis=1, keepdims=True)
        prod = kp_ref[0, 0] * q3 + jnp.where(on_diag, bias[None], 0.0)
        s3 = jnp.broadcast_to(jnp.sum(prod, axis=2, keepdims=True), prod.shape)
        m_prev = m_s[...]
        m_new = jnp.maximum(m_prev, jnp.max(s3, axis=0))
        alpha = jnp.exp(m_prev - m_new)
        p3 = jnp.exp(s3 - m_new[None])
        l_s[...] = alpha * l_s[...] + jnp.sum(p3, axis=0)
        acc[...] = alpha * acc[...] + jnp.sum(p3 * vp_ref[0, 0], axis=0)
        m_s[...] = m_new

    @pl.when(g == pl.num_programs(1) - 1)
    def _():
        o_ref[0] = (acc[...] / l_s[...]).astype(o_ref.dtype)


def _fox_decode(q, k_new, v_new, f_raw, b_f, k_pool, v_pool, lf_pool_t, page_table, layer, heads):
    n_seq, _, dh = q.shape
    page = k_pool.shape[2]
    n_pages = page_table.shape[1]
    pps = math.gcd(DECODE_PAGES_PER_STEP, n_pages)
    assert page == LANES and dh == LANES, "one vreg per cached token; token index doubles as a lane index"
    row = lambda b, g, pt: (b, 0, 0)

    def page_map(j, ndim):
        def index(b, g, pt):
            return (layer, pt[b * n_pages + (n_pages - 1 - (g * pps + j))]) + (0,) * (ndim - 2)
        return index

    kv_specs = [pl.BlockSpec((1, 1, page, heads, dh), page_map(j, 5)) for j in range(pps)]
    lf_specs = [pl.BlockSpec((1, 1, heads, page), page_map(j, 4)) for j in range(pps)]
    hblk = pl.BlockSpec((1, heads, dh), row)
    grid_spec = pltpu.PrefetchScalarGridSpec(
        num_scalar_prefetch=1,
        grid=(n_seq, n_pages // pps),
        in_specs=[hblk, hblk, hblk, pl.BlockSpec((1, 1, LANES), row),
                  pl.BlockSpec((1, LANES), lambda b, g, pt: (0, 0))] + kv_specs + kv_specs + lf_specs,
        out_specs=[hblk, pl.BlockSpec((1, 1, LANES), row)],
        scratch_shapes=[pltpu.VMEM((heads, LANES), F32), pltpu.VMEM((heads, dh), F32),
                        pltpu.VMEM((heads, dh), F32), pltpu.VMEM((heads, dh), F32)],
    )
    return pl.pallas_call(
        functools.partial(_fox_decode_kernel, scale=dh ** -0.5, heads=heads, dh=dh, page=page, pps=pps),
        grid_spec=grid_spec,
        out_shape=[jax.ShapeDtypeStruct((n_seq, heads, dh), BF16), jax.ShapeDtypeStruct((n_seq, 1, LANES), F32)],
        compiler_params=_params(("parallel", "arbitrary")),
        name="fox_decode",
    )(page_table.reshape(-1), q, k_new, v_new, f_raw, b_f, *([k_pool] * pps), *([v_pool] * pps),
      *([lf_pool_t] * pps))


def _fox_proj_spec(j, p):
    hd = p["cache_k_fox"].shape[3] * p["cache_k_fox"].shape[4]
    w = _pad_cols(p["w_in_fox"][j], 3 * hd + LANES).astype(BF16)
    return w, [(0, hd), (hd, hd), (2 * hd, hd), (3 * hd, LANES)]


def _fox_layer(proj, j, bsz, n, tp, p):
    q, k, v, f_raw = proj
    t_all = q.shape[0]
    heads = p["cache_k_fox"].shape[3]
    dh = p["cache_k_fox"].shape[4]
    hd = heads * dh
    b_f = _pad_cols(p["b_forget_fox"][j][None, :], LANES)
    lf_p, c_p = _fox_gate(f_raw[:tp].reshape(bsz, n, LANES), b_f, _tile(n, (512, 256, 128)))
    tq = _tile(n, (512, 256, 128))
    c_row = c_p[:, :, :heads].transpose(0, 2, 1).reshape(bsz, heads, n // tq, tq)
    o_p = _fox_attn_prompt(q, k, v, c_p, c_row, bsz, n, heads, tq)
    n_seq = t_all - tp
    s3 = lambda a: a[tp:].reshape(n_seq, heads, dh)
    lf_pool_t = p["cache_logf_fox"].transpose(0, 1, 3, 2)
    o_s, lf_s = _fox_decode(s3(q), s3(k), s3(v), f_raw[tp:].reshape(n_seq, 1, LANES), b_f,
                            p["cache_k_fox"], p["cache_v_fox"], lf_pool_t, p["page_table"], j, heads)
    o = jnp.concatenate([o_p, o_s.reshape(n_seq, hd)], axis=0)
    outs = dict(
        k_p=k[:tp].reshape(bsz, n, heads, dh), v_p=v[:tp].reshape(bsz, n, heads, dh),
        lf_p=lf_p[:, :, :heads],
        k_s=k[tp:].reshape(n_seq, 1, heads, dh), v_s=v[tp:].reshape(n_seq, 1, heads, dh),
        lf_s=lf_s[:, :, :heads])
    return o, outs


HALO = SUBLANES


def _causal_taps(hist, u, w_ref, first_tile, tl):
    width = w_ref.shape[0]
    u = _round_bf16(u)

    @pl.when(first_tile)
    def _():
        hist[0:HALO, :] = jnp.zeros((HALO, hist.shape[1]), F32)

    @pl.when(jnp.logical_not(first_tile))
    def _():
        hist[0:HALO, :] = hist[tl:tl + HALO, :]

    hist[HALO:HALO + tl, :] = u
    out = w_ref[width - 1:width, :] * u
    for k in range(width - 1):
        shift = width - 1 - k
        out = out + w_ref[k:k + 1, :] * hist[HALO - shift:HALO - shift + tl, :]
    return out


def _short_prompt_kernel(b_ref, c_ref, h_ref, w_ref, y_ref, st_ref, hist, *, tl):
    u = c_ref[...] * h_ref[...]
    conv = _causal_taps(hist, u, w_ref, pl.program_id(1) == 0, tl)
    y_ref[...] = (b_ref[...] * conv).astype(y_ref.dtype)
    keep = w_ref.shape[0] - 1
    st_ref[0] = u[tl - keep:tl, :]


def _short_prompt(bg, cg, hg, w_conv, bsz, n, t_all, tl):
    d = bg.shape[1]
    width = w_conv.shape[0]
    nl = n // tl
    row = lambda b, l: (b * nl + l, 0)
    return pl.pallas_call(
        functools.partial(_short_prompt_kernel, tl=tl),
        grid=(bsz, nl),
        in_specs=[pl.BlockSpec((tl, d), row)] * 3 + [pl.BlockSpec((width, d), lambda b, l: (0, 0))],
        out_specs=[pl.BlockSpec((tl, d), row), pl.BlockSpec((1, width - 1, d), lambda b, l: (b, 0, 0))],
        out_shape=[jax.ShapeDtypeStruct((t_all, d), BF16), jax.ShapeDtypeStruct((bsz, width - 1, d), F32)],
        scratch_shapes=[pltpu.VMEM((HALO + tl, d), F32)],
        compiler_params=_params(("parallel", "arbitrary")),
        name="short_prompt",
    )(bg, cg, hg, w_conv)


def _short_decode_kernel(b_ref, c_ref, h_ref, s_ref, w_ref, y_ref, u_ref):
    u = c_ref[...] * h_ref[...]
    width = w_ref.shape[0]
    conv = w_ref[width - 1:width, :] * _round_bf16(u)
    for k in range(width - 1):
        conv = conv + w_ref[k:k + 1, :] * _round_bf16(s_ref[k])
    y_ref[...] = (b_ref[...] * conv).astype(y_ref.dtype)
    u_ref[...] = u


def _short_decode(bg, cg, hg, state_t, w_conv):
    n_seq, d = bg.shape
    return pl.pallas_call(
        _short_decode_kernel,
        out_shape=[jax.ShapeDtypeStruct((n_seq, d), BF16), jax.ShapeDtypeStruct((n_seq, d), F32)],
        compiler_params=pltpu.CompilerParams(vmem_limit_bytes=VMEM_LIMIT),
        name="short_decode",
    )(bg, cg, hg, state_t, w_conv)


def _short_proj_spec(j, p):
    d = p["w_in_short"].shape[1]
    return p["w_in_short"][j].astype(BF16), [(0, d), (d, d), (2 * d, d)]


def _short_layer(proj, j, bsz, n, tp, p):
    bg, cg, hg = proj
    w_conv = p["w_conv_short"][j]
    y_p, st_p = _short_prompt(bg, cg, hg, w_conv, bsz, n, tp, _tile(n, (512, 256, 128)))
    state = p["state_conv_short"][j]
    y_s, u_s = _short_decode(bg[tp:], cg[tp:], hg[tp:], state.transpose(1, 0, 2), w_conv)
    y = jnp.concatenate([y_p, y_s], axis=0)
    st_s = jnp.concatenate([state[:, 1:], u_s[:, None, :]], axis=1)
    return y, dict(conv_p=st_p, conv_s=st_s)


def _gla_gate_kernel(r_ref, w_ref, b_ref, o_ref):
    z = jnp.dot(r_ref[...].astype(BF16), w_ref[...], preferred_element_type=F32) + b_ref[...]
    o_ref[...] = _log_sigmoid(z) * (1.0 / GLA_TAU)


def _gla_gate(r, w_gate, b_gate, tm):
    t = r.shape[0]
    dk = w_gate.shape[1]
    row = lambda i: (i, 0)
    full = lambda i: (0, 0)
    return pl.pallas_call(
        _gla_gate_kernel,
        grid=(t // tm,),
        in_specs=[pl.BlockSpec((tm, LANES), row), pl.BlockSpec((LANES, dk), full), pl.BlockSpec((1, dk), full)],
        out_specs=pl.BlockSpec((tm, dk), row),
        out_shape=jax.ShapeDtypeStruct((t, dk), F32),
        compiler_params=_params(("parallel",)),
        name="gla_gate",
    )(r, w_gate, b_gate)


def _rms_gate(o, g_norm, gate):
    on = o * lax.rsqrt(jnp.mean(o * o, axis=-1, keepdims=True) + RMS_EPS) * g_norm
    return on * _silu(gate)


def _gla_prompt_kernel(q_ref, k_ref, v_ref, g_ref, la_ref, gn_ref, o_ref, st_ref, state,
                       *, tl, heads, dk, dv, scale):
    l_idx = pl.program_id(1)

    @pl.when(l_idx == 0)
    def _():
        state[...] = jnp.zeros(state.shape, F32)

    cs = GLA_CHUNK
    row = lax.broadcasted_iota(I32, (cs, cs), 0)
    col = lax.broadcasted_iota(I32, (cs, cs), 1)
    causal = row >= col
    tri = jnp.where(causal, 1.0, 0.0).astype(BF16)
    for c in range(tl // cs):
        rows = slice(c * cs, (c + 1) * cs)
        for h in range(heads):
            kcols = slice(h * dk, (h + 1) * dk)
            vcols = slice(h * dv, (h + 1) * dv)
            b = _dot_exact_lhs(tri, la_ref[rows, kcols])
            b_last = b[cs - 1:cs, :]
            kh = k_ref[rows, kcols]
            q_dec = (q_ref[rows, kcols] * scale * jnp.exp(b)).astype(BF16)
            k_inv = (kh * jnp.exp(-b)).astype(BF16)
            k_end = (kh * jnp.exp(b_last - b)).astype(BF16)
            vh = v_ref[rows, vcols].astype(BF16)
            scores = jnp.where(causal, _nt(q_dec, k_inv), 0.0).astype(BF16)
            s_t = state[h]
            o = jnp.dot(scores, vh, preferred_element_type=F32) + _nt(q_dec, s_t.astype(BF16))
            state[h] = s_t * jnp.exp(b_last) + _tn(vh, k_end)
            o_ref[rows, vcols] = _rms_gate(o, gn_ref[...], g_ref[rows, vcols]).astype(o_ref.dtype)

    @pl.when(l_idx == pl.num_programs(1) - 1)
    def _():
        st_ref[0] = state[...]


def _gla_prompt(q, k, v, g, log_a, g_norm, bsz, n, heads, t_all, tl):
    dk = q.shape[1] // heads
    dv = v.shape[1] // heads
    nl = n // tl
    row = lambda b, l: (b * nl + l, 0)
    return pl.pallas_call(
        functools.partial(_gla_prompt_kernel, tl=tl, heads=heads, dk=dk, dv=dv, scale=dk ** -0.5),
        grid=(bsz, nl),
        in_specs=[pl.BlockSpec((tl, heads * dk), row), pl.BlockSpec((tl, heads * dk), row),
                  pl.BlockSpec((tl, heads * dv), row), pl.BlockSpec((tl, heads * dv), row),
                  pl.BlockSpec((tl, heads * dk), row), pl.BlockSpec((1, dv), lambda b, l: (0, 0))],
        out_specs=[pl.BlockSpec((tl, heads * dv), row),
                   pl.BlockSpec((1, heads, dv, dk), lambda b, l: (b, 0, 0, 0))],
        out_shape=[jax.ShapeDtypeStruct((t_all, heads * dv), BF16),
                   jax.ShapeDtypeStruct((bsz, heads, dv, dk), F32)],
        scratch_shapes=[pltpu.VMEM((heads, dv, dk), F32)],
        compiler_params=_params(("parallel", "arbitrary")),
        name="gla_prompt",
    )(q, k, v, g, log_a, g_norm)


def _gla_decode_kernel(s_ref, cp_ref, v_ref, g_ref, gn_ref, so_ref, o_ref, *, heads, dv, scale):
    for h in range(heads):
        cols = cp_ref[0, h]
        decay = jnp.exp(cols[:, 0:1])
        vcols = slice(h * dv, (h + 1) * dv)
        s_new = s_ref[0, h] * decay + cols[:, 1:2] * v_ref[0, :, vcols]
        so_ref[0, h] = s_new
        o = jnp.sum(_round_bf16(s_new) * _round_bf16(cols[:, 2:3] * scale), axis=0, keepdims=True)
        o_ref[0, :, vcols] = _rms_gate(o, gn_ref[...], g_ref[0, :, vcols]).astype(o_ref.dtype)


def _gla_decode(state, colpack, v, g, g_norm):
    n_seq, heads, dk, dv = state.shape
    sblk = pl.BlockSpec((1, heads, dk, dv), lambda b: (b, 0, 0, 0))
    vblk = pl.BlockSpec((1, 1, heads * dv), lambda b: (b, 0, 0))
    return pl.pallas_call(
        functools.partial(_gla_decode_kernel, heads=heads, dv=dv, scale=dk ** -0.5),
        grid=(n_seq,),
        in_specs=[sblk, pl.BlockSpec((1, heads, dk, SUBLANES), lambda b: (b, 0, 0, 0)), vblk, vblk,
                  pl.BlockSpec((1, dv), lambda b: (0, 0))],
        out_specs=[sblk, vblk],
        out_shape=[jax.ShapeDtypeStruct(state.shape, F32), jax.ShapeDtypeStruct((n_seq, 1, heads * dv), BF16)],
        compiler_params=_params(("parallel",)),
        name="gla_decode",
    )(state, colpack, v, g, g_norm)


def _gla_proj_spec(j, p):
    heads, dk, dv = p["state_gla"].shape[2:]
    hk, hv = heads * dk, heads * dv
    w = _pad_cols(p["w_in_gla"][j], 2 * hk + 2 * hv + LANES).astype(BF16)
    return w, [(0, hk), (hk, hk), (2 * hk, hv), (2 * hk + hv, hv), (2 * hk + 2 * hv, LANES)]


def _gla_layer(proj, j, bsz, n, tp, p):
    q, k, v, g, r = proj
    t_all = q.shape[0]
    n_seq, heads, dk, dv = p["state_gla"].shape[1:]
    hk, hv = heads * dk, heads * dv
    rank = p["w_gate2_gla"].shape[1]
    tm = _tile(t_all, (384, 256, 128))
    w_gate = jnp.pad(p["w_gate2_gla"][j], ((0, LANES - rank), (0, 0))).astype(BF16)
    log_a = _gla_gate(r, w_gate, p["b_gate_gla"][j][None, :], tm)
    g_norm = p["g_norm_gla"][j][None, :]
    o_p, st_p = _gla_prompt(q, k, v, g, log_a, g_norm, bsz, n, heads, tp, _tile(n, (256, 128, 64)))
    colpack = jnp.stack([log_a[tp:], k[tp:], q[tp:]], axis=-1).reshape(n_seq, heads, dk, 3)
    colpack = jnp.pad(colpack, ((0, 0), (0, 0), (0, 0), (0, SUBLANES - 3)))
    st_s, o_s = _gla_decode(p["state_gla"][j], colpack, v[tp:].reshape(n_seq, 1, hv),
                            g[tp:].reshape(n_seq, 1, hv), g_norm)
    o = jnp.concatenate([o_p, o_s.reshape(n_seq, hv)], axis=0)
    return o, dict(gla_p=st_p.transpose(0, 1, 3, 2), gla_s=st_s)


def _ssd_conv_prompt_kernel(x_ref, w_ref, b_ref, a_ref, st_ref, hist, *, tl):
    xbc = x_ref[...]
    conv = _causal_taps(hist, xbc, w_ref, pl.program_id(1) == 0, tl)
    a_ref[...] = _silu(conv + b_ref[...])
    keep = w_ref.shape[0] - 1
    st_ref[0] = xbc[tl - keep:tl, :]


def _ssd_conv_prompt(xbc, w_conv, b_conv, bsz, n, t_all, tl):
    ch = xbc.shape[1]
    width = w_conv.shape[0]
    nl = n // tl
    row = lambda b, l: (b * nl + l, 0)
    full = lambda b, l: (0, 0)
    return pl.pallas_call(
        functools.partial(_ssd_conv_prompt_kernel, tl=tl),
        grid=(bsz, nl),
        in_specs=[pl.BlockSpec((tl, ch), row), pl.BlockSpec((width, ch), full), pl.BlockSpec((1, ch), full)],
        out_specs=[pl.BlockSpec((tl, ch), row), pl.BlockSpec((1, width - 1, ch), lambda b, l: (b, 0, 0))],
        out_shape=[jax.ShapeDtypeStruct((t_all, ch), F32), jax.ShapeDtypeStruct((bsz, width - 1, ch), F32)],
        scratch_shapes=[pltpu.VMEM((HALO + tl, ch), F32)],
        compiler_params=_params(("parallel", "arbitrary")),
        name="ssd_conv_prompt",
    )(xbc, w_conv, b_conv)


def _ssd_conv_decode_kernel(x_ref, s_ref, w_ref, b_ref, a_ref):
    width = w_ref.shape[0]
    conv = w_ref[width - 1:width, :] * _round_bf16(x_ref[...])
    for k in range(width - 1):
        conv = conv + w_ref[k:k + 1, :] * _round_bf16(s_ref[k])
    a_ref[...] = _silu(conv + b_ref[...])


def _ssd_conv_decode(xbc, state_t, w_conv, b_conv):
    return pl.pallas_call(
        _ssd_conv_decode_kernel,
        out_shape=jax.ShapeDtypeStruct(xbc.shape, F32),
        compiler_params=pltpu.CompilerParams(vmem_limit_bytes=VMEM_LIMIT),
        name="ssd_conv_decode",
    )(xbc, state_t, w_conv, b_conv)


def _group_rms_gate(y, z, g_norm):
    yg = y * _silu(z)
    return yg * lax.rsqrt(jnp.mean(yg * yg, axis=-1, keepdims=True) + RMS_EPS) * g_norm


def _ssd_prompt_kernel(xs_ref, bm_ref, cm_ref, dt_ref, z_ref, dtb_ref, alog_ref, dsk_ref, gn_ref,
                       y_ref, st_ref, state, ybuf, *, q, heads, hd, ns, groups):
    l_idx = pl.program_id(1)

    @pl.when(l_idx == 0)
    def _():
        state[...] = jnp.zeros(state.shape, F32)

    pw = 2 * hd
    dt = _softplus(dt_ref[...] + dtb_ref[...])
    a = -jnp.exp(alog_ref[...])
    row = lax.broadcasted_iota(I32, (q, q), 0)
    col = lax.broadcasted_iota(I32, (q, q), 1)
    causal = row >= col
    tri = jnp.where(causal, 1.0, 0.0).astype(BF16)
    cum = _dot_exact_lhs(tri, dt * a)
    cum_t = cum.T
    e_cum = jnp.exp(cum)
    w_end = jnp.exp(cum[q - 1:q, :] - cum)
    dsk = dsk_ref[...]
    lo = lax.broadcasted_iota(I32, (q, pw), 1) < hd
    lo_rows = lax.broadcasted_iota(I32, (pw, ns), 0) < hd
    pairs_per_group = heads // groups // 2

    def pick(mat, h0):
        return jnp.where(lo, mat[:, h0:h0 + 1], mat[:, h0 + 1:h0 + 2])

    for g in range(groups):
        cmb = cm_ref[:, g * ns:(g + 1) * ns].astype(BF16)
        bmb = bm_ref[:, g * ns:(g + 1) * ns].astype(BF16)
        cb = _nt(cmb, bmb)
        for pp in range(pairs_per_group):
            pr = g * pairs_per_group + pp
            h0 = 2 * pr
            cols = slice(pr * pw, (pr + 1) * pw)
            x_pair = xs_ref[:, cols]
            xdt = x_pair * pick(dt, h0)

            def decay_mix(h):
                return (cb * jnp.exp(jnp.where(causal, cum[:, h:h + 1] - cum_t[h:h + 1, :], -jnp.inf))).astype(BF16)

            y = (jnp.dot(decay_mix(h0), jnp.where(lo, xdt, 0.0).astype(BF16), preferred_element_type=F32)
                 + jnp.dot(decay_mix(h0 + 1), jnp.where(lo, 0.0, xdt).astype(BF16), preferred_element_type=F32))
            s_pair = state[pr]
            y = y + _nt(cmb, s_pair.astype(BF16)) * pick(e_cum, h0)
            dec = jnp.where(lo_rows, jnp.exp(cum_t[h0:h0 + 1, q - 1:q]), jnp.exp(cum_t[h0 + 1:h0 + 2, q - 1:q]))
            state[pr] = s_pair * dec + _tn((xdt * pick(w_end, h0)).astype(BF16), bmb)
            ybuf[:, cols] = y + x_pair * pick(dsk, h0)

    gw = heads * hd // groups
    for g in range(groups):
        cols = slice(g * gw, (g + 1) * gw)
        y_ref[:, cols] = _group_rms_gate(ybuf[:, cols], z_ref[:, cols], gn_ref[:, cols]).astype(y_ref.dtype)

    @pl.when(l_idx == pl.num_programs(1) - 1)
    def _():
        st_ref[0] = state[...]


def _ssd_prompt(act, dt_raw, z, dt_bias, a_log, d_skip, g_norm, bsz, n, heads, hd, ns, groups, t_all, q):
    inner = heads * hd
    gn = groups * ns
    nl = n // q
    row = lambda b, l: (b * nl + l, 0)
    full = lambda b, l: (0, 0)
    n_pairs = heads // 2
    return pl.pallas_call(
        functools.partial(_ssd_prompt_kernel, q=q, heads=heads, hd=hd, ns=ns, groups=groups),
        grid=(bsz, nl),
        in_specs=[pl.BlockSpec((q, inner), row),
                  pl.BlockSpec((q, gn), lambda b, l: (b * nl + l, inner // gn)),
                  pl.BlockSpec((q, gn), lambda b, l: (b * nl + l, inner // gn + 1)),
                  pl.BlockSpec((q, LANES), row), pl.BlockSpec((q, inner), row),
                  pl.BlockSpec((1, LANES), full), pl.BlockSpec((1, LANES), full), pl.BlockSpec((1, LANES), full),
                  pl.BlockSpec((1, inner), full)],
        out_specs=[pl.BlockSpec((q, inner), row),
                   pl.BlockSpec((1, n_pairs, 2 * hd, ns), lambda b, l: (b, 0, 0, 0))],
        out_shape=[jax.ShapeDtypeStruct((t_all, inner), BF16),
                   jax.ShapeDtypeStruct((bsz, n_pairs, 2 * hd, ns), F32)],
        scratch_shapes=[pltpu.VMEM((n_pairs, 2 * hd, ns), F32), pltpu.VMEM((q, inner), F32)],
        compiler_params=_params(("parallel", "arbitrary")),
        name="ssd_prompt",
    )(act, act, act, dt_raw, z, dt_bias, a_log, d_skip, g_norm)


SSD_PACK_ROWS = 16


def _ssd_decode_kernel(s_ref, pack_ref, b_ref, c_ref, z_ref, gn_ref, so_ref, y_ref, *, n_pairs, ns, groups):
    pt = pack_ref[0].T
    pw = s_ref.shape[2]
    lane = lax.broadcasted_iota(I32, (pw, LANES), 1)
    y_cols = jnp.zeros((pw, LANES), F32)
    pairs_per_group = n_pairs // groups
    for pr in range(n_pairs):
        g = pr // pairs_per_group
        col = lambda k: pt[:, k * SSD_PACK_ROWS + pr:k * SSD_PACK_ROWS + pr + 1]
        x_col = col(0)
        dt = _softplus(col(1) + col(2))
        dec = jnp.exp(dt * -jnp.exp(col(3)))
        b_row = b_ref[0, :, g * ns:(g + 1) * ns]
        c_row = c_ref[0, :, g * ns:(g + 1) * ns]
        s_new = s_ref[0, pr] * dec + (x_col * dt) * b_row
        so_ref[0, pr] = s_new
        y_col = jnp.sum(_round_bf16(s_new) * _round_bf16(c_row), axis=1, keepdims=True) + x_col * col(4)
        y_cols = jnp.where(lane == pr, y_col, y_cols)
    y_rows = y_cols.T[0:n_pairs, :]
    yg = y_rows * _silu(z_ref[0])
    sq = jnp.sum(yg * yg, axis=1, keepdims=True)
    out = jnp.zeros(yg.shape, F32)
    sub = lax.broadcasted_iota(I32, yg.shape, 0)
    sub1 = lax.broadcasted_iota(I32, sq.shape, 0)
    for g in range(groups):
        lo_r, hi_r = g * pairs_per_group, (g + 1) * pairs_per_group
        in_g1 = jnp.logical_and(sub1 >= lo_r, sub1 < hi_r)
        ms = jnp.sum(jnp.where(in_g1, sq, 0.0), axis=0, keepdims=True) / (pairs_per_group * pw)
        out = jnp.where(jnp.logical_and(sub >= lo_r, sub < hi_r), yg * lax.rsqrt(ms + RMS_EPS), out)
    y_ref[0] = (out * gn_ref[...]).astype(y_ref.dtype)


def _ssd_decode(state, pack, b_rows, c_rows, z, g_norm, groups):
    n_seq, n_pairs, pw, ns = state.shape
    sblk = pl.BlockSpec((1, n_pairs, pw, ns), lambda b: (b, 0, 0, 0))
    rblk = pl.BlockSpec((1, 1, groups * ns), lambda b: (b, 0, 0))
    zblk = pl.BlockSpec((1, n_pairs, pw), lambda b: (b, 0, 0))
    return pl.pallas_call(
        functools.partial(_ssd_decode_kernel, n_pairs=n_pairs, ns=ns, groups=groups),
        grid=(n_seq,),
        in_specs=[sblk, pl.BlockSpec((1, LANES, LANES), lambda b: (b, 0, 0)), rblk, rblk, zblk,
                  pl.BlockSpec((n_pairs, pw), lambda b: (0, 0))],
        out_specs=[sblk, zblk],
        out_shape=[jax.ShapeDtypeStruct(state.shape, F32), jax.ShapeDtypeStruct((n_seq, n_pairs, pw), BF16)],
        compiler_params=_params(("parallel",)),
        name="ssd_decode",
    )(state, pack, b_rows, c_rows, z, g_norm)


def _ssd_proj_spec(j, p):
    heads, hd, ns = p["state_ssm"].shape[2:]
    inner = heads * hd
    ch = inner + 2 * SSD_GROUPS * ns
    w = _pad_cols(p["w_in_ssd"][j], inner + ch + LANES).astype(BF16)
    return w, [(0, inner), (inner, ch), (inner + ch, LANES)]


def _ssd_layer(proj, j, bsz, n, tp, p):
    z, xbc, dt_raw = proj
    n_seq, heads, hd, ns = p["state_ssm"].shape[1:]
    groups = SSD_GROUPS
    inner = heads * hd
    gn = groups * ns
    ch = inner + 2 * gn
    n_pairs = heads // 2
    pw = 2 * hd
    assert pw == LANES and ns == LANES and n_pairs == SSD_PACK_ROWS and inner % gn == 0
    w_conv = p["w_conv_ssd"][j]
    b_conv = p["b_conv_ssd"][j][None, :]
    lane_row = lambda v: _pad_cols(v[None, :], LANES)
    g_norm = p["g_norm_ssd"][j]
    act, conv_p = _ssd_conv_prompt(xbc, w_conv, b_conv, bsz, n, tp, _tile(n, (256, 128)))
    y_p, ssm_p = _ssd_prompt(act, dt_raw, z, lane_row(p["dt_bias_ssd"][j]), lane_row(p["a_log_ssd"][j]),
                           lane_row(p["d_skip_ssd"][j]), g_norm[None, :], bsz, n, heads, hd, ns, groups,
                           tp, _tile(n, (128,)))
    conv_state = p["state_conv_ssd"][j]
    xbc_s = xbc[tp:]
    act_s = _ssd_conv_decode(xbc_s, conv_state.transpose(1, 0, 2), w_conv, b_conv)
    per_row = lambda v: jnp.broadcast_to(jnp.repeat(v, hd, axis=-1).reshape(-1, n_pairs, pw), (n_seq, n_pairs, pw))
    pack = jnp.concatenate([
        act_s[:, :inner].reshape(n_seq, n_pairs, pw),
        per_row(dt_raw[tp:, :heads]), per_row(p["dt_bias_ssd"][j][None, :]),
        per_row(p["a_log_ssd"][j][None, :]), per_row(p["d_skip_ssd"][j][None, :])], axis=1)
    pack = jnp.pad(pack, ((0, 0), (0, LANES - pack.shape[1]), (0, 0)))
    ssm_s, y_s = _ssd_decode(p["state_ssm"][j].reshape(n_seq, n_pairs, pw, ns), pack,
                             act_s[:, inner:inner + gn].reshape(n_seq, 1, gn),
                             act_s[:, inner + gn:].reshape(n_seq, 1, gn),
                             z[tp:].reshape(n_seq, n_pairs, pw), g_norm.reshape(n_pairs, pw), groups)
    y = jnp.concatenate([y_p, y_s.reshape(n_seq, inner)], axis=0)
    conv_s = jnp.concatenate([conv_state[:, 1:], xbc_s[:, None, :]], axis=1)
    return y, dict(ssm_p=ssm_p.reshape(bsz, heads, hd, ns), ssm_s=ssm_s.reshape(n_seq, heads, hd, ns),
                   conv_p=conv_p, conv_s=conv_s)


_LAYERS = (_fox_layer, _short_layer, _gla_layer, _ssd_layer)
_PROJ_SPECS = (_fox_proj_spec, _short_proj_spec, _gla_proj_spec, _ssd_proj_spec)
_OUT_ORDER = ("k", "v", "lf", "conv_short", "gla", "ssm", "conv_ssd")


def kernel(x_prompt, x_sample, cache_k_fox, cache_v_fox, cache_logf_fox, page_table, state_conv_short, state_gla, state_ssm, state_conv_ssd, w_in_fox, b_forget_fox, w_out_fox, w_in_short, w_conv_short, w_out_short, w_in_gla, w_gate2_gla, b_gate_gla, g_norm_gla, w_out_gla, w_in_ssd, w_conv_ssd, b_conv_ssd, dt_bias_ssd, a_log_ssd, d_skip_ssd, g_norm_ssd, w_out_ssd, ln_mix_g, ln_mix_b, ln_ffn_g, ln_ffn_b, w_router, b_router, w_gate_up, b_gate_up, w_down, b_down):
    p = dict(locals())
    bsz, n, d = x_prompt.shape
    n_seq, n_dec, _ = x_sample.shape
    assert n_dec == 1, "the sample group decodes one token per sequence"
    depth = ln_mix_g.shape[0]
    alpha = (2 * depth) ** 0.25
    tp = bsz * n
    x = jnp.concatenate([x_prompt.reshape(tp, d), x_sample.reshape(n_seq, d)], axis=0)
    w_outs = (w_out_fox, w_out_short, w_out_gla, w_out_ssd)
    acc = {name + sfx: [] for name in _OUT_ORDER for sfx in ("_p", "_s")}
    layer_of = lambda i: (i % len(_LAYERS), i // len(_LAYERS))
    w0, segs0 = _PROJ_SPECS[0](0, p)
    proj = _proj(x, w0, segs0, _tile(tp + n_seq, (384, 256, 128)))
    for i in range(depth):
        kind, j = layer_of(i)
        y_mix, outs = _LAYERS[kind](proj, j, bsz, n, tp, p)
        rename = {"conv_p": ("conv_short_p" if kind == 1 else "conv_ssd_p"),
                  "conv_s": ("conv_short_s" if kind == 1 else "conv_ssd_s")}
        for key, val in outs.items():
            acc[rename.get(key, key)].append(val)
        if i + 1 < depth:
            nkind, nj = layer_of(i + 1)
            x, *proj = _post_blocks(x, y_mix, w_outs[kind][j], i, alpha, p, _PROJ_SPECS[nkind](nj, p))
        else:
            x = _post_blocks(x, y_mix, w_outs[kind][j], i, alpha, p)
    stack = lambda name: jnp.stack(acc[name])
    return (x[:tp].reshape(bsz, n, d), x[tp:].reshape(n_seq, n_dec, d),
            *[stack(name + "_p") for name in _OUT_ORDER],
            *[stack(name + "_s") for name in _OUT_ORDER])
```

```python
import functools
import math

import jax
import jax.numpy as jnp
from jax import lax
from jax.experimental import pallas as pl
from jax.experimental.pallas import tpu as pltpu

F32 = jnp.float32
BF16 = jnp.bfloat16
I32 = jnp.int32

LANES = 128
SUBLANES = 8
VMEM_LIMIT = 56 * 1024 * 1024

LN_EPS = 1e-5
RMS_EPS = 1e-6
TOP_K = 4
SWIGLU_LIMIT = 7.0
SWIGLU_ALPHA = 1.702
GLA_TAU = 16.0
GLA_CHUNK = 64
SSD_GROUPS = 4
MOE_BLOCK = 256
FFN_UP_PHASES = 4
FFN_DOWN_PHASES = 4
GATHER_AHEAD = 2


def _params(semantics):
    return pltpu.CompilerParams(dimension_semantics=semantics, vmem_limit_bytes=VMEM_LIMIT)


def _tile(n, candidates):
    for c in candidates:
        if n % c == 0:
            return c
    raise ValueError(f"no tile for {n} among {candidates}")


def _round_up(n, m):
    return (n + m - 1) // m * m


def _log_sigmoid(z):
    return jnp.minimum(z, 0.0) - jnp.log(1.0 + jnp.exp(-jnp.abs(z)))


def _softplus(z):
    return jnp.maximum(z, 0.0) + jnp.log(1.0 + jnp.exp(-jnp.abs(z)))


def _silu(z):
    return z * jax.nn.sigmoid(z)


def _split3(x):
    hi = x.astype(BF16)
    r = x - hi.astype(F32)
    mid = r.astype(BF16)
    lo = (r - mid.astype(F32)).astype(BF16)
    return hi, mid, lo


def _dot_exact_lhs(a01, x):
    return sum(jnp.dot(a01, p, preferred_element_type=F32) for p in _split3(x))


def _dot_exact_rhs(x, a01):
    return sum(jnp.dot(p, a01, preferred_element_type=F32) for p in _split3(x))


def _round_bf16(x):
    return x.astype(BF16).astype(F32)


def _nt(a, b):
    return lax.dot_general(a, b, (((1,), (1,)), ((), ())), preferred_element_type=F32)


def _tn(a, b):
    return lax.dot_general(a, b, (((0,), (0,)), ((), ())), preferred_element_type=F32)


def _proj_kernel(x_ref, w_ref, *out_refs, segs, chunk):
    xb = x_ref[...].astype(BF16)
    for o_ref, (start, width) in zip(out_refs, segs):
        for c in range(0, width, chunk):
            cw = min(chunk, width - c)
            o_ref[:, c:c + cw] = jnp.dot(
                xb, w_ref[:, start + c:start + c + cw], preferred_element_type=F32).astype(o_ref.dtype)


def _proj(x, w, segs, tm):
    t, d = x.shape
    n = w.shape[1]
    assert all(s % LANES == 0 and wd % LANES == 0 for s, wd in segs)
    return pl.pallas_call(
        functools.partial(_proj_kernel, segs=tuple(segs), chunk=512),
        grid=(t // tm,),
        in_specs=[pl.BlockSpec((tm, d), lambda i: (i, 0)),
                  pl.BlockSpec((d, n), lambda i: (0, 0))],
        out_specs=[pl.BlockSpec((tm, wd), lambda i: (i, 0)) for _, wd in segs],
        out_shape=[jax.ShapeDtypeStruct((t, wd), F32) for _, wd in segs],
        compiler_params=_params(("parallel",)),
        name="proj",
    )(x, w)


def _pad_cols(w, n):
    return jnp.pad(w, ((0, 0), (0, n - w.shape[1])))


def _layer_norm_rows(z, g, b):
    mu = jnp.mean(z, axis=-1, keepdims=True)
    zc = z - mu
    var = jnp.mean(zc * zc, axis=-1, keepdims=True)
    return zc * lax.rsqrt(var + LN_EPS) * g + b


def _mix_ln_router_kernel(y_ref, w_ref, x_ref, g_ref, b_ref, wr_ref, br_ref,
                          x1_ref, topi_ref, gate_ref, rank_ref, cnt_ref, *, alpha, n_experts):
    tm = x_ref.shape[0]
    mix = jnp.dot(y_ref[...].astype(BF16), w_ref[...], preferred_element_type=F32)
    x1 = _layer_norm_rows(alpha * x_ref[...] + mix, g_ref[...], b_ref[...])
    x1_ref[...] = x1
    logits = jnp.dot(x1.astype(BF16), wr_ref[...], preferred_element_type=F32) + br_ref[...]
    lane = lax.broadcasted_iota(I32, logits.shape, 1)
    neg_inf = jnp.float32(-jnp.inf)
    cur = jnp.where(lane < n_experts, logits, neg_inf)
    topi = jnp.zeros(logits.shape, I32)
    chosen = jnp.zeros(logits.shape, F32)
    vals, ids = [], []
    for k in range(TOP_K):
        m = jnp.max(cur, axis=1, keepdims=True)
        idx = jnp.min(jnp.where(cur == m, lane, LANES), axis=1, keepdims=True)
        vals.append(m)
        ids.append(idx)
        topi = jnp.where(lane == k, idx, topi)
        chosen = jnp.where(lane == idx, 1.0, chosen)
        cur = jnp.where(lane == idx, neg_inf, cur)
    es = [jnp.exp(v - vals[0]) for v in vals]
    inv = 1.0 / sum(es)
    gate = jnp.zeros(logits.shape, F32)
    for k in range(TOP_K):
        gate = jnp.where(lane == k, es[k] * inv, gate)
    topi_ref[...] = topi
    gate_ref[...] = gate
    earlier = jnp.where(lax.broadcasted_iota(I32, (tm, tm), 0) > lax.broadcasted_iota(I32, (tm, tm), 1),
                        1.0, 0.0).astype(BF16)
    before = jnp.dot(earlier, chosen.astype(BF16), preferred_element_type=F32)
    rank = jnp.zeros(logits.shape, I32)
    for k in range(TOP_K):
        r_k = jnp.sum(jnp.where(lane == ids[k], before, 0.0), axis=1, keepdims=True)
        rank = jnp.where(lane == k, r_k.astype(I32), rank)
    rank_ref[...] = rank
    cnt_ref[0] = jnp.broadcast_to(jnp.sum(chosen, axis=0, keepdims=True), (SUBLANES, LANES)).astype(I32)


def _mix_ln_router(y, w_out, x, ln_g, ln_b, w_r, b_r, alpha, n_experts, tm):
    t, d = x.shape
    kdim = y.shape[1]
    row = lambda i: (i, 0)
    full = lambda i: (0, 0)
    return pl.pallas_call(
        functools.partial(_mix_ln_router_kernel, alpha=alpha, n_experts=n_experts),
        grid=(t // tm,),
        in_specs=[pl.BlockSpec((tm, kdim), row), pl.BlockSpec((kdim, d), full),
                  pl.BlockSpec((tm, d), row), pl.BlockSpec((1, d), full), pl.BlockSpec((1, d), full),
                  pl.BlockSpec((d, LANES), full), pl.BlockSpec((1, LANES), full)],
        out_specs=[pl.BlockSpec((tm, d), row), pl.BlockSpec((tm, LANES), row),
                   pl.BlockSpec((tm, LANES), row), pl.BlockSpec((tm, LANES), row),
                   pl.BlockSpec((1, SUBLANES, LANES), lambda i: (i, 0, 0))],
        out_shape=[jax.ShapeDtypeStruct((t, d), F32), jax.ShapeDtypeStruct((t, LANES), I32),
                   jax.ShapeDtypeStruct((t, LANES), F32), jax.ShapeDtypeStruct((t, LANES), I32),
                   jax.ShapeDtypeStruct((t // tm, SUBLANES, LANES), I32)],
        compiler_params=_params(("parallel",)),
        name="mix_ln_router",
    )(y, w_out, x, ln_g, ln_b, w_r, b_r)


def _route_tables(topi, rank, tile_cnt, n_experts, bm):
    t = topi.shape[0]
    n_pairs = t * TOP_K
    tm = t // tile_cnt.shape[0]
    e_ids = jnp.arange(n_experts, dtype=I32)
    cnt = tile_cnt[:, 0, :n_experts]
    tile_off = jnp.cumsum(cnt, axis=0) - cnt
    counts = jnp.sum(cnt, axis=0)
    starts = jnp.cumsum(counts) - counts
    padded = (counts + bm - 1) // bm * bm
    pad_ends = jnp.cumsum(padded)
    pad_starts = pad_ends - padded
    expert = topi[:, :TOP_K]
    base = jnp.repeat(pad_starts[None, :] + tile_off, tm, axis=0)
    hit = expert[:, :, None] == e_ids[None, None, :]
    pos = (jnp.sum(jnp.where(hit, base[:, None, :], 0), axis=2) + rank[:, :TOP_K]).reshape(-1).astype(I32)
    n_blocks = (n_pairs + bm - 1) // bm + n_experts + GATHER_AHEAD
    block_start = jnp.arange(n_blocks, dtype=I32) * bm
    block_expert = jnp.minimum(jnp.sum((pad_ends[None, :] <= block_start[:, None]).astype(I32), axis=1),
                               n_experts - 1).astype(I32)
    n_used = (pad_ends[-1] // bm).astype(I32).reshape(1)
    pair_bits = max(1, (n_pairs - 1).bit_length())
    assert (n_experts << pair_bits) < 2 ** 31
    keyed = jnp.sort(expert.reshape(-1) * (1 << pair_bits) + jnp.arange(n_pairs, dtype=I32))
    order = keyed & ((1 << pair_bits) - 1)
    e_row = jnp.repeat(block_expert, bm)
    off = jnp.arange(n_blocks * bm, dtype=I32) - pad_starts[e_row]
    src = jnp.clip(starts[e_row] + off, 0, n_pairs - 1)
    row_tok = jnp.where(off < counts[e_row], order[src] // TOP_K, 0).astype(I32)
    return block_expert, row_tok, n_used, pos, n_blocks


def _moe_ffn_kernel(be_ref, rt_ref, nu_ref, x_hbm, wgu_ref, bgu_ref, wd_ref, bd_ref,
                    y_ref, xbuf, sem, wgu_b, wd_b, xb, act, *, bm, d_ff):
    i = pl.program_id(0)
    n_used = nu_ref[0]
    n_slots = GATHER_AHEAD + 1
    slot = i % n_slots
    ahead_slot = (i + GATHER_AHEAD) % n_slots

    def row_copy(blk, r, dst_slot):
        tok = rt_ref[blk * bm + r]
        return pltpu.make_async_copy(x_hbm.at[pl.ds(tok, 1), :],
                                     xbuf.at[dst_slot, pl.ds(r, 1), :], sem.at[dst_slot])

    def wait_rows(dst_slot):
        pltpu.make_async_copy(x_hbm.at[pl.ds(0, bm), :], xbuf.at[dst_slot], sem.at[dst_slot]).wait()

    @pl.when(i == 0)
    def _():
        for blk in range(GATHER_AHEAD):
            def body(r, carry):
                row_copy(blk, r, blk).start()
                return carry
            lax.fori_loop(0, bm, body, 0)

    expert_changed = jnp.logical_or(i == 0, be_ref[jnp.maximum(i - 1, 0)] != be_ref[i])

    @pl.when(jnp.logical_and(i < n_used, expert_changed))
    def _():
        wgu_b[...] = wgu_ref[0, 0].astype(BF16)
        wd_b[...] = wd_ref[0, 0].astype(BF16)

    @pl.when(jnp.logical_and(i >= n_used, i < n_used + GATHER_AHEAD))
    def _():
        wait_rows(slot)

    @pl.when(i < n_used)
    def _():
        wait_rows(slot)
        xb[...] = xbuf[slot].astype(BF16)
        d = xb.shape[1]
        phases = [("up", c) for c in range(FFN_UP_PHASES)] + [("down", c) for c in range(FFN_DOWN_PHASES)]
        per_group = bm // (len(phases) - 1)
        uw, dw = d_ff // FFN_UP_PHASES, d // FFN_DOWN_PHASES
        zero = None
        for n, (kind, c) in enumerate(phases):
            if kind == "up":
                hg = (jnp.dot(xb[...], wgu_b[:, c * uw:(c + 1) * uw], preferred_element_type=F32)
                      + bgu_ref[0, 0, :, c * uw:(c + 1) * uw])
                hl = (jnp.dot(xb[...], wgu_b[:, d_ff + c * uw:d_ff + (c + 1) * uw], preferred_element_type=F32)
                      + bgu_ref[0, 0, :, d_ff + c * uw:d_ff + (c + 1) * uw])
                if zero is not None:
                    hg = hg + zero
                glu = jnp.minimum(hg, SWIGLU_LIMIT)
                lin = jnp.clip(hl, -SWIGLU_LIMIT, SWIGLU_LIMIT)
                act[:, c * uw:(c + 1) * uw] = (glu * jax.nn.sigmoid(SWIGLU_ALPHA * glu) * (lin + 1.0)).astype(BF16)
            else:
                y = (jnp.dot(act[...], wd_b[:, c * dw:(c + 1) * dw], preferred_element_type=F32)
                     + bd_ref[0, 0, :, c * dw:(c + 1) * dw])
                y_ref[:, c * dw:(c + 1) * dw] = y + zero
            if n < len(phases) - 1:
                for r in range(n * per_group, bm if n == len(phases) - 2 else (n + 1) * per_group):
                    row_copy(i + GATHER_AHEAD, r, ahead_slot).start()
                zero = jnp.minimum(jnp.abs(xbuf[slot, 0:1, 0:LANES]), 0.0)[:, 0:1]

    @pl.when(i >= n_used)
    def _():
        y_ref[...] = jnp.zeros(y_ref.shape, y_ref.dtype)


def _moe_ffn(x1, block_expert, row_tok, n_used, n_blocks, layer, w_gate_up, b_gate_up, w_down, b_down, bm):
    t, d = x1.shape
    depth, n_e, _, two_f = w_gate_up.shape
    d_ff = two_f // 2
    wmap = lambda i, be, rt, nu: (layer, be[i], 0, 0)
    grid_spec = pltpu.PrefetchScalarGridSpec(
        num_scalar_prefetch=3,
        grid=(n_blocks,),
        in_specs=[pl.BlockSpec(memory_space=pl.ANY),
                  pl.BlockSpec((1, 1, d, two_f), wmap),
                  pl.BlockSpec((1, 1, 1, two_f), wmap),
                  pl.BlockSpec((1, 1, d_ff, d), wmap),
                  pl.BlockSpec((1, 1, 1, d), wmap)],
        out_specs=pl.BlockSpec((bm, d), lambda i, be, rt, nu: (i, 0)),
        scratch_shapes=[pltpu.VMEM((GATHER_AHEAD + 1, bm, d), F32), pltpu.SemaphoreType.DMA((GATHER_AHEAD + 1,)),
                        pltpu.VMEM((d, two_f), BF16), pltpu.VMEM((d_ff, d), BF16), pltpu.VMEM((bm, d), BF16),
                        pltpu.VMEM((bm, d_ff), BF16)],
    )
    return pl.pallas_call(
        functools.partial(_moe_ffn_kernel, bm=bm, d_ff=d_ff),
        grid_spec=grid_spec,
        out_shape=jax.ShapeDtypeStruct((n_blocks * bm, d), F32),
        compiler_params=_params(("arbitrary",)),
        name="moe_ffn",
    )(block_expert, row_tok, n_used, x1, w_gate_up, b_gate_up.reshape(depth, n_e, 1, two_f),
      w_down, b_down.reshape(depth, n_e, 1, d))


def _moe_combine_kernel(pos_ref, y_hbm, gate_ref, x_ref, g_ref, b_ref, *rest, tm, alpha, segs, chunk):
    if segs:
        w_ref, o_ref, *proj_refs = rest[:-2]
    else:
        (o_ref,), proj_refs = rest[:-2], []
    buf, sem = rest[-2:]
    i = pl.program_id(0)
    nb = pl.num_programs(0)
    n_slots = GATHER_AHEAD + 1
    slot = i % n_slots
    ahead_slot = (i + GATHER_AHEAD) % n_slots
    n_rows = TOP_K * tm

    def row_copy(tile, j, dst_slot):
        r, k = j // TOP_K, j % TOP_K
        p = pos_ref[(tile * tm + r) * TOP_K + k]
        return pltpu.make_async_copy(y_hbm.at[pl.ds(p, 1), :],
                                     buf.at[dst_slot, pl.ds(k * tm + r, 1), :], sem.at[dst_slot])

    def wait_rows(dst_slot):
        pltpu.make_async_copy(y_hbm.at[pl.ds(0, n_rows), :], buf.at[dst_slot], sem.at[dst_slot]).wait()

    @pl.when(i == 0)
    def _():
        for tile in range(GATHER_AHEAD):
            def body(r, carry):
                for k in range(TOP_K):
                    row_copy(jnp.minimum(tile, nb - 1), r * TOP_K + k, tile).start()
                return carry
            lax.fori_loop(0, tm, body, 0)

    wait_rows(slot)
    gate = gate_ref[...]
    ffn = gate[:, 0:1] * buf[slot, pl.ds(0, tm), :]
    for k in range(1, TOP_K):
        ffn = ffn + gate[:, k:k + 1] * buf[slot, pl.ds(k * tm, tm), :]
    x2 = _layer_norm_rows(alpha * x_ref[...] + ffn, g_ref[...], b_ref[...])
    o_ref[...] = x2

    chunks = [(o, s, c, min(chunk, wd - c)) for o, (s, wd) in zip(proj_refs, segs) for c in range(0, wd, chunk)]
    n_groups = max(1, len(chunks))
    per_group = -(-n_rows // n_groups)
    next_tile = jnp.minimum(i + GATHER_AHEAD, nb - 1)
    xb = x2.astype(BF16)
    zero = None
    for n in range(n_groups):
        if chunks:
            o, s, c, cw = chunks[n]
            res = jnp.dot(xb, w_ref[:, s + c:s + c + cw], preferred_element_type=F32)
            o[:, c:c + cw] = res if zero is None else res + zero
        for j in range(n * per_group, min(n_rows, (n + 1) * per_group)):
            row_copy(next_tile, j, ahead_slot).start()
        zero = jnp.minimum(jnp.abs(buf[slot, 0:1, 0:LANES]), 0.0)[:, 0:1]

    @pl.when(i == nb - 1)
    def _():
        for extra in range(1, GATHER_AHEAD + 1):
            wait_rows((i + extra) % n_slots)


def _moe_combine(y_rows, pos, gates, x1, ln_g, ln_b, alpha, tm, w_next=None, segs=None):
    t, d = x1.shape
    segs = tuple(segs or ())
    row = lambda i, p: (i, 0)
    full = lambda i, p: (0, 0)
    in_specs = [pl.BlockSpec(memory_space=pl.ANY), pl.BlockSpec((tm, LANES), row),
                pl.BlockSpec((tm, d), row), pl.BlockSpec((1, d), full), pl.BlockSpec((1, d), full)]
    args = [pos, y_rows, gates, x1, ln_g, ln_b]
    if segs:
        assert all(s % LANES == 0 and wd % LANES == 0 for s, wd in segs)
        in_specs.append(pl.BlockSpec(w_next.shape, full))
        args.append(w_next)
    grid_spec = pltpu.PrefetchScalarGridSpec(
        num_scalar_prefetch=1,
        grid=(t // tm,),
        in_specs=in_specs,
        out_specs=[pl.BlockSpec((tm, d), row)] + [pl.BlockSpec((tm, wd), row) for _, wd in segs],
        scratch_shapes=[pltpu.VMEM((GATHER_AHEAD + 1, TOP_K * tm, d), F32),
                        pltpu.SemaphoreType.DMA((GATHER_AHEAD + 1,))],
    )
    outs = pl.pallas_call(
        functools.partial(_moe_combine_kernel, tm=tm, alpha=alpha, segs=segs, chunk=512),
        grid_spec=grid_spec,
        out_shape=[jax.ShapeDtypeStruct((t, d), F32)] + [jax.ShapeDtypeStruct((t, wd), F32) for _, wd in segs],
        compiler_params=_params(("arbitrary",)),
        name="moe_combine",
    )(*args)
    return tuple(outs) if segs else outs[0]


def _post_blocks(x, y_mix, w_out, i, alpha, p, next_proj=None):
    n_experts = p["w_router"].shape[-1]
    d = x.shape[1]
    tm = _tile(x.shape[0], (384, 256, 128))
    w_r = _pad_cols(p["w_router"][i], LANES).astype(BF16)
    b_r = _pad_cols(p["b_router"][i][None, :].astype(F32), LANES)
    x1, topi, gates, rank, tile_cnt = _mix_ln_router(
        y_mix, w_out.astype(BF16), x, p["ln_mix_g"][i].reshape(1, d), p["ln_mix_b"][i].reshape(1, d),
        w_r, b_r, alpha, n_experts, tm)
    block_expert, row_tok, n_used, pos, n_blocks = _route_tables(topi, rank, tile_cnt, n_experts, MOE_BLOCK)
    y_rows = _moe_ffn(x1, block_expert, row_tok, n_used, n_blocks, i, p["w_gate_up"], p["b_gate_up"],
                      p["w_down"], p["b_down"], MOE_BLOCK)
    w_next, segs = next_proj if next_proj is not None else (None, None)
    return _moe_combine(y_rows, pos, gates, x1, p["ln_ffn_g"][i].reshape(1, d),
                        p["ln_ffn_b"][i].reshape(1, d), alpha, _tile(x.shape[0], (128,)), w_next, segs)


def _fox_gate_kernel(f_ref, bf_ref, lf_ref, c_ref, carry, *, tl):
    @pl.when(pl.program_id(1) == 0)
    def _():
        carry[...] = jnp.zeros(carry.shape, F32)

    lf = _log_sigmoid(f_ref[0] + bf_ref[...])
    lf_ref[0] = lf
    row = lax.broadcasted_iota(I32, (tl, tl), 0)
    col = lax.broadcasted_iota(I32, (tl, tl), 1)
    tri = jnp.where(row >= col, 1.0, 0.0).astype(BF16)
    c = _dot_exact_lhs(tri, lf) + carry[0:1, :]
    c_ref[0] = c
    carry[0:1, :] = c[tl - 1:tl, :]


def _fox_gate(f_raw, b_f, tl):
    bsz, n, _ = f_raw.shape
    blk = pl.BlockSpec((1, tl, LANES), lambda b, l: (b, l, 0))
    return pl.pallas_call(
        functools.partial(_fox_gate_kernel, tl=tl),
        grid=(bsz, n // tl),
        in_specs=[blk, pl.BlockSpec((1, LANES), lambda b, l: (0, 0))],
        out_specs=[blk, blk],
        out_shape=[jax.ShapeDtypeStruct(f_raw.shape, F32)] * 2,
        scratch_shapes=[pltpu.VMEM((SUBLANES, LANES), F32)],
        compiler_params=_params(("parallel", "arbitrary")),
        name="fox_gate",
    )(f_raw, b_f)


def _fox_attn_kernel(q_ref, k_ref, v_ref, cc_ref, cr_ref, o_ref, kb, vb, s_buf, m_s, l_s, acc, *, scale, tq):
    head = pl.program_id(1)
    qi = pl.program_id(2)

    @pl.when(qi == 0)
    def _():
        kb[...] = k_ref[...].astype(BF16)
        vb[...] = v_ref[...].astype(BF16)

    qs = (q_ref[...] * scale).astype(BF16)
    lane = lax.broadcasted_iota(I32, (tq, LANES), 1)
    cq = jnp.sum(jnp.where(lane == head, cc_ref[0], 0.0), axis=1, keepdims=True)
    n_lt = tq // LANES
    lane_tiles = lambda a: [a[:, c * LANES:(c + 1) * LANES] for c in range(n_lt)]

    def score_tile(kj, masked):
        rows = pl.ds(pl.multiple_of(kj * tq, tq), tq)
        s = _nt(qs, kb[rows, :]) + (cq - cr_ref[0, 0, pl.ds(kj, 1), :])
        if masked:
            s = jnp.where(lax.broadcasted_iota(I32, (tq, tq), 1) <= lax.broadcasted_iota(I32, (tq, tq), 0),
                          s, -jnp.inf)
        s_buf[kj] = s
        return functools.reduce(jnp.maximum, lane_tiles(s))

    m_s[...] = score_tile(qi, True)

    def scores(kj, carry):
        m_s[...] = jnp.maximum(m_s[...], score_tile(kj, False))
        return carry

    lax.fori_loop(0, qi, scores, 0)
    m_s[...] = jnp.broadcast_to(jnp.max(m_s[...], axis=1, keepdims=True), m_s.shape)

    l_s[...] = jnp.zeros(l_s.shape, F32)

    def exps(kj, carry):
        m_rep = m_s[...]
        e_tiles = [jnp.exp(t - m_rep) for t in lane_tiles(s_buf[kj])]
        s_buf[kj] = jnp.concatenate(e_tiles, axis=1)
        l_s[...] += functools.reduce(jnp.add, e_tiles)
        return carry

    lax.fori_loop(0, qi + 1, exps, 0)
    l_s[...] = jnp.broadcast_to(1.0 / jnp.sum(l_s[...], axis=1, keepdims=True), l_s.shape)

    acc[...] = jnp.zeros(acc.shape, F32)

    def values(kj, carry):
        rows = pl.ds(pl.multiple_of(kj * tq, tq), tq)
        inv_rep = l_s[...]
        p = jnp.concatenate([(t * inv_rep).astype(BF16) for t in lane_tiles(s_buf[kj])], axis=1)
        acc[...] += jnp.dot(p, vb[rows, :], preferred_element_type=F32)
        return carry

    lax.fori_loop(0, qi + 1, values, 0)
    o_ref[...] = acc[...].astype(o_ref.dtype)


def _fox_attn_prompt(q, k, v, c_col, c_row, bsz, n, heads, tq):
    dh = q.shape[1] // heads
    nq = n // tq
    qmap = lambda b, h, i: (b * nq + i, h)
    kmap = lambda b, h, i: (b, h)
    return pl.pallas_call(
        functools.partial(_fox_attn_kernel, scale=dh ** -0.5, tq=tq),
        grid=(bsz, heads, nq),
        in_specs=[pl.BlockSpec((tq, dh), qmap), pl.BlockSpec((n, dh), kmap), pl.BlockSpec((n, dh), kmap),
                  pl.BlockSpec((1, tq, LANES), lambda b, h, i: (b, i, 0)),
                  pl.BlockSpec((1, 1, nq, tq), lambda b, h, i: (b, h, 0, 0))],
        out_specs=pl.BlockSpec((tq, dh), qmap),
        out_shape=jax.ShapeDtypeStruct((bsz * n, heads * dh), BF16),
        scratch_shapes=[pltpu.VMEM((n, dh), BF16), pltpu.VMEM((n, dh), BF16), pltpu.VMEM((nq, tq, tq), F32),
                        pltpu.VMEM((tq, LANES), F32), pltpu.VMEM((tq, LANES), F32), pltpu.VMEM((tq, dh), F32)],
        compiler_params=_params(("parallel", "parallel", "arbitrary")),
        name="fox_attn_prompt",
    )(q, k, v, c_col, c_row)


DECODE_PAGES_PER_STEP = 4


def _fox_decode_kernel(pt_ref, q_ref, kn_ref, vn_ref, f_ref, bf_ref, *rest, scale, heads, dh, page, pps):
    kp_refs, vp_refs, lf_refs = rest[:pps], rest[pps:2 * pps], rest[2 * pps:3 * pps]
    o_ref, lfo_ref, carry, m_s, l_s, acc = rest[3 * pps:]
    g = pl.program_id(1)
    sub_s = lax.broadcasted_iota(I32, (heads, LANES), 0)
    lane_s = lax.broadcasted_iota(I32, (heads, LANES), 1)

    @pl.when(g == 0)
    def _():
        lf_row = _log_sigmoid(f_ref[0] + bf_ref[...])
        lfo_ref[0] = lf_row
        lf_col = jnp.sum(jnp.where(lane_s == sub_s, lf_row, 0.0), axis=1, keepdims=True)
        carry[...] = jnp.broadcast_to(lf_col, carry.shape)
        s_new = jnp.sum(q_ref[0] * scale * kn_ref[0], axis=1, keepdims=True)
        m_s[...] = jnp.broadcast_to(s_new, m_s.shape)
        l_s[...] = jnp.ones(l_s.shape, F32)
        acc[...] = vn_ref[0]

    q3 = (q_ref[0] * scale)[None]
    r_i = lax.broadcasted_iota(I32, (page, page), 0)
    c_i = lax.broadcasted_iota(I32, (page, page), 1)
    newer = jnp.where(r_i > c_i, 1.0, 0.0).astype(BF16)
    tok3 = lax.broadcasted_iota(I32, (page, heads, dh), 0)
    lane3 = lax.broadcasted_iota(I32, (page, heads, dh), 2)
    on_diag = lane3 == tok3
    for kp_ref, vp_ref, lf_ref in zip(kp_refs, vp_refs, lf_refs):
        lf = lf_ref[0, 0]
        bias = carry[...] + _dot_exact_rhs(lf, newer)
        carry[...] = carry[...] + jnp.sum(lf, axis=1, keepdims=True)
        prod = kp_ref[0, 0] * q3 + jnp.where(on_diag, bias[None], 0.0)
        s3 = jnp.broadcast_to(jnp.sum(prod, axis=2, keepdims=True), prod.shape)
        m_prev = m_s[...]
        m_new = jnp.maximum(m_prev, jnp.max(s3, axis=0))
        alpha = jnp.exp(m_prev - m_new)
        p3 = jnp.exp(s3 - m_new[None])
        l_s[...] = alpha * l_s[...] + jnp.sum(p3, axis=0)
        acc[...] = alpha * acc[...] + jnp.sum(p3 * vp_ref[0, 0], axis=0)
        m_s[...] = m_new

    @pl.when(g == pl.num_programs(1) - 1)
    def _():
        o_ref[0] = (acc[...] / l_s[...]).astype(o_ref.dtype)


def _fox_decode(q, k_new, v_new, f_raw, b_f, k_pool, v_pool, lf_pool_t, page_table, layer, heads):
    n_seq, _, dh = q.shape
    page = k_pool.shape[2]
    n_pages = page_table.shape[1]
    pps = math.gcd(DECODE_PAGES_PER_STEP, n_pages)
    assert page == LANES and dh == LANES, "one vreg per cached token; token index doubles as a lane index"
    row = lambda b, g, pt: (b, 0, 0)

    def page_map(j, ndim):
        def index(b, g, pt):
            return (layer, pt[b * n_pages + (n_pages - 1 - (g * pps + j))]) + (0,) * (ndim - 2)
        return index

    kv_specs = [pl.BlockSpec((1, 1, page, heads, dh), page_map(j, 5)) for j in range(pps)]
    lf_specs = [pl.BlockSpec((1, 1, heads, page), page_map(j, 4)) for j in range(pps)]
    hblk = pl.BlockSpec((1, heads, dh), row)
    grid_spec = pltpu.PrefetchScalarGridSpec(
        num_scalar_prefetch=1,
        grid=(n_seq, n_pages // pps),
        in_specs=[hblk, hblk, hblk, pl.BlockSpec((1, 1, LANES), row),
                  pl.BlockSpec((1, LANES), lambda b, g, pt: (0, 0))] + kv_specs + kv_specs + lf_specs,
        out_specs=[hblk, pl.BlockSpec((1, 1, LANES), row)],
        scratch_shapes=[pltpu.VMEM((heads, LANES), F32), pltpu.VMEM((heads, dh), F32),
                        pltpu.VMEM((heads, dh), F32), pltpu.VMEM((heads, dh), F32)],
    )
    return pl.pallas_call(
        functools.partial(_fox_decode_kernel, scale=dh ** -0.5, heads=heads, dh=dh, page=page, pps=pps),
        grid_spec=grid_spec,
        out_shape=[jax.ShapeDtypeStruct((n_seq, heads, dh), BF16), jax.ShapeDtypeStruct((n_seq, 1, LANES), F32)],
        compiler_params=_params(("parallel", "arbitrary")),
        name="fox_decode",
    )(page_table.reshape(-1), q, k_new, v_new, f_raw, b_f, *([k_pool] * pps), *([v_pool] * pps),
      *([lf_pool_t] * pps))


def _fox_proj_spec(j, p):
    hd = p["cache_k_fox"].shape[3] * p["cache_k_fox"].shape[4]
    w = _pad_cols(p["w_in_fox"][j], 3 * hd + LANES).astype(BF16)
    return w, [(0, hd), (hd, hd), (2 * hd, hd), (3 * hd, LANES)]


def _fox_layer(proj, j, bsz, n, tp, p):
    q, k, v, f_raw = proj
    t_all = q.shape[0]
    heads = p["cache_k_fox"].shape[3]
    dh = p["cache_k_fox"].shape[4]
    hd = heads * dh
    b_f = _pad_cols(p["b_forget_fox"][j][None, :], LANES)
    lf_p, c_p = _fox_gate(f_raw[:tp].reshape(bsz, n, LANES), b_f, _tile(n, (512, 256, 128)))
    tq = _tile(n, (512, 256, 128))
    c_row = c_p[:, :, :heads].transpose(0, 2, 1).reshape(bsz, heads, n // tq, tq)
    o_p = _fox_attn_prompt(q, k, v, c_p, c_row, bsz, n, heads, tq)
    n_seq = t_all - tp
    s3 = lambda a: a[tp:].reshape(n_seq, heads, dh)
    lf_pool_t = p["cache_logf_fox"].transpose(0, 1, 3, 2)
    o_s, lf_s = _fox_decode(s3(q), s3(k), s3(v), f_raw[tp:].reshape(n_seq, 1, LANES), b_f,
                            p["cache_k_fox"], p["cache_v_fox"], lf_pool_t, p["page_table"], j, heads)
    o = jnp.concatenate([o_p, o_s.reshape(n_seq, hd)], axis=0)
    outs = dict(
        k_p=k[:tp].reshape(bsz, n, heads, dh), v_p=v[:tp].reshape(bsz, n, heads, dh),
        lf_p=lf_p[:, :, :heads],
        k_s=k[tp:].reshape(n_seq, 1, heads, dh), v_s=v[tp:].reshape(n_seq, 1, heads, dh),
        lf_s=lf_s[:, :, :heads])
    return o, outs


HALO = SUBLANES


def _causal_taps(hist, u, w_ref, first_tile, tl):
    width = w_ref.shape[0]
    u = _round_bf16(u)

    @pl.when(first_tile)
    def _():
        hist[0:HALO, :] = jnp.zeros((HALO, hist.shape[1]), F32)

    @pl.when(jnp.logical_not(first_tile))
    def _():
        hist[0:HALO, :] = hist[tl:tl + HALO, :]

    hist[HALO:HALO + tl, :] = u
    out = w_ref[width - 1:width, :] * u
    for k in range(width - 1):
        shift = width - 1 - k
        out = out + w_ref[k:k + 1, :] * hist[HALO - shift:HALO - shift + tl, :]
    return out


def _short_prompt_kernel(b_ref, c_ref, h_ref, w_ref, y_ref, st_ref, hist, *, tl):
    u = c_ref[...] * h_ref[...]
    conv = _causal_taps(hist, u, w_ref, pl.program_id(1) == 0, tl)
    y_ref[...] = (b_ref[...] * conv).astype(y_ref.dtype)
    keep = w_ref.shape[0] - 1
    st_ref[0] = u[tl - keep:tl, :]


def _short_prompt(bg, cg, hg, w_conv, bsz, n, t_all, tl):
    d = bg.shape[1]
    width = w_conv.shape[0]
    nl = n // tl
    row = lambda b, l: (b * nl + l, 0)
    return pl.pallas_call(
        functools.partial(_short_prompt_kernel, tl=tl),
        grid=(bsz, nl),
        in_specs=[pl.BlockSpec((tl, d), row)] * 3 + [pl.BlockSpec((width, d), lambda b, l: (0, 0))],
        out_specs=[pl.BlockSpec((tl, d), row), pl.BlockSpec((1, width - 1, d), lambda b, l: (b, 0, 0))],
        out_shape=[jax.ShapeDtypeStruct((t_all, d), BF16), jax.ShapeDtypeStruct((bsz, width - 1, d), F32)],
        scratch_shapes=[pltpu.VMEM((HALO + tl, d), F32)],
        compiler_params=_params(("parallel", "arbitrary")),
        name="short_prompt",
    )(bg, cg, hg, w_conv)


def _short_decode_kernel(b_ref, c_ref, h_ref, s_ref, w_ref, y_ref, u_ref):
    u = c_ref[...] * h_ref[...]
    width = w_ref.shape[0]
    conv = w_ref[width - 1:width, :] * _round_bf16(u)
    for k in range(width - 1):
        conv = conv + w_ref[k:k + 1, :] * _round_bf16(s_ref[k])
    y_ref[...] = (b_ref[...] * conv).astype(y_ref.dtype)
    u_ref[...] = u


def _short_decode(bg, cg, hg, state_t, w_conv):
    n_seq, d = bg.shape
    return pl.pallas_call(
        _short_decode_kernel,
        out_shape=[jax.ShapeDtypeStruct((n_seq, d), BF16), jax.ShapeDtypeStruct((n_seq, d), F32)],
        compiler_params=pltpu.CompilerParams(vmem_limit_bytes=VMEM_LIMIT),
        name="short_decode",
    )(bg, cg, hg, state_t, w_conv)


def _short_proj_spec(j, p):
    d = p["w_in_short"].shape[1]
    return p["w_in_short"][j].astype(BF16), [(0, d), (d, d), (2 * d, d)]


def _short_layer(proj, j, bsz, n, tp, p):
    bg, cg, hg = proj
    w_conv = p["w_conv_short"][j]
    y_p, st_p = _short_prompt(bg, cg, hg, w_conv, bsz, n, tp, _tile(n, (512, 256, 128)))
    state = p["state_conv_short"][j]
    y_s, u_s = _short_decode(bg[tp:], cg[tp:], hg[tp:], state.transpose(1, 0, 2), w_conv)
    y = jnp.concatenate([y_p, y_s], axis=0)
    st_s = jnp.concatenate([state[:, 1:], u_s[:, None, :]], axis=1)
    return y, dict(conv_p=st_p, conv_s=st_s)


def _gla_gate_kernel(r_ref, w_ref, b_ref, o_ref):
    z = jnp.dot(r_ref[...].astype(BF16), w_ref[...], preferred_element_type=F32) + b_ref[...]
    o_ref[...] = _log_sigmoid(z) * (1.0 / GLA_TAU)


def _gla_gate(r, w_gate, b_gate, tm):
    t = r.shape[0]
    dk = w_gate.shape[1]
    row = lambda i: (i, 0)
    full = lambda i: (0, 0)
    return pl.pallas_call(
        _gla_gate_kernel,
        grid=(t // tm,),
        in_specs=[pl.BlockSpec((tm, LANES), row), pl.BlockSpec((LANES, dk), full), pl.BlockSpec((1, dk), full)],
        out_specs=pl.BlockSpec((tm, dk), row),
        out_shape=jax.ShapeDtypeStruct((t, dk), F32),
        compiler_params=_params(("parallel",)),
        name="gla_gate",
    )(r, w_gate, b_gate)


def _rms_gate(o, g_norm, gate):
    on = o * lax.rsqrt(jnp.mean(o * o, axis=-1, keepdims=True) + RMS_EPS) * g_norm
    return on * _silu(gate)


def _gla_prompt_kernel(q_ref, k_ref, v_ref, g_ref, la_ref, gn_ref, o_ref, st_ref, state,
                       *, tl, heads, dk, dv, scale):
    l_idx = pl.program_id(1)

    @pl.when(l_idx == 0)
    def _():
        state[...] = jnp.zeros(state.shape, F32)

    cs = GLA_CHUNK
    row = lax.broadcasted_iota(I32, (cs, cs), 0)
    col = lax.broadcasted_iota(I32, (cs, cs), 1)
    causal = row >= col
    tri = jnp.where(causal, 1.0, 0.0).astype(BF16)
    for c in range(tl // cs):
        rows = slice(c * cs, (c + 1) * cs)
        for h in range(heads):
            kcols = slice(h * dk, (h + 1) * dk)
            vcols = slice(h * dv, (h + 1) * dv)
            b = _dot_exact_lhs(tri, la_ref[rows, kcols])
            b_last = b[cs - 1:cs, :]
            kh = k_ref[rows, kcols]
            q_dec = (q_ref[rows, kcols] * scale * jnp.exp(b)).astype(BF16)
            k_inv = (kh * jnp.exp(-b)).astype(BF16)
            k_end = (kh * jnp.exp(b_last - b)).astype(BF16)
            vh = v_ref[rows, vcols].astype(BF16)
            scores = jnp.where(causal, _nt(q_dec, k_inv), 0.0).astype(BF16)
            s_t = state[h]
            o = jnp.dot(scores, vh, preferred_element_type=F32) + _nt(q_dec, s_t.astype(BF16))
            state[h] = s_t * jnp.exp(b_last) + _tn(vh, k_end)
            o_ref[rows, vcols] = _rms_gate(o, gn_ref[...], g_ref[rows, vcols]).astype(o_ref.dtype)

    @pl.when(l_idx == pl.num_programs(1) - 1)
    def _():
        st_ref[0] = state[...]


def _gla_prompt(q, k, v, g, log_a, g_norm, bsz, n, heads, t_all, tl):
    dk = q.shape[1] // heads
    dv = v.shape[1] // heads
    nl = n // tl
    row = lambda b, l: (b * nl + l, 0)
    return pl.pallas_call(
        functools.partial(_gla_prompt_kernel, tl=tl, heads=heads, dk=dk, dv=dv, scale=dk ** -0.5),
        grid=(bsz, nl),
        in_specs=[pl.BlockSpec((tl, heads * dk), row), pl.BlockSpec((tl, heads * dk), row),
                  pl.BlockSpec((tl, heads * dv), row), pl.BlockSpec((tl, heads * dv), row),
                  pl.BlockSpec((tl, heads * dk), row), pl.BlockSpec((1, dv), lambda b, l: (0, 0))],
        out_specs=[pl.BlockSpec((tl, heads * dv), row),
                   pl.BlockSpec((1, heads, dv, dk), lambda b, l: (b, 0, 0, 0))],
        out_shape=[jax.ShapeDtypeStruct((t_all, heads * dv), BF16),
                   jax.ShapeDtypeStruct((bsz, heads, dv, dk), F32)],
        scratch_shapes=[pltpu.VMEM((heads, dv, dk), F32)],
        compiler_params=_params(("parallel", "arbitrary")),
        name="gla_prompt",
    )(q, k, v, g, log_a, g_norm)


def _gla_decode_kernel(s_ref, cp_ref, v_ref, g_ref, gn_ref, so_ref, o_ref, *, heads, dv, scale):
    for h in range(heads):
        cols = cp_ref[0, h]
        decay = jnp.exp(cols[:, 0:1])
        vcols = slice(h * dv, (h + 1) * dv)
        s_new = s_ref[0, h] * decay + cols[:, 1:2] * v_ref[0, :, vcols]
        so_ref[0, h] = s_new
        o = jnp.sum(_round_bf16(s_new) * _round_bf16(cols[:, 2:3] * scale), axis=0, keepdims=True)
        o_ref[0, :, vcols] = _rms_gate(o, gn_ref[...], g_ref[0, :, vcols]).astype(o_ref.dtype)


def _gla_decode(state, colpack, v, g, g_norm):
    n_seq, heads, dk, dv = state.shape
    sblk = pl.BlockSpec((1, heads, dk, dv), lambda b: (b, 0, 0, 0))
    vblk = pl.BlockSpec((1, 1, heads * dv), lambda b: (b, 0, 0))
    return pl.pallas_call(
        functools.partial(_gla_decode_kernel, heads=heads, dv=dv, scale=dk ** -0.5),
        grid=(n_seq,),
        in_specs=[sblk, pl.BlockSpec((1, heads, dk, SUBLANES), lambda b: (b, 0, 0, 0)), vblk, vblk,
                  pl.BlockSpec((1, dv), lambda b: (0, 0))],
        out_specs=[sblk, vblk],
        out_shape=[jax.ShapeDtypeStruct(state.shape, F32), jax.ShapeDtypeStruct((n_seq, 1, heads * dv), BF16)],
        compiler_params=_params(("parallel",)),
        name="gla_decode",
    )(state, colpack, v, g, g_norm)


def _gla_proj_spec(j, p):
    heads, dk, dv = p["state_gla"].shape[2:]
    hk, hv = heads * dk, heads * dv
    w = _pad_cols(p["w_in_gla"][j], 2 * hk + 2 * hv + LANES).astype(BF16)
    return w, [(0, hk), (hk, hk), (2 * hk, hv), (2 * hk + hv, hv), (2 * hk + 2 * hv, LANES)]


def _gla_layer(proj, j, bsz, n, tp, p):
    q, k, v, g, r = proj
    t_all = q.shape[0]
    n_seq, heads, dk, dv = p["state_gla"].shape[1:]
    hk, hv = heads * dk, heads * dv
    rank = p["w_gate2_gla"].shape[1]
    tm = _tile(t_all, (384, 256, 128))
    w_gate = jnp.pad(p["w_gate2_gla"][j], ((0, LANES - rank), (0, 0))).astype(BF16)
    log_a = _gla_gate(r, w_gate, p["b_gate_gla"][j][None, :], tm)
    g_norm = p["g_norm_gla"][j][None, :]
    o_p, st_p = _gla_prompt(q, k, v, g, log_a, g_norm, bsz, n, heads, tp, _tile(n, (256, 128, 64)))
    colpack = jnp.stack([log_a[tp:], k[tp:], q[tp:]], axis=-1).reshape(n_seq, heads, dk, 3)
    colpack = jnp.pad(colpack, ((0, 0), (0, 0), (0, 0), (0, SUBLANES - 3)))
    st_s, o_s = _gla_decode(p["state_gla"][j], colpack, v[tp:].reshape(n_seq, 1, hv),
                            g[tp:].reshape(n_seq, 1, hv), g_norm)
    o = jnp.concatenate([o_p, o_s.reshape(n_seq, hv)], axis=0)
    return o, dict(gla_p=st_p.transpose(0, 1, 3, 2), gla_s=st_s)


def _ssd_conv_prompt_kernel(x_ref, w_ref, b_ref, a_ref, st_ref, hist, *, tl):
    xbc = x_ref[...]
    conv = _causal_taps(hist, xbc, w_ref, pl.program_id(1) == 0, tl)
    a_ref[...] = _silu(conv + b_ref[...])
    keep = w_ref.shape[0] - 1
    st_ref[0] = xbc[tl - keep:tl, :]


def _ssd_conv_prompt(xbc, w_conv, b_conv, bsz, n, t_all, tl):
    ch = xbc.shape[1]
    width = w_conv.shape[0]
    nl = n // tl
    row = lambda b, l: (b * nl + l, 0)
    full = lambda b, l: (0, 0)
    return pl.pallas_call(
        functools.partial(_ssd_conv_prompt_kernel, tl=tl),
        grid=(bsz, nl),
        in_specs=[pl.BlockSpec((tl, ch), row), pl.BlockSpec((width, ch), full), pl.BlockSpec((1, ch), full)],
        out_specs=[pl.BlockSpec((tl, ch), row), pl.BlockSpec((1, width - 1, ch), lambda b, l: (b, 0, 0))],
        out_shape=[jax.ShapeDtypeStruct((t_all, ch), F32), jax.ShapeDtypeStruct((bsz, width - 1, ch), F32)],
        scratch_shapes=[pltpu.VMEM((HALO + tl, ch), F32)],
        compiler_params=_params(("parallel", "arbitrary")),
        name="ssd_conv_prompt",
    )(xbc, w_conv, b_conv)


def _ssd_conv_decode_kernel(x_ref, s_ref, w_ref, b_ref, a_ref):
    width = w_ref.shape[0]
    conv = w_ref[width - 1:width, :] * _round_bf16(x_ref[...])
    for k in range(width - 1):
        conv = conv + w_ref[k:k + 1, :] * _round_bf16(s_ref[k])
    a_ref[...] = _silu(conv + b_ref[...])


def _ssd_conv_decode(xbc, state_t, w_conv, b_conv):
    return pl.pallas_call(
        _ssd_conv_decode_kernel,
        out_shape=jax.ShapeDtypeStruct(xbc.shape, F32),
        compiler_params=pltpu.CompilerParams(vmem_limit_bytes=VMEM_LIMIT),
        name="ssd_conv_decode",
    )(xbc, state_t, w_conv, b_conv)


def _group_rms_gate(y, z, g_norm):
    yg = y * _silu(z)
    return yg * lax.rsqrt(jnp.mean(yg * yg, axis=-1, keepdims=True) + RMS_EPS) * g_norm


def _ssd_prompt_kernel(xs_ref, bm_ref, cm_ref, dt_ref, z_ref, dtb_ref, alog_ref, dsk_ref, gn_ref,
                       y_ref, st_ref, state, ybuf, *, q, heads, hd, ns, groups):
    l_idx = pl.program_id(1)

    @pl.when(l_idx == 0)
    def _():
        state[...] = jnp.zeros(state.shape, F32)

    pw = 2 * hd
    dt = _softplus(dt_ref[...] + dtb_ref[...])
    a = -jnp.exp(alog_ref[...])
    row = lax.broadcasted_iota(I32, (q, q), 0)
    col = lax.broadcasted_iota(I32, (q, q), 1)
    causal = row >= col
    tri = jnp.where(causal, 1.0, 0.0).astype(BF16)
    cum = _dot_exact_lhs(tri, dt * a)
    cum_t = cum.T
    e_cum = jnp.exp(cum)
    w_end = jnp.exp(cum[q - 1:q, :] - cum)
    dsk = dsk_ref[...]
    lo = lax.broadcasted_iota(I32, (q, pw), 1) < hd
    lo_rows = lax.broadcasted_iota(I32, (pw, ns), 0) < hd
    pairs_per_group = heads // groups // 2

    def pick(mat, h0):
        return jnp.where(lo, mat[:, h0:h0 + 1], mat[:, h0 + 1:h0 + 2])

    for g in range(groups):
        cmb = cm_ref[:, g * ns:(g + 1) * ns].astype(BF16)
        bmb = bm_ref[:, g * ns:(g + 1) * ns].astype(BF16)
        cb = _nt(cmb, bmb)
        for pp in range(pairs_per_group):
            pr = g * pairs_per_group + pp
            h0 = 2 * pr
            cols = slice(pr * pw, (pr + 1) * pw)
            x_pair = xs_ref[:, cols]
            xdt = x_pair * pick(dt, h0)

            def decay_mix(h):
                return (cb * jnp.exp(jnp.where(causal, cum[:, h:h + 1] - cum_t[h:h + 1, :], -jnp.inf))).astype(BF16)

            y = (jnp.dot(decay_mix(h0), jnp.where(lo, xdt, 0.0).astype(BF16), preferred_element_type=F32)
                 + jnp.dot(decay_mix(h0 + 1), jnp.where(lo, 0.0, xdt).astype(BF16), preferred_element_type=F32))
            s_pair = state[pr]
            y = y + _nt(cmb, s_pair.astype(BF16)) * pick(e_cum, h0)
            dec = jnp.where(lo_rows, jnp.exp(cum_t[h0:h0 + 1, q - 1:q]), jnp.exp(cum_t[h0 + 1:h0 + 2, q - 1:q]))
            state[pr] = s_pair * dec + _tn((xdt * pick(w_end, h0)).astype(BF16), bmb)
            ybuf[:, cols] = y + x_pair * pick(dsk, h0)

    gw = heads * hd // groups
    for g in range(groups):
        cols = slice(g * gw, (g + 1) * gw)
        y_ref[:, cols] = _group_rms_gate(ybuf[:, cols], z_ref[:, cols], gn_ref[:, cols]).astype(y_ref.dtype)

    @pl.when(l_idx == pl.num_programs(1) - 1)
    def _():
        st_ref[0] = state[...]


def _ssd_prompt(act, dt_raw, z, dt_bias, a_log, d_skip, g_norm, bsz, n, heads, hd, ns, groups, t_all, q):
    inner = heads * hd
    gn = groups * ns
    nl = n // q
    row = lambda b, l: (b * nl + l, 0)
    full = lambda b, l: (0, 0)
    n_pairs = heads // 2
    return pl.pallas_call(
        functools.partial(_ssd_prompt_kernel, q=q, heads=heads, hd=hd, ns=ns, groups=groups),
        grid=(bsz, nl),
        in_specs=[pl.BlockSpec((q, inner), row),
                  pl.BlockSpec((q, gn), lambda b, l: (b * nl + l, inner // gn)),
                  pl.BlockSpec((q, gn), lambda b, l: (b * nl + l, inner // gn + 1)),
                  pl.BlockSpec((q, LANES), row), pl.BlockSpec((q, inner), row),
                  pl.BlockSpec((1, LANES), full), pl.BlockSpec((1, LANES), full), pl.BlockSpec((1, LANES), full),
                  pl.BlockSpec((1, inner), full)],
        out_specs=[pl.BlockSpec((q, inner), row),
                   pl.BlockSpec((1, n_pairs, 2 * hd, ns), lambda b, l: (b, 0, 0, 0))],
        out_shape=[jax.ShapeDtypeStruct((t_all, inner), BF16),
                   jax.ShapeDtypeStruct((bsz, n_pairs, 2 * hd, ns), F32)],
        scratch_shapes=[pltpu.VMEM((n_pairs, 2 * hd, ns), F32), pltpu.VMEM((q, inner), F32)],
        compiler_params=_params(("parallel", "arbitrary")),
        name="ssd_prompt",
    )(act, act, act, dt_raw, z, dt_bias, a_log, d_skip, g_norm)


SSD_PACK_ROWS = 16


def _ssd_decode_kernel(s_ref, pack_ref, b_ref, c_ref, z_ref, gn_ref, so_ref, y_ref, *, n_pairs, ns, groups):
    pt = pack_ref[0].T
    pw = s_ref.shape[2]
    lane = lax.broadcasted_iota(I32, (pw, LANES), 1)
    pairs_per_group = n_pairs // groups
    quantity = lambda k: pt[:, k * SSD_PACK_ROWS:k * SSD_PACK_ROWS + n_pairs]
    x_all = quantity(0)
    dt_all = _softplus(quantity(1) + quantity(2))
    dec_all = jnp.exp(dt_all * -jnp.exp(quantity(3)))
    xdt_all = x_all * dt_all
    skip_all = x_all * quantity(4)
    y_cols = jnp.zeros((pw, LANES), F32)
    for pr in range(n_pairs):
        g = pr // pairs_per_group
        b_row = b_ref[0, :, g * ns:(g + 1) * ns]
        c_row = c_ref[0, :, g * ns:(g + 1) * ns]
        s_new = s_ref[0, pr] * dec_all[:, pr:pr + 1] + xdt_all[:, pr:pr + 1] * b_row
        so_ref[0, pr] = s_new
        y_col = jnp.sum(_round_bf16(s_new) * _round_bf16(c_row), axis=1, keepdims=True)
        y_cols = jnp.where(lane == pr, y_col, y_cols)
    y_cols = y_cols + jnp.pad(skip_all, ((0, 0), (0, LANES - n_pairs)))
    y_rows = y_cols.T[0:n_pairs, :]
    yg = y_rows * _silu(z_ref[0])
    sq = jnp.sum(yg * yg, axis=1, keepdims=True)
    out = jnp.zeros(yg.shape, F32)
    sub = lax.broadcasted_iota(I32, yg.shape, 0)
    sub1 = lax.broadcasted_iota(I32, sq.shape, 0)
    for g in range(groups):
        lo_r, hi_r = g * pairs_per_group, (g + 1) * pairs_per_group
        in_g1 = jnp.logical_and(sub1 >= lo_r, sub1 < hi_r)
        ms = jnp.sum(jnp.where(in_g1, sq, 0.0), axis=0, keepdims=True) / (pairs_per_group * pw)
        out = jnp.where(jnp.logical_and(sub >= lo_r, sub < hi_r), yg * lax.rsqrt(ms + RMS_EPS), out)
    y_ref[0] = (out * gn_ref[...]).astype(y_ref.dtype)


def _ssd_decode(state, pack, b_rows, c_rows, z, g_norm, groups):
    n_seq, n_pairs, pw, ns = state.shape
    sblk = pl.BlockSpec((1, n_pairs, pw, ns), lambda b: (b, 0, 0, 0))
    rblk = pl.BlockSpec((1, 1, groups * ns), lambda b: (b, 0, 0))
    zblk = pl.BlockSpec((1, n_pairs, pw), lambda b: (b, 0, 0))
    return pl.pallas_call(
        functools.partial(_ssd_decode_kernel, n_pairs=n_pairs, ns=ns, groups=groups),
        grid=(n_seq,),
        in_specs=[sblk, pl.BlockSpec((1, LANES, LANES), lambda b: (b, 0, 0)), rblk, rblk, zblk,
                  pl.BlockSpec((n_pairs, pw), lambda b: (0, 0))],
        out_specs=[sblk, zblk],
        out_shape=[jax.ShapeDtypeStruct(state.shape, F32), jax.ShapeDtypeStruct((n_seq, n_pairs, pw), BF16)],
        compiler_params=_params(("parallel",)),
        name="ssd_decode",
    )(state, pack, b_rows, c_rows, z, g_norm)


def _ssd_proj_spec(j, p):
    heads, hd, ns = p["state_ssm"].shape[2:]
    inner = heads * hd
    ch = inner + 2 * SSD_GROUPS * ns
    w = _pad_cols(p["w_in_ssd"][j], inner + ch + LANES).astype(BF16)
    return w, [(0, inner), (inner, ch), (inner + ch, LANES)]


def _ssd_layer(proj, j, bsz, n, tp, p):
    z, xbc, dt_raw = proj
    n_seq, heads, hd, ns = p["state_ssm"].shape[1:]
    groups = SSD_GROUPS
    inner = heads * hd
    gn = groups * ns
    ch = inner + 2 * gn
    n_pairs = heads // 2
    pw = 2 * hd
    assert pw == LANES and ns == LANES and n_pairs == SSD_PACK_ROWS and inner % gn == 0
    w_conv = p["w_conv_ssd"][j]
    b_conv = p["b_conv_ssd"][j][None, :]
    lane_row = lambda v: _pad_cols(v[None, :], LANES)
    g_norm = p["g_norm_ssd"][j]
    act, conv_p = _ssd_conv_prompt(xbc, w_conv, b_conv, bsz, n, tp, _tile(n, (256, 128)))
    y_p, ssm_p = _ssd_prompt(act, dt_raw, z, lane_row(p["dt_bias_ssd"][j]), lane_row(p["a_log_ssd"][j]),
                           lane_row(p["d_skip_ssd"][j]), g_norm[None, :], bsz, n, heads, hd, ns, groups,
                           tp, _tile(n, (128,)))
    conv_state = p["state_conv_ssd"][j]
    xbc_s = xbc[tp:]
    act_s = _ssd_conv_decode(xbc_s, conv_state.transpose(1, 0, 2), w_conv, b_conv)
    per_row = lambda v: jnp.broadcast_to(jnp.repeat(v, hd, axis=-1).reshape(-1, n_pairs, pw), (n_seq, n_pairs, pw))
    pack = jnp.concatenate([
        act_s[:, :inner].reshape(n_seq, n_pairs, pw),
        per_row(dt_raw[tp:, :heads]), per_row(p["dt_bias_ssd"][j][None, :]),
        per_row(p["a_log_ssd"][j][None, :]), per_row(p["d_skip_ssd"][j][None, :])], axis=1)
    pack = jnp.pad(pack, ((0, 0), (0, LANES - pack.shape[1]), (0, 0)))
    ssm_s, y_s = _ssd_decode(p["state_ssm"][j].reshape(n_seq, n_pairs, pw, ns), pack,
                             act_s[:, inner:inner + gn].reshape(n_seq, 1, gn),
                             act_s[:, inner + gn:].reshape(n_seq, 1, gn),
                             z[tp:].reshape(n_seq, n_pairs, pw), g_norm.reshape(n_pairs, pw), groups)
    y = jnp.concatenate([y_p, y_s.reshape(n_seq, inner)], axis=0)
    conv_s = jnp.concatenate([conv_state[:, 1:], xbc_s[:, None, :]], axis=1)
    return y, dict(ssm_p=ssm_p.reshape(bsz, heads, hd, ns), ssm_s=ssm_s.reshape(n_seq, heads, hd, ns),
                   conv_p=conv_p, conv_s=conv_s)


_LAYERS = (_fox_layer, _short_layer, _gla_layer, _ssd_layer)
_PROJ_SPECS = (_fox_proj_spec, _short_proj_spec, _gla_proj_spec, _ssd_proj_spec)
_OUT_ORDER = ("k", "v", "lf", "conv_short", "gla", "ssm", "conv_ssd")


def kernel(x_prompt, x_sample, cache_k_fox, cache_v_fox, cache_logf_fox, page_table, state_conv_short, state_gla, state_ssm, state_conv_ssd, w_in_fox, b_forget_fox, w_out_fox, w_in_short, w_conv_short, w_out_short, w_in_gla, w_gate2_gla, b_gate_gla, g_norm_gla, w_out_gla, w_in_ssd, w_conv_ssd, b_conv_ssd, dt_bias_ssd, a_log_ssd, d_skip_ssd, g_norm_ssd, w_out_ssd, ln_mix_g, ln_mix_b, ln_ffn_g, ln_ffn_b, w_router, b_router, w_gate_up, b_gate_up, w_down, b_down):
    p = dict(locals())
    bsz, n, d = x_prompt.shape
    n_seq, n_dec, _ = x_sample.shape
    assert n_dec == 1, "the sample group decodes one token per sequence"
    depth = ln_mix_g.shape[0]
    alpha = (2 * depth) ** 0.25
    tp = bsz * n
    x = jnp.concatenate([x_prompt.reshape(tp, d), x_sample.reshape(n_seq, d)], axis=0)
    w_outs = (w_out_fox, w_out_short, w_out_gla, w_out_ssd)
    acc = {name + sfx: [] for name in _OUT_ORDER for sfx in ("_p", "_s")}
    layer_of = lambda i: (i % len(_LAYERS), i // len(_LAYERS))
    w0, segs0 = _PROJ_SPECS[0](0, p)
    proj = _proj(x, w0, segs0, _tile(tp + n_seq, (384, 256, 128)))
    for i in range(depth):
        kind, j = layer_of(i)
        y_mix, outs = _LAYERS[kind](proj, j, bsz, n, tp, p)
        rename = {"conv_p": ("conv_short_p" if kind == 1 else "conv_ssd_p"),
                  "conv_s": ("conv_short_s" if kind == 1 else "conv_ssd_s")}
        for key, val in outs.items():
            acc[rename.get(key, key)].append(val)
        if i + 1 < depth:
            nkind, nj = layer_of(i + 1)
            x, *proj = _post_blocks(x, y_mix, w_outs[kind][j], i, alpha, p, _PROJ_SPECS[nkind](nj, p))
        else:
            x = _post_blocks(x, y_mix, w_outs[kind][j], i, alpha, p)
    stack = lambda name: jnp.stack(acc[name])
    return (x[:tp].reshape(bsz, n, d), x[tp:].reshape(n_seq, n_dec, d),
            *[stack(name + "_p") for name in _OUT_ORDER],
            *[stack(name + "_s") for name in _OUT_ORDER])
```

```python
import functools
import math

import jax
import jax.numpy as jnp
from jax import lax
from jax.experimental import pallas as pl
from jax.experimental.pallas import tpu as pltpu

F32 = jnp.float32
BF16 = jnp.bfloat16
I32 = jnp.int32

LANES = 128
SUBLANES = 8
VMEM_LIMIT = 56 * 1024 * 1024

LN_EPS = 1e-5
RMS_EPS = 1e-6
TOP_K = 4
SWIGLU_LIMIT = 7.0
SWIGLU_ALPHA = 1.702
GLA_TAU = 16.0
GLA_CHUNK = 64
SSD_GROUPS = 4
MOE_BLOCK = 256
FFN_UP_PHASES = 2
FFN_DOWN_PHASES = 2
GATHER_AHEAD = 2


def _params(semantics):
    return pltpu.CompilerParams(dimension_semantics=semantics, vmem_limit_bytes=VMEM_LIMIT)


def _tile(n, candidates):
    for c in candidates:
        if n % c == 0:
            return c
    raise ValueError(f"no tile for {n} among {candidates}")


def _round_up(n, m):
    return (n + m - 1) // m * m


def _log_sigmoid(z):
    return jnp.minimum(z, 0.0) - jnp.log(1.0 + jnp.exp(-jnp.abs(z)))


def _softplus(z):
    return jnp.maximum(z, 0.0) + jnp.log(1.0 + jnp.exp(-jnp.abs(z)))


def _silu(z):
    return z * jax.nn.sigmoid(z)


def _split3(x):
    hi = x.astype(BF16)
    r = x - hi.astype(F32)
    mid = r.astype(BF16)
    lo = (r - mid.astype(F32)).astype(BF16)
    return hi, mid, lo


def _dot_exact_lhs(a01, x):
    return sum(jnp.dot(a01, p, preferred_element_type=F32) for p in _split3(x))


def _dot_exact_rhs(x, a01):
    return sum(jnp.dot(p, a01, preferred_element_type=F32) for p in _split3(x))


def _round_bf16(x):
    return x.astype(BF16).astype(F32)


def _nt(a, b):
    return lax.dot_general(a, b, (((1,), (1,)), ((), ())), preferred_element_type=F32)


def _tn(a, b):
    return lax.dot_general(a, b, (((0,), (0,)), ((), ())), preferred_element_type=F32)


def _proj_kernel(x_ref, w_ref, *out_refs, segs, chunk):
    xb = x_ref[...].astype(BF16)
    for o_ref, (start, width) in zip(out_refs, segs):
        for c in range(0, width, chunk):
            cw = min(chunk, width - c)
            o_ref[:, c:c + cw] = jnp.dot(
                xb, w_ref[:, start + c:start + c + cw], preferred_element_type=F32).astype(o_ref.dtype)


def _proj(x, w, segs, tm):
    t, d = x.shape
    n = w.shape[1]
    assert all(s % LANES == 0 and wd % LANES == 0 for s, wd in segs)
    return pl.pallas_call(
        functools.partial(_proj_kernel, segs=tuple(segs), chunk=512),
        grid=(t // tm,),
        in_specs=[pl.BlockSpec((tm, d), lambda i: (i, 0)),
                  pl.BlockSpec((d, n), lambda i: (0, 0))],
        out_specs=[pl.BlockSpec((tm, wd), lambda i: (i, 0)) for _, wd in segs],
        out_shape=[jax.ShapeDtypeStruct((t, wd), F32) for _, wd in segs],
        compiler_params=_params(("parallel",)),
        name="proj",
    )(x, w)


def _pad_cols(w, n):
    return jnp.pad(w, ((0, 0), (0, n - w.shape[1])))


def _layer_norm_rows(z, g, b):
    mu = jnp.mean(z, axis=-1, keepdims=True)
    zc = z - mu
    var = jnp.mean(zc * zc, axis=-1, keepdims=True)
    return zc * lax.rsqrt(var + LN_EPS) * g + b


def _mix_ln_router_kernel(y_ref, w_ref, x_ref, g_ref, b_ref, wr_ref, br_ref,
                          x1_ref, topi_ref, gate_ref, rank_ref, cnt_ref, *, alpha, n_experts):
    tm = x_ref.shape[0]
    mix = jnp.dot(y_ref[...].astype(BF16), w_ref[...], preferred_element_type=F32)
    x1 = _layer_norm_rows(alpha * x_ref[...] + mix, g_ref[...], b_ref[...])
    x1_ref[...] = x1
    logits = jnp.dot(x1.astype(BF16), wr_ref[...], preferred_element_type=F32) + br_ref[...]
    lane = lax.broadcasted_iota(I32, logits.shape, 1)
    neg_inf = jnp.float32(-jnp.inf)
    cur = jnp.where(lane < n_experts, logits, neg_inf)
    topi = jnp.zeros(logits.shape, I32)
    chosen = jnp.zeros(logits.shape, F32)
    vals, ids = [], []
    for k in range(TOP_K):
        m = jnp.max(cur, axis=1, keepdims=True)
        idx = jnp.min(jnp.where(cur == m, lane, LANES), axis=1, keepdims=True)
        vals.append(m)
        ids.append(idx)
        topi = jnp.where(lane == k, idx, topi)
        chosen = jnp.where(lane == idx, 1.0, chosen)
        cur = jnp.where(lane == idx, neg_inf, cur)
    es = [jnp.exp(v - vals[0]) for v in vals]
    inv = 1.0 / sum(es)
    gate = jnp.zeros(logits.shape, F32)
    for k in range(TOP_K):
        gate = jnp.where(lane == k, es[k] * inv, gate)
    topi_ref[...] = topi
    gate_ref[...] = gate
    earlier = jnp.where(lax.broadcasted_iota(I32, (tm, tm), 0) > lax.broadcasted_iota(I32, (tm, tm), 1),
                        1.0, 0.0).astype(BF16)
    before = jnp.dot(earlier, chosen.astype(BF16), preferred_element_type=F32)
    rank = jnp.zeros(logits.shape, I32)
    for k in range(TOP_K):
        r_k = jnp.sum(jnp.where(lane == ids[k], before, 0.0), axis=1, keepdims=True)
        rank = jnp.where(lane == k, r_k.astype(I32), rank)
    rank_ref[...] = rank
    cnt_ref[0] = jnp.broadcast_to(jnp.sum(chosen, axis=0, keepdims=True), (SUBLANES, LANES)).astype(I32)


def _mix_ln_router(y, w_out, x, ln_g, ln_b, w_r, b_r, alpha, n_experts, tm):
    t, d = x.shape
    kdim = y.shape[1]
    row = lambda i: (i, 0)
    full = lambda i: (0, 0)
    return pl.pallas_call(
        functools.partial(_mix_ln_router_kernel, alpha=alpha, n_experts=n_experts),
        grid=(t // tm,),
        in_specs=[pl.BlockSpec((tm, kdim), row), pl.BlockSpec((kdim, d), full),
                  pl.BlockSpec((tm, d), row), pl.BlockSpec((1, d), full), pl.BlockSpec((1, d), full),
                  pl.BlockSpec((d, LANES), full), pl.BlockSpec((1, LANES), full)],
        out_specs=[pl.BlockSpec((tm, d), row), pl.BlockSpec((tm, LANES), row),
                   pl.BlockSpec((tm, LANES), row), pl.BlockSpec((tm, LANES), row),
                   pl.BlockSpec((1, SUBLANES, LANES), lambda i: (i, 0, 0))],
        out_shape=[jax.ShapeDtypeStruct((t, d), F32), jax.ShapeDtypeStruct((t, LANES), I32),
                   jax.ShapeDtypeStruct((t, LANES), F32), jax.ShapeDtypeStruct((t, LANES), I32),
                   jax.ShapeDtypeStruct((t // tm, SUBLANES, LANES), I32)],
        compiler_params=_params(("parallel",)),
        name="mix_ln_router",
    )(y, w_out, x, ln_g, ln_b, w_r, b_r)


def _route_tables(topi, rank, tile_cnt, n_experts, bm):
    t = topi.shape[0]
    n_pairs = t * TOP_K
    tm = t // tile_cnt.shape[0]
    e_ids = jnp.arange(n_experts, dtype=I32)
    cnt = tile_cnt[:, 0, :n_experts]
    tile_off = jnp.cumsum(cnt, axis=0) - cnt
    counts = jnp.sum(cnt, axis=0)
    starts = jnp.cumsum(counts) - counts
    padded = (counts + bm - 1) // bm * bm
    pad_ends = jnp.cumsum(padded)
    pad_starts = pad_ends - padded
    expert = topi[:, :TOP_K]
    base = jnp.repeat(pad_starts[None, :] + tile_off, tm, axis=0)
    hit = expert[:, :, None] == e_ids[None, None, :]
    pos = (jnp.sum(jnp.where(hit, base[:, None, :], 0), axis=2) + rank[:, :TOP_K]).reshape(-1).astype(I32)
    n_blocks = (n_pairs + bm - 1) // bm + n_experts + GATHER_AHEAD
    block_start = jnp.arange(n_blocks, dtype=I32) * bm
    block_expert = jnp.minimum(jnp.sum((pad_ends[None, :] <= block_start[:, None]).astype(I32), axis=1),
                               n_experts - 1).astype(I32)
    n_used = (pad_ends[-1] // bm).astype(I32).reshape(1)
    pair_bits = max(1, (n_pairs - 1).bit_length())
    assert (n_experts << pair_bits) < 2 ** 31
    keyed = jnp.sort(expert.reshape(-1) * (1 << pair_bits) + jnp.arange(n_pairs, dtype=I32))
    order = keyed & ((1 << pair_bits) - 1)
    e_row = jnp.repeat(block_expert, bm)
    off = jnp.arange(n_blocks * bm, dtype=I32) - pad_starts[e_row]
    src = jnp.clip(starts[e_row] + off, 0, n_pairs - 1)
    row_tok = jnp.where(off < counts[e_row], order[src] // TOP_K, 0).astype(I32)
    return block_expert, row_tok, n_used, pos, n_blocks


def _moe_ffn_kernel(be_ref, rt_ref, nu_ref, x_hbm, wgu_ref, bgu_ref, wd_ref, bd_ref,
                    y_ref, xbuf, sem, wgu_b, wd_b, xb, act, *, bm, d_ff):
    i = pl.program_id(0)
    n_used = nu_ref[0]
    n_slots = GATHER_AHEAD + 1
    slot = i % n_slots
    ahead_slot = (i + GATHER_AHEAD) % n_slots

    def row_copy(blk, r, dst_slot):
        tok = rt_ref[blk * bm + r]
        return pltpu.make_async_copy(x_hbm.at[pl.ds(tok, 1), :],
                                     xbuf.at[dst_slot, pl.ds(r, 1), :], sem.at[dst_slot])

    def wait_rows(dst_slot):
        pltpu.make_async_copy(x_hbm.at[pl.ds(0, bm), :], xbuf.at[dst_slot], sem.at[dst_slot]).wait()

    @pl.when(i == 0)
    def _():
        for blk in range(GATHER_AHEAD):
            def body(r, carry):
                row_copy(blk, r, blk).start()
                return carry
            lax.fori_loop(0, bm, body, 0)

    expert_changed = jnp.logical_or(i == 0, be_ref[jnp.maximum(i - 1, 0)] != be_ref[i])

    @pl.when(jnp.logical_and(i < n_used, expert_changed))
    def _():
        wgu_b[...] = wgu_ref[0, 0].astype(BF16)
        wd_b[...] = wd_ref[0, 0].astype(BF16)

    @pl.when(jnp.logical_and(i >= n_used, i < n_used + GATHER_AHEAD))
    def _():
        wait_rows(slot)

    @pl.when(i < n_used)
    def _():
        wait_rows(slot)
        xb[...] = xbuf[slot].astype(BF16)
        d = xb.shape[1]
        phases = [("up", c) for c in range(FFN_UP_PHASES)] + [("down", c) for c in range(FFN_DOWN_PHASES)]
        per_group = bm // (len(phases) - 1)
        uw, dw = d_ff // FFN_UP_PHASES, d // FFN_DOWN_PHASES
        zero = None
        for n, (kind, c) in enumerate(phases):
            if kind == "up":
                hg = (jnp.dot(xb[...], wgu_b[:, c * uw:(c + 1) * uw], preferred_element_type=F32)
                      + bgu_ref[0, 0, :, c * uw:(c + 1) * uw])
                hl = (jnp.dot(xb[...], wgu_b[:, d_ff + c * uw:d_ff + (c + 1) * uw], preferred_element_type=F32)
                      + bgu_ref[0, 0, :, d_ff + c * uw:d_ff + (c + 1) * uw])
                if zero is not None:
                    hg = hg + zero
                glu = jnp.minimum(hg, SWIGLU_LIMIT)
                lin = jnp.clip(hl, -SWIGLU_LIMIT, SWIGLU_LIMIT)
                act[:, c * uw:(c + 1) * uw] = (glu * jax.nn.sigmoid(SWIGLU_ALPHA * glu) * (lin + 1.0)).astype(BF16)
            else:
                y = (jnp.dot(act[...], wd_b[:, c * dw:(c + 1) * dw], preferred_element_type=F32)
                     + bd_ref[0, 0, :, c * dw:(c + 1) * dw])
                y_ref[:, c * dw:(c + 1) * dw] = y + zero
            if n < len(phases) - 1:
                for r in range(n * per_group, bm if n == len(phases) - 2 else (n + 1) * per_group):
                    row_copy(i + GATHER_AHEAD, r, ahead_slot).start()
                zero = jnp.minimum(jnp.abs(xbuf[slot, 0:1, 0:LANES]), 0.0)[:, 0:1]

    @pl.when(i >= n_used)
    def _():
        y_ref[...] = jnp.zeros(y_ref.shape, y_ref.dtype)


def _moe_ffn(x1, block_expert, row_tok, n_used, n_blocks, layer, w_gate_up, b_gate_up, w_down, b_down, bm):
    t, d = x1.shape
    depth, n_e, _, two_f = w_gate_up.shape
    d_ff = two_f // 2
    wmap = lambda i, be, rt, nu: (layer, be[i], 0, 0)
    grid_spec = pltpu.PrefetchScalarGridSpec(
        num_scalar_prefetch=3,
        grid=(n_blocks,),
        in_specs=[pl.BlockSpec(memory_space=pl.ANY),
                  pl.BlockSpec((1, 1, d, two_f), wmap),
                  pl.BlockSpec((1, 1, 1, two_f), wmap),
                  pl.BlockSpec((1, 1, d_ff, d), wmap),
                  pl.BlockSpec((1, 1, 1, d), wmap)],
        out_specs=pl.BlockSpec((bm, d), lambda i, be, rt, nu: (i, 0)),
        scratch_shapes=[pltpu.VMEM((GATHER_AHEAD + 1, bm, d), F32), pltpu.SemaphoreType.DMA((GATHER_AHEAD + 1,)),
                        pltpu.VMEM((d, two_f), BF16), pltpu.VMEM((d_ff, d), BF16), pltpu.VMEM((bm, d), BF16),
                        pltpu.VMEM((bm, d_ff), BF16)],
    )
    return pl.pallas_call(
        functools.partial(_moe_ffn_kernel, bm=bm, d_ff=d_ff),
        grid_spec=grid_spec,
        out_shape=jax.ShapeDtypeStruct((n_blocks * bm, d), F32),
        compiler_params=_params(("arbitrary",)),
        name="moe_ffn",
    )(block_expert, row_tok, n_used, x1, w_gate_up, b_gate_up.reshape(depth, n_e, 1, two_f),
      w_down, b_down.reshape(depth, n_e, 1, d))


def _moe_combine_kernel(pos_ref, y_hbm, gate_ref, x_ref, g_ref, b_ref, *rest, tm, alpha, segs, chunk):
    if segs:
        w_ref, o_ref, *proj_refs = rest[:-2]
    else:
        (o_ref,), proj_refs = rest[:-2], []
    buf, sem = rest[-2:]
    i = pl.program_id(0)
    nb = pl.num_programs(0)
    n_slots = GATHER_AHEAD + 1
    slot = i % n_slots
    ahead_slot = (i + GATHER_AHEAD) % n_slots
    n_rows = TOP_K * tm

    def row_copy(tile, j, dst_slot):
        r, k = j // TOP_K, j % TOP_K
        p = pos_ref[(tile * tm + r) * TOP_K + k]
        return pltpu.make_async_copy(y_hbm.at[pl.ds(p, 1), :],
                                     buf.at[dst_slot, pl.ds(k * tm + r, 1), :], sem.at[dst_slot])

    def wait_rows(dst_slot):
        pltpu.make_async_copy(y_hbm.at[pl.ds(0, n_rows), :], buf.at[dst_slot], sem.at[dst_slot]).wait()

    @pl.when(i == 0)
    def _():
        for tile in range(GATHER_AHEAD):
            def body(r, carry):
                for k in range(TOP_K):
                    row_copy(jnp.minimum(tile, nb - 1), r * TOP_K + k, tile).start()
                return carry
            lax.fori_loop(0, tm, body, 0)

    wait_rows(slot)
    gate = gate_ref[...]
    ffn = gate[:, 0:1] * buf[slot, pl.ds(0, tm), :]
    for k in range(1, TOP_K):
        ffn = ffn + gate[:, k:k + 1] * buf[slot, pl.ds(k * tm, tm), :]
    x2 = _layer_norm_rows(alpha * x_ref[...] + ffn, g_ref[...], b_ref[...])
    o_ref[...] = x2

    chunks = [(o, s, c, min(chunk, wd - c)) for o, (s, wd) in zip(proj_refs, segs) for c in range(0, wd, chunk)]
    n_groups = max(1, len(chunks))
    per_group = -(-n_rows // n_groups)
    next_tile = jnp.minimum(i + GATHER_AHEAD, nb - 1)
    xb = x2.astype(BF16)
    zero = None
    for n in range(n_groups):
        if chunks:
            o, s, c, cw = chunks[n]
            res = jnp.dot(xb, w_ref[:, s + c:s + c + cw], preferred_element_type=F32)
            o[:, c:c + cw] = res if zero is None else res + zero
        for j in range(n * per_group, min(n_rows, (n + 1) * per_group)):
            row_copy(next_tile, j, ahead_slot).start()
        zero = jnp.minimum(jnp.abs(buf[slot, 0:1, 0:LANES]), 0.0)[:, 0:1]

    @pl.when(i == nb - 1)
    def _():
        for extra in range(1, GATHER_AHEAD + 1):
            wait_rows((i + extra) % n_slots)


def _moe_combine(y_rows, pos, gates, x1, ln_g, ln_b, alpha, tm, w_next=None, segs=None):
    t, d = x1.shape
    segs = tuple(segs or ())
    row = lambda i, p: (i, 0)
    full = lambda i, p: (0, 0)
    in_specs = [pl.BlockSpec(memory_space=pl.ANY), pl.BlockSpec((tm, LANES), row),
                pl.BlockSpec((tm, d), row), pl.BlockSpec((1, d), full), pl.BlockSpec((1, d), full)]
    args = [pos, y_rows, gates, x1, ln_g, ln_b]
    if segs:
        assert all(s % LANES == 0 and wd % LANES == 0 for s, wd in segs)
        in_specs.append(pl.BlockSpec(w_next.shape, full))
        args.append(w_next)
    grid_spec = pltpu.PrefetchScalarGridSpec(
        num_scalar_prefetch=1,
        grid=(t // tm,),
        in_specs=in_specs,
        out_specs=[pl.BlockSpec((tm, d), row)] + [pl.BlockSpec((tm, wd), row) for _, wd in segs],
        scratch_shapes=[pltpu.VMEM((GATHER_AHEAD + 1, TOP_K * tm, d), F32),
                        pltpu.SemaphoreType.DMA((GATHER_AHEAD + 1,))],
    )
    outs = pl.pallas_call(
        functools.partial(_moe_combine_kernel, tm=tm, alpha=alpha, segs=segs, chunk=512),
        grid_spec=grid_spec,
        out_shape=[jax.ShapeDtypeStruct((t, d), F32)] + [jax.ShapeDtypeStruct((t, wd), F32) for _, wd in segs],
        compiler_params=_params(("arbitrary",)),
        name="moe_combine",
    )(*args)
    return tuple(outs) if segs else outs[0]


def _post_blocks(x, y_mix, w_out, i, alpha, p, next_proj=None):
    n_experts = p["w_router"].shape[-1]
    d = x.shape[1]
    tm = _tile(x.shape[0], (384, 256, 128))
    w_r = _pad_cols(p["w_router"][i], LANES).astype(BF16)
    b_r = _pad_cols(p["b_router"][i][None, :].astype(F32), LANES)
    x1, topi, gates, rank, tile_cnt = _mix_ln_router(
        y_mix, w_out.astype(BF16), x, p["ln_mix_g"][i].reshape(1, d), p["ln_mix_b"][i].reshape(1, d),
        w_r, b_r, alpha, n_experts, tm)
    block_expert, row_tok, n_used, pos, n_blocks = _route_tables(topi, rank, tile_cnt, n_experts, MOE_BLOCK)
    y_rows = _moe_ffn(x1, block_expert, row_tok, n_used, n_blocks, i, p["w_gate_up"], p["b_gate_up"],
                      p["w_down"], p["b_down"], MOE_BLOCK)
    w_next, segs = next_proj if next_proj is not None else (None, None)
    return _moe_combine(y_rows, pos, gates, x1, p["ln_ffn_g"][i].reshape(1, d),
                        p["ln_ffn_b"][i].reshape(1, d), alpha, _tile(x.shape[0], (128,)), w_next, segs)


def _fox_gate_kernel(f_ref, bf_ref, lf_ref, c_ref, carry, *, tl):
    @pl.when(pl.program_id(1) == 0)
    def _():
        carry[...] = jnp.zeros(carry.shape, F32)

    lf = _log_sigmoid(f_ref[0] + bf_ref[...])
    lf_ref[0] = lf
    row = lax.broadcasted_iota(I32, (tl, tl), 0)
    col = lax.broadcasted_iota(I32, (tl, tl), 1)
    tri = jnp.where(row >= col, 1.0, 0.0).astype(BF16)
    c = _dot_exact_lhs(tri, lf) + carry[0:1, :]
    c_ref[0] = c
    carry[0:1, :] = c[tl - 1:tl, :]


def _fox_gate(f_raw, b_f, tl):
    bsz, n, _ = f_raw.shape
    blk = pl.BlockSpec((1, tl, LANES), lambda b, l: (b, l, 0))
    return pl.pallas_call(
        functools.partial(_fox_gate_kernel, tl=tl),
        grid=(bsz, n // tl),
        in_specs=[blk, pl.BlockSpec((1, LANES), lambda b, l: (0, 0))],
        out_specs=[blk, blk],
        out_shape=[jax.ShapeDtypeStruct(f_raw.shape, F32)] * 2,
        scratch_shapes=[pltpu.VMEM((SUBLANES, LANES), F32)],
        compiler_params=_params(("parallel", "arbitrary")),
        name="fox_gate",
    )(f_raw, b_f)


def _fox_attn_kernel(q_ref, k_ref, v_ref, cc_ref, cr_ref, o_ref, kb, vb, s_buf, m_s, l_s, acc, *, scale, tq):
    head = pl.program_id(1)
    qi = pl.program_id(2)

    @pl.when(qi == 0)
    def _():
        kb[...] = k_ref[...].astype(BF16)
        vb[...] = v_ref[...].astype(BF16)

    qs = (q_ref[...] * scale).astype(BF16)
    lane = lax.broadcasted_iota(I32, (tq, LANES), 1)
    cq = jnp.sum(jnp.where(lane == head, cc_ref[0], 0.0), axis=1, keepdims=True)
    n_lt = tq // LANES
    lane_tiles = lambda a: [a[:, c * LANES:(c + 1) * LANES] for c in range(n_lt)]

    def score_tile(kj, masked):
        rows = pl.ds(pl.multiple_of(kj * tq, tq), tq)
        s = _nt(qs, kb[rows, :]) + (cq - cr_ref[0, 0, pl.ds(kj, 1), :])
        if masked:
            s = jnp.where(lax.broadcasted_iota(I32, (tq, tq), 1) <= lax.broadcasted_iota(I32, (tq, tq), 0),
                          s, -jnp.inf)
        s_buf[kj] = s
        return functools.reduce(jnp.maximum, lane_tiles(s))

    m_s[...] = score_tile(qi, True)

    def scores(kj, carry):
        m_s[...] = jnp.maximum(m_s[...], score_tile(kj, False))
        return carry

    lax.fori_loop(0, qi, scores, 0)
    m_s[...] = jnp.broadcast_to(jnp.max(m_s[...], axis=1, keepdims=True), m_s.shape)

    l_s[...] = jnp.zeros(l_s.shape, F32)

    def exps(kj, carry):
        m_rep = m_s[...]
        e_tiles = [jnp.exp(t - m_rep) for t in lane_tiles(s_buf[kj])]
        s_buf[kj] = jnp.concatenate(e_tiles, axis=1)
        l_s[...] += functools.reduce(jnp.add, e_tiles)
        return carry

    lax.fori_loop(0, qi + 1, exps, 0)
    l_s[...] = jnp.broadcast_to(1.0 / jnp.sum(l_s[...], axis=1, keepdims=True), l_s.shape)

    acc[...] = jnp.zeros(acc.shape, F32)

    def values(kj, carry):
        rows = pl.ds(pl.multiple_of(kj * tq, tq), tq)
        inv_rep = l_s[...]
        p = jnp.concatenate([(t * inv_rep).astype(BF16) for t in lane_tiles(s_buf[kj])], axis=1)
        acc[...] += jnp.dot(p, vb[rows, :], preferred_element_type=F32)
        return carry

    lax.fori_loop(0, qi + 1, values, 0)
    o_ref[...] = acc[...].astype(o_ref.dtype)


def _fox_attn_prompt(q, k, v, c_col, c_row, bsz, n, heads, tq):
    dh = q.shape[1] // heads
    nq = n // tq
    qmap = lambda b, h, i: (b * nq + i, h)
    kmap = lambda b, h, i: (b, h)
    return pl.pallas_call(
        functools.partial(_fox_attn_kernel, scale=dh ** -0.5, tq=tq),
        grid=(bsz, heads, nq),
        in_specs=[pl.BlockSpec((tq, dh), qmap), pl.BlockSpec((n, dh), kmap), pl.BlockSpec((n, dh), kmap),
                  pl.BlockSpec((1, tq, LANES), lambda b, h, i: (b, i, 0)),
                  pl.BlockSpec((1, 1, nq, tq), lambda b, h, i: (b, h, 0, 0))],
        out_specs=pl.BlockSpec((tq, dh), qmap),
        out_shape=jax.ShapeDtypeStruct((bsz * n, heads * dh), BF16),
        scratch_shapes=[pltpu.VMEM((n, dh), BF16), pltpu.VMEM((n, dh), BF16), pltpu.VMEM((nq, tq, tq), F32),
                        pltpu.VMEM((tq, LANES), F32), pltpu.VMEM((tq, LANES), F32), pltpu.VMEM((tq, dh), F32)],
        compiler_params=_params(("parallel", "parallel", "arbitrary")),
        name="fox_attn_prompt",
    )(q, k, v, c_col, c_row)


DECODE_PAGES_PER_STEP = 8


def _fox_decode_kernel(pt_ref, q_ref, kn_ref, vn_ref, f_ref, bf_ref, *rest, scale, heads, dh, page, pps):
    kp_refs, vp_refs, lf_refs = rest[:pps], rest[pps:2 * pps], rest[2 * pps:3 * pps]
    o_ref, lfo_ref, carry, m_s, l_s, acc = rest[3 * pps:]
    g = pl.program_id(1)
    sub_s = lax.broadcasted_iota(I32, (heads, LANES), 0)
    lane_s = lax.broadcasted_iota(I32, (heads, LANES), 1)

    @pl.when(g == 0)
    def _():
        lf_row = _log_sigmoid(f_ref[0] + bf_ref[...])
        lfo_ref[0] = lf_row
        lf_col = jnp.sum(jnp.where(lane_s == sub_s, lf_row, 0.0), axis=1, keepdims=True)
        carry[...] = jnp.broadcast_to(lf_col, carry.shape)
        s_new = jnp.sum(q_ref[0] * scale * kn_ref[0], axis=1, keepdims=True)
        m_s[...] = jnp.broadcast_to(s_new, m_s.shape)
        l_s[...] = jnp.ones(l_s.shape, F32)
        acc[...] = vn_ref[0]

    q3 = (q_ref[0] * scale)[None]
    r_i = lax.broadcasted_iota(I32, (page, page), 0)
    c_i = lax.broadcasted_iota(I32, (page, page), 1)
    newer = jnp.where(r_i > c_i, 1.0, 0.0).astype(BF16)
    tok3 = lax.broadcasted_iota(I32, (page, heads, dh), 0)
    lane3 = lax.broadcasted_iota(I32, (page, heads, dh), 2)
    on_diag = lane3 == tok3
    for kp_ref, vp_ref, lf_ref in zip(kp_refs, vp_refs, lf_refs):
        lf = lf_ref[0, 0]
        bias = carry[...] + _dot_exact_rhs(lf, newer)
        carry[...] = carry[...] + jnp.sum(lf, axis=1, keepdims=True)
        prod = kp_ref[0, 0] * q3 + jnp.where(on_diag, bias[None], 0.0)
        s3 = jnp.broadcast_to(jnp.sum(prod, axis=2, keepdims=True), prod.shape)
        m_prev = m_s[...]
        m_new = jnp.maximum(m_prev, jnp.max(s3, axis=0))
        alpha = jnp.exp(m_prev - m_new)
        p3 = jnp.exp(s3 - m_new[None])
        l_s[...] = alpha * l_s[...] + jnp.sum(p3, axis=0)
        acc[...] = alpha * acc[...] + jnp.sum(p3 * vp_ref[0, 0], axis=0)
        m_s[...] = m_new

    @pl.when(g == pl.num_programs(1) - 1)
    def _():
        o_ref[0] = (acc[...] / l_s[...]).astype(o_ref.dtype)


def _fox_decode(q, k_new, v_new, f_raw, b_f, k_pool, v_pool, lf_pool_t, page_table, layer, heads):
    n_seq, _, dh = q.shape
    page = k_pool.shape[2]
    n_pages = page_table.shape[1]
    pps = math.gcd(DECODE_PAGES_PER_STEP, n_pages)
    assert page == LANES and dh == LANES, "one vreg per cached token; token index doubles as a lane index"
    row = lambda b, g, pt: (b, 0, 0)

    def page_map(j, ndim):
        def index(b, g, pt):
            return (layer, pt[b * n_pages + (n_pages - 1 - (g * pps + j))]) + (0,) * (ndim - 2)
        return index

    kv_specs = [pl.BlockSpec((1, 1, page, heads, dh), page_map(j, 5)) for j in range(pps)]
    lf_specs = [pl.BlockSpec((1, 1, heads, page), page_map(j, 4)) for j in range(pps)]
    hblk = pl.BlockSpec((1, heads, dh), row)
    grid_spec = pltpu.PrefetchScalarGridSpec(
        num_scalar_prefetch=1,
        grid=(n_seq, n_pages // pps),
        in_specs=[hblk, hblk, hblk, pl.BlockSpec((1, 1, LANES), row),
                  pl.BlockSpec((1, LANES), lambda b, g, pt: (0, 0))] + kv_specs + kv_specs + lf_specs,
        out_specs=[hblk, pl.BlockSpec((1, 1, LANES), row)],
        scratch_shapes=[pltpu.VMEM((heads, LANES), F32), pltpu.VMEM((heads, dh), F32),
                        pltpu.VMEM((heads, dh), F32), pltpu.VMEM((heads, dh), F32)],
    )
    return pl.pallas_call(
        functools.partial(_fox_decode_kernel, scale=dh ** -0.5, heads=heads, dh=dh, page=page, pps=pps),
        grid_spec=grid_spec,
        out_shape=[jax.ShapeDtypeStruct((n_seq, heads, dh), BF16), jax.ShapeDtypeStruct((n_seq, 1, LANES), F32)],
        compiler_params=_params(("parallel", "arbitrary")),
        name="fox_decode",
    )(page_table.reshape(-1), q, k_new, v_new, f_raw, b_f, *([k_pool] * pps), *([v_pool] * pps),
      *([lf_pool_t] * pps))


def _fox_proj_spec(j, p):
    hd = p["cache_k_fox"].shape[3] * p["cache_k_fox"].shape[4]
    w = _pad_cols(p["w_in_fox"][j], 3 * hd + LANES).astype(BF16)
    return w, [(0, hd), (hd, hd), (2 * hd, hd), (3 * hd, LANES)]


def _fox_layer(proj, j, bsz, n, tp, p):
    q, k, v, f_raw = proj
    t_all = q.shape[0]
    heads = p["cache_k_fox"].shape[3]
    dh = p["cache_k_fox"].shape[4]
    hd = heads * dh
    b_f = _pad_cols(p["b_forget_fox"][j][None, :], LANES)
    lf_p, c_p = _fox_gate(f_raw[:tp].reshape(bsz, n, LANES), b_f, _tile(n, (512, 256, 128)))
    tq = _tile(n, (512, 256, 128))
    c_row = c_p[:, :, :heads].transpose(0, 2, 1).reshape(bsz, heads, n // tq, tq)
    o_p = _fox_attn_prompt(q, k, v, c_p, c_row, bsz, n, heads, tq)
    n_seq = t_all - tp
    s3 = lambda a: a[tp:].reshape(n_seq, heads, dh)
    lf_pool_t = p["cache_logf_fox"].transpose(0, 1, 3, 2)
    o_s, lf_s = _fox_decode(s3(q), s3(k), s3(v), f_raw[tp:].reshape(n_seq, 1, LANES), b_f,
                            p["cache_k_fox"], p["cache_v_fox"], lf_pool_t, p["page_table"], j, heads)
    o = jnp.concatenate([o_p, o_s.reshape(n_seq, hd)], axis=0)
    outs = dict(
        k_p=k[:tp].reshape(bsz, n, heads, dh), v_p=v[:tp].reshape(bsz, n, heads, dh),
        lf_p=lf_p[:, :, :heads],
        k_s=k[tp:].reshape(n_seq, 1, heads, dh), v_s=v[tp:].reshape(n_seq, 1, heads, dh),
        lf_s=lf_s[:, :, :heads])
    return o, outs


HALO = SUBLANES


def _causal_taps(hist, u, w_ref, first_tile, tl):
    width = w_ref.shape[0]
    u = _round_bf16(u)

    @pl.when(first_tile)
    def _():
        hist[0:HALO, :] = jnp.zeros((HALO, hist.shape[1]), F32)

    @pl.when(jnp.logical_not(first_tile))
    def _():
        hist[0:HALO, :] = hist[tl:tl + HALO, :]

    hist[HALO:HALO + tl, :] = u
    out = w_ref[width - 1:width, :] * u
    for k in range(width - 1):
        shift = width - 1 - k
        out = out + w_ref[k:k + 1, :] * hist[HALO - shift:HALO - shift + tl, :]
    return out


def _short_prompt_kernel(b_ref, c_ref, h_ref, w_ref, y_ref, st_ref, hist, *, tl):
    u = c_ref[...] * h_ref[...]
    conv = _causal_taps(hist, u, w_ref, pl.program_id(1) == 0, tl)
    y_ref[...] = (b_ref[...] * conv).astype(y_ref.dtype)
    keep = w_ref.shape[0] - 1
    st_ref[0] = u[tl - keep:tl, :]


def _short_prompt(bg, cg, hg, w_conv, bsz, n, t_all, tl):
    d = bg.shape[1]
    width = w_conv.shape[0]
    nl = n // tl
    row = lambda b, l: (b * nl + l, 0)
    return pl.pallas_call(
        functools.partial(_short_prompt_kernel, tl=tl),
        grid=(bsz, nl),
        in_specs=[pl.BlockSpec((tl, d), row)] * 3 + [pl.BlockSpec((width, d), lambda b, l: (0, 0))],
        out_specs=[pl.BlockSpec((tl, d), row), pl.BlockSpec((1, width - 1, d), lambda b, l: (b, 0, 0))],
        out_shape=[jax.ShapeDtypeStruct((t_all, d), BF16), jax.ShapeDtypeStruct((bsz, width - 1, d), F32)],
        scratch_shapes=[pltpu.VMEM((HALO + tl, d), F32)],
        compiler_params=_params(("parallel", "arbitrary")),
        name="short_prompt",
    )(bg, cg, hg, w_conv)


def _short_decode_kernel(b_ref, c_ref, h_ref, s_ref, w_ref, y_ref, u_ref):
    u = c_ref[...] * h_ref[...]
    width = w_ref.shape[0]
    conv = w_ref[width - 1:width, :] * _round_bf16(u)
    for k in range(width - 1):
        conv = conv + w_ref[k:k + 1, :] * _round_bf16(s_ref[k])
    y_ref[...] = (b_ref[...] * conv).astype(y_ref.dtype)
    u_ref[...] = u


def _short_decode(bg, cg, hg, state_t, w_conv):
    n_seq, d = bg.shape
    return pl.pallas_call(
        _short_decode_kernel,
        out_shape=[jax.ShapeDtypeStruct((n_seq, d), BF16), jax.ShapeDtypeStruct((n_seq, d), F32)],
        compiler_params=pltpu.CompilerParams(vmem_limit_bytes=VMEM_LIMIT),
        name="short_decode",
    )(bg, cg, hg, state_t, w_conv)


def _short_proj_spec(j, p):
    d = p["w_in_short"].shape[1]
    return p["w_in_short"][j].astype(BF16), [(0, d), (d, d), (2 * d, d)]


def _short_layer(proj, j, bsz, n, tp, p):
    bg, cg, hg = proj
    w_conv = p["w_conv_short"][j]
    y_p, st_p = _short_prompt(bg, cg, hg, w_conv, bsz, n, tp, _tile(n, (512, 256, 128)))
    state = p["state_conv_short"][j]
    y_s, u_s = _short_decode(bg[tp:], cg[tp:], hg[tp:], state.transpose(1, 0, 2), w_conv)
    y = jnp.concatenate([y_p, y_s], axis=0)
    st_s = jnp.concatenate([state[:, 1:], u_s[:, None, :]], axis=1)
    return y, dict(conv_p=st_p, conv_s=st_s)


def _gla_gate_kernel(r_ref, w_ref, b_ref, o_ref):
    z = jnp.dot(r_ref[...].astype(BF16), w_ref[...], preferred_element_type=F32) + b_ref[...]
    o_ref[...] = _log_sigmoid(z) * (1.0 / GLA_TAU)


def _gla_gate(r, w_gate, b_gate, tm):
    t = r.shape[0]
    dk = w_gate.shape[1]
    row = lambda i: (i, 0)
    full = lambda i: (0, 0)
    return pl.pallas_call(
        _gla_gate_kernel,
        grid=(t // tm,),
        in_specs=[pl.BlockSpec((tm, LANES), row), pl.BlockSpec((LANES, dk), full), pl.BlockSpec((1, dk), full)],
        out_specs=pl.BlockSpec((tm, dk), row),
        out_shape=jax.ShapeDtypeStruct((t, dk), F32),
        compiler_params=_params(("parallel",)),
        name="gla_gate",
    )(r, w_gate, b_gate)


def _rms_gate(o, g_norm, gate):
    on = o * lax.rsqrt(jnp.mean(o * o, axis=-1, keepdims=True) + RMS_EPS) * g_norm
    return on * _silu(gate)


def _gla_prompt_kernel(q_ref, k_ref, v_ref, g_ref, la_ref, gn_ref, o_ref, st_ref, state,
                       *, tl, heads, dk, dv, scale):
    l_idx = pl.program_id(1)

    @pl.when(l_idx == 0)
    def _():
        state[...] = jnp.zeros(state.shape, F32)

    cs = GLA_CHUNK
    row = lax.broadcasted_iota(I32, (cs, cs), 0)
    col = lax.broadcasted_iota(I32, (cs, cs), 1)
    causal = row >= col
    tri = jnp.where(causal, 1.0, 0.0).astype(BF16)
    for c in range(tl // cs):
        rows = slice(c * cs, (c + 1) * cs)
        for h in range(heads):
            kcols = slice(h * dk, (h + 1) * dk)
            vcols = slice(h * dv, (h + 1) * dv)
            b = _dot_exact_lhs(tri, la_ref[rows, kcols])
            b_last = b[cs - 1:cs, :]
            kh = k_ref[rows, kcols]
            q_dec = (q_ref[rows, kcols] * scale * jnp.exp(b)).astype(BF16)
            k_inv = (kh * jnp.exp(-b)).astype(BF16)
            k_end = (kh * jnp.exp(b_last - b)).astype(BF16)
            vh = v_ref[rows, vcols].astype(BF16)
            scores = jnp.where(causal, _nt(q_dec, k_inv), 0.0).astype(BF16)
            s_t = state[h]
            o = jnp.dot(scores, vh, preferred_element_type=F32) + _nt(q_dec, s_t.astype(BF16))
            state[h] = s_t * jnp.exp(b_last) + _tn(vh, k_end)
            o_ref[rows, vcols] = _rms_gate(o, gn_ref[...], g_ref[rows, vcols]).astype(o_ref.dtype)

    @pl.when(l_idx == pl.num_programs(1) - 1)
    def _():
        st_ref[0] = state[...]


def _gla_prompt(q, k, v, g, log_a, g_norm, bsz, n, heads, t_all, tl):
    dk = q.shape[1] // heads
    dv = v.shape[1] // heads
    nl = n // tl
    row = lambda b, l: (b * nl + l, 0)
    return pl.pallas_call(
        functools.partial(_gla_prompt_kernel, tl=tl, heads=heads, dk=dk, dv=dv, scale=dk ** -0.5),
        grid=(bsz, nl),
        in_specs=[pl.BlockSpec((tl, heads * dk), row), pl.BlockSpec((tl, heads * dk), row),
                  pl.BlockSpec((tl, heads * dv), row), pl.BlockSpec((tl, heads * dv), row),
                  pl.BlockSpec((tl, heads * dk), row), pl.BlockSpec((1, dv), lambda b, l: (0, 0))],
        out_specs=[pl.BlockSpec((tl, heads * dv), row),
                   pl.BlockSpec((1, heads, dv, dk), lambda b, l: (b, 0, 0, 0))],
        out_shape=[jax.ShapeDtypeStruct((t_all, heads * dv), BF16),
                   jax.ShapeDtypeStruct((bsz, heads, dv, dk), F32)],
        scratch_shapes=[pltpu.VMEM((heads, dv, dk), F32)],
        compiler_params=_params(("parallel", "arbitrary")),
        name="gla_prompt",
    )(q, k, v, g, log_a, g_norm)


def _gla_decode_kernel(s_ref, cp_ref, v_ref, g_ref, gn_ref, so_ref, o_ref, *, heads, dv, scale):
    for h in range(heads):
        cols = cp_ref[0, h]
        decay = jnp.exp(cols[:, 0:1])
        vcols = slice(h * dv, (h + 1) * dv)
        s_new = s_ref[0, h] * decay + cols[:, 1:2] * v_ref[0, :, vcols]
        so_ref[0, h] = s_new
        o = jnp.sum(_round_bf16(s_new) * _round_bf16(cols[:, 2:3] * scale), axis=0, keepdims=True)
        o_ref[0, :, vcols] = _rms_gate(o, gn_ref[...], g_ref[0, :, vcols]).astype(o_ref.dtype)


def _gla_decode(state, colpack, v, g, g_norm):
    n_seq, heads, dk, dv = state.shape
    sblk = pl.BlockSpec((1, heads, dk, dv), lambda b: (b, 0, 0, 0))
    vblk = pl.BlockSpec((1, 1, heads * dv), lambda b: (b, 0, 0))
    return pl.pallas_call(
        functools.partial(_gla_decode_kernel, heads=heads, dv=dv, scale=dk ** -0.5),
        grid=(n_seq,),
        in_specs=[sblk, pl.BlockSpec((1, heads, dk, SUBLANES), lambda b: (b, 0, 0, 0)), vblk, vblk,
                  pl.BlockSpec((1, dv), lambda b: (0, 0))],
        out_specs=[sblk, vblk],
        out_shape=[jax.ShapeDtypeStruct(state.shape, F32), jax.ShapeDtypeStruct((n_seq, 1, heads * dv), BF16)],
        compiler_params=_params(("parallel",)),
        name="gla_decode",
    )(state, colpack, v, g, g_norm)


def _gla_proj_spec(j, p):
    heads, dk, dv = p["state_gla"].shape[2:]
    hk, hv = heads * dk, heads * dv
    w = _pad_cols(p["w_in_gla"][j], 2 * hk + 2 * hv + LANES).astype(BF16)
    return w, [(0, hk), (hk, hk), (2 * hk, hv), (2 * hk + hv, hv), (2 * hk + 2 * hv, LANES)]


def _gla_layer(proj, j, bsz, n, tp, p):
    q, k, v, g, r = proj
    t_all = q.shape[0]
    n_seq, heads, dk, dv = p["state_gla"].shape[1:]
    hk, hv = heads * dk, heads * dv
    rank = p["w_gate2_gla"].shape[1]
    tm = _tile(t_all, (384, 256, 128))
    w_gate = jnp.pad(p["w_gate2_gla"][j], ((0, LANES - rank), (0, 0))).astype(BF16)
    log_a = _gla_gate(r, w_gate, p["b_gate_gla"][j][None, :], tm)
    g_norm = p["g_norm_gla"][j][None, :]
    o_p, st_p = _gla_prompt(q, k, v, g, log_a, g_norm, bsz, n, heads, tp, _tile(n, (256, 128, 64)))
    colpack = jnp.stack([log_a[tp:], k[tp:], q[tp:]], axis=-1).reshape(n_seq, heads, dk, 3)
    colpack = jnp.pad(colpack, ((0, 0), (0, 0), (0, 0), (0, SUBLANES - 3)))
    st_s, o_s = _gla_decode(p["state_gla"][j], colpack, v[tp:].reshape(n_seq, 1, hv),
                            g[tp:].reshape(n_seq, 1, hv), g_norm)
    o = jnp.concatenate([o_p, o_s.reshape(n_seq, hv)], axis=0)
    return o, dict(gla_p=st_p.transpose(0, 1, 3, 2), gla_s=st_s)


def _ssd_conv_prompt_kernel(x_ref, w_ref, b_ref, a_ref, st_ref, hist, *, tl):
    xbc = x_ref[...]
    conv = _causal_taps(hist, xbc, w_ref, pl.program_id(1) == 0, tl)
    a_ref[...] = _silu(conv + b_ref[...])
    keep = w_ref.shape[0] - 1
    st_ref[0] = xbc[tl - keep:tl, :]


def _ssd_conv_prompt(xbc, w_conv, b_conv, bsz, n, t_all, tl):
    ch = xbc.shape[1]
    width = w_conv.shape[0]
    nl = n // tl
    row = lambda b, l: (b * nl + l, 0)
    full = lambda b, l: (0, 0)
    return pl.pallas_call(
        functools.partial(_ssd_conv_prompt_kernel, tl=tl),
        grid=(bsz, nl),
        in_specs=[pl.BlockSpec((tl, ch), row), pl.BlockSpec((width, ch), full), pl.BlockSpec((1, ch), full)],
        out_specs=[pl.BlockSpec((tl, ch), row), pl.BlockSpec((1, width - 1, ch), lambda b, l: (b, 0, 0))],
        out_shape=[jax.ShapeDtypeStruct((t_all, ch), F32), jax.ShapeDtypeStruct((bsz, width - 1, ch), F32)],
        scratch_shapes=[pltpu.VMEM((HALO + tl, ch), F32)],
        compiler_params=_params(("parallel", "arbitrary")),
        name="ssd_conv_prompt",
    )(xbc, w_conv, b_conv)


def _ssd_conv_decode_kernel(x_ref, s_ref, w_ref, b_ref, a_ref):
    width = w_ref.shape[0]
    conv = w_ref[width - 1:width, :] * _round_bf16(x_ref[...])
    for k in range(width - 1):
        conv = conv + w_ref[k:k + 1, :] * _round_bf16(s_ref[k])
    a_ref[...] = _silu(conv + b_ref[...])


def _ssd_conv_decode(xbc, state_t, w_conv, b_conv):
    return pl.pallas_call(
        _ssd_conv_decode_kernel,
        out_shape=jax.ShapeDtypeStruct(xbc.shape, F32),
        compiler_params=pltpu.CompilerParams(vmem_limit_bytes=VMEM_LIMIT),
        name="ssd_conv_decode",
    )(xbc, state_t, w_conv, b_conv)


def _group_rms_gate(y, z, g_norm):
    yg = y * _silu(z)
    return yg * lax.rsqrt(jnp.mean(yg * yg, axis=-1, keepdims=True) + RMS_EPS) * g_norm


def _ssd_prompt_kernel(xs_ref, bm_ref, cm_ref, dt_ref, z_ref, dtb_ref, alog_ref, dsk_ref, gn_ref,
                       y_ref, st_ref, state, ybuf, *, q, heads, hd, ns, groups):
    l_idx = pl.program_id(1)

    @pl.when(l_idx == 0)
    def _():
        state[...] = jnp.zeros(state.shape, F32)

    pw = 2 * hd
    dt = _softplus(dt_ref[...] + dtb_ref[...])
    a = -jnp.exp(alog_ref[...])
    row = lax.broadcasted_iota(I32, (q, q), 0)
    col = lax.broadcasted_iota(I32, (q, q), 1)
    causal = row >= col
    tri = jnp.where(causal, 1.0, 0.0).astype(BF16)
    cum = _dot_exact_lhs(tri, dt * a)
    cum_t = cum.T
    e_cum = jnp.exp(cum)
    w_end = jnp.exp(cum[q - 1:q, :] - cum)
    dsk = dsk_ref[...]
    lo = lax.broadcasted_iota(I32, (q, pw), 1) < hd
    lo_rows = lax.broadcasted_iota(I32, (pw, ns), 0) < hd
    pairs_per_group = heads // groups // 2

    def pick(mat, h0):
        return jnp.where(lo, mat[:, h0:h0 + 1], mat[:, h0 + 1:h0 + 2])

    for g in range(groups):
        cmb = cm_ref[:, g * ns:(g + 1) * ns].astype(BF16)
        bmb = bm_ref[:, g * ns:(g + 1) * ns].astype(BF16)
        cb = _nt(cmb, bmb)
        for pp in range(pairs_per_group):
            pr = g * pairs_per_group + pp
            h0 = 2 * pr
            cols = slice(pr * pw, (pr + 1) * pw)
            x_pair = xs_ref[:, cols]
            xdt = x_pair * pick(dt, h0)

            def decay_mix(h):
                return (cb * jnp.exp(jnp.where(causal, cum[:, h:h + 1] - cum_t[h:h + 1, :], -jnp.inf))).astype(BF16)

            y = (jnp.dot(decay_mix(h0), jnp.where(lo, xdt, 0.0).astype(BF16), preferred_element_type=F32)
                 + jnp.dot(decay_mix(h0 + 1), jnp.where(lo, 0.0, xdt).astype(BF16), preferred_element_type=F32))
            s_pair = state[pr]
            y = y + _nt(cmb, s_pair.astype(BF16)) * pick(e_cum, h0)
            dec = jnp.where(lo_rows, jnp.exp(cum_t[h0:h0 + 1, q - 1:q]), jnp.exp(cum_t[h0 + 1:h0 + 2, q - 1:q]))
            state[pr] = s_pair * dec + _tn((xdt * pick(w_end, h0)).astype(BF16), bmb)
            ybuf[:, cols] = y + x_pair * pick(dsk, h0)

    gw = heads * hd // groups
    for g in range(groups):
        cols = slice(g * gw, (g + 1) * gw)
        y_ref[:, cols] = _group_rms_gate(ybuf[:, cols], z_ref[:, cols], gn_ref[:, cols]).astype(y_ref.dtype)

    @pl.when(l_idx == pl.num_programs(1) - 1)
    def _():
        st_ref[0] = state[...]


def _ssd_prompt(act, dt_raw, z, dt_bias, a_log, d_skip, g_norm, bsz, n, heads, hd, ns, groups, t_all, q):
    inner = heads * hd
    gn = groups * ns
    nl = n // q
    row = lambda b, l: (b * nl + l, 0)
    full = lambda b, l: (0, 0)
    n_pairs = heads // 2
    return pl.pallas_call(
        functools.partial(_ssd_prompt_kernel, q=q, heads=heads, hd=hd, ns=ns, groups=groups),
        grid=(bsz, nl),
        in_specs=[pl.BlockSpec((q, inner), row),
                  pl.BlockSpec((q, gn), lambda b, l: (b * nl + l, inner // gn)),
                  pl.BlockSpec((q, gn), lambda b, l: (b * nl + l, inner // gn + 1)),
                  pl.BlockSpec((q, LANES), row), pl.BlockSpec((q, inner), row),
                  pl.BlockSpec((1, LANES), full), pl.BlockSpec((1, LANES), full), pl.BlockSpec((1, LANES), full),
                  pl.BlockSpec((1, inner), full)],
        out_specs=[pl.BlockSpec((q, inner), row),
                   pl.BlockSpec((1, n_pairs, 2 * hd, ns), lambda b, l: (b, 0, 0, 0))],
        out_shape=[jax.ShapeDtypeStruct((t_all, inner), BF16),
                   jax.ShapeDtypeStruct((bsz, n_pairs, 2 * hd, ns), F32)],
        scratch_shapes=[pltpu.VMEM((n_pairs, 2 * hd, ns), F32), pltpu.VMEM((q, inner), F32)],
        compiler_params=_params(("parallel", "arbitrary")),
        name="ssd_prompt",
    )(act, act, act, dt_raw, z, dt_bias, a_log, d_skip, g_norm)


SSD_PACK_ROWS = 16


def _ssd_decode_kernel(s_ref, pack_ref, b_ref, c_ref, z_ref, gn_ref, so_ref, y_ref, *, n_pairs, ns, groups):
    pt = pack_ref[0].T
    pw = s_ref.shape[2]
    lane = lax.broadcasted_iota(I32, (pw, LANES), 1)
    pairs_per_group = n_pairs // groups
    quantity = lambda k: pt[:, k * SSD_PACK_ROWS:k * SSD_PACK_ROWS + n_pairs]
    x_all = quantity(0)
    dt_all = _softplus(quantity(1) + quantity(2))
    dec_all = jnp.exp(dt_all * -jnp.exp(quantity(3)))
    xdt_all = x_all * dt_all
    skip_all = x_all * quantity(4)
    y_cols = jnp.zeros((pw, LANES), F32)
    for pr in range(n_pairs):
        g = pr // pairs_per_group
        b_row = b_ref[0, :, g * ns:(g + 1) * ns]
        c_row = c_ref[0, :, g * ns:(g + 1) * ns]
        s_new = s_ref[0, pr] * dec_all[:, pr:pr + 1] + xdt_all[:, pr:pr + 1] * b_row
        so_ref[0, pr] = s_new
        y_col = jnp.sum(_round_bf16(s_new) * _round_bf16(c_row), axis=1, keepdims=True)
        y_cols = jnp.where(lane == pr, y_col, y_cols)
    y_cols = y_cols + jnp.pad(skip_all, ((0, 0), (0, LANES - n_pairs)))
    y_rows = y_cols.T[0:n_pairs, :]
    yg = y_rows * _silu(z_ref[0])
    sq = jnp.sum(yg * yg, axis=1, keepdims=True)
    out = jnp.zeros(yg.shape, F32)
    sub = lax.broadcasted_iota(I32, yg.shape, 0)
    sub1 = lax.broadcasted_iota(I32, sq.shape, 0)
    for g in range(groups):
        lo_r, hi_r = g * pairs_per_group, (g + 1) * pairs_per_group
        in_g1 = jnp.logical_and(sub1 >= lo_r, sub1 < hi_r)
        ms = jnp.sum(jnp.where(in_g1, sq, 0.0), axis=0, keepdims=True) / (pairs_per_group * pw)
        out = jnp.where(jnp.logical_and(sub >= lo_r, sub < hi_r), yg * lax.rsqrt(ms + RMS_EPS), out)
    y_ref[0] = (out * gn_ref[...]).astype(y_ref.dtype)


def _ssd_decode(state, pack, b_rows, c_rows, z, g_norm, groups):
    n_seq, n_pairs, pw, ns = state.shape
    sblk = pl.BlockSpec((1, n_pairs, pw, ns), lambda b: (b, 0, 0, 0))
    rblk = pl.BlockSpec((1, 1, groups * ns), lambda b: (b, 0, 0))
    zblk = pl.BlockSpec((1, n_pairs, pw), lambda b: (b, 0, 0))
    return pl.pallas_call(
        functools.partial(_ssd_decode_kernel, n_pairs=n_pairs, ns=ns, groups=groups),
        grid=(n_seq,),
        in_specs=[sblk, pl.BlockSpec((1, LANES, LANES), lambda b: (b, 0, 0)), rblk, rblk, zblk,
                  pl.BlockSpec((n_pairs, pw), lambda b: (0, 0))],
        out_specs=[sblk, zblk],
        out_shape=[jax.ShapeDtypeStruct(state.shape, F32), jax.ShapeDtypeStruct((n_seq, n_pairs, pw), BF16)],
        compiler_params=_params(("parallel",)),
        name="ssd_decode",
    )(state, pack, b_rows, c_rows, z, g_norm)


def _ssd_proj_spec(j, p):
    heads, hd, ns = p["state_ssm"].shape[2:]
    inner = heads * hd
    ch = inner + 2 * SSD_GROUPS * ns
    w = _pad_cols(p["w_in_ssd"][j], inner + ch + LANES).astype(BF16)
    return w, [(0, inner), (inner, ch), (inner + ch, LANES)]


def _ssd_layer(proj, j, bsz, n, tp, p):
    z, xbc, dt_raw = proj
    n_seq, heads, hd, ns = p["state_ssm"].shape[1:]
    groups = SSD_GROUPS
    inner = heads * hd
    gn = groups * ns
    ch = inner + 2 * gn
    n_pairs = heads // 2
    pw = 2 * hd
    assert pw == LANES and ns == LANES and n_pairs == SSD_PACK_ROWS and inner % gn == 0
    w_conv = p["w_conv_ssd"][j]
    b_conv = p["b_conv_ssd"][j][None, :]
    lane_row = lambda v: _pad_cols(v[None, :], LANES)
    g_norm = p["g_norm_ssd"][j]
    act, conv_p = _ssd_conv_prompt(xbc, w_conv, b_conv, bsz, n, tp, _tile(n, (256, 128)))
    y_p, ssm_p = _ssd_prompt(act, dt_raw, z, lane_row(p["dt_bias_ssd"][j]), lane_row(p["a_log_ssd"][j]),
                           lane_row(p["d_skip_ssd"][j]), g_norm[None, :], bsz, n, heads, hd, ns, groups,
                           tp, _tile(n, (128,)))
    conv_state = p["state_conv_ssd"][j]
    xbc_s = xbc[tp:]
    act_s = _ssd_conv_decode(xbc_s, conv_state.transpose(1, 0, 2), w_conv, b_conv)
    per_row = lambda v: jnp.broadcast_to(jnp.repeat(v, hd, axis=-1).reshape(-1, n_pairs, pw), (n_seq, n_pairs, pw))
    pack = jnp.concatenate([
        act_s[:, :inner].reshape(n_seq, n_pairs, pw),
        per_row(dt_raw[tp:, :heads]), per_row(p["dt_bias_ssd"][j][None, :]),
        per_row(p["a_log_ssd"][j][None, :]), per_row(p["d_skip_ssd"][j][None, :])], axis=1)
    pack = jnp.pad(pack, ((0, 0), (0, LANES - pack.shape[1]), (0, 0)))
    ssm_s, y_s = _ssd_decode(p["state_ssm"][j].reshape(n_seq, n_pairs, pw, ns), pack,
                             act_s[:, inner:inner + gn].reshape(n_seq, 1, gn),
                             act_s[:, inner + gn:].reshape(n_seq, 1, gn),
                             z[tp:].reshape(n_seq, n_pairs, pw), g_norm.reshape(n_pairs, pw), groups)
    y = jnp.concatenate([y_p, y_s.reshape(n_seq, inner)], axis=0)
    conv_s = jnp.concatenate([conv_state[:, 1:], xbc_s[:, None, :]], axis=1)
    return y, dict(ssm_p=ssm_p.reshape(bsz, heads, hd, ns), ssm_s=ssm_s.reshape(n_seq, heads, hd, ns),
                   conv_p=conv_p, conv_s=conv_s)


_LAYERS = (_fox_layer, _short_layer, _gla_layer, _ssd_layer)
_PROJ_SPECS = (_fox_proj_spec, _short_proj_spec, _gla_proj_spec, _ssd_proj_spec)
_OUT_ORDER = ("k", "v", "lf", "conv_short", "gla", "ssm", "conv_ssd")


def kernel(x_prompt, x_sample, cache_k_fox, cache_v_fox, cache_logf_fox, page_table, state_conv_short, state_gla, state_ssm, state_conv_ssd, w_in_fox, b_forget_fox, w_out_fox, w_in_short, w_conv_short, w_out_short, w_in_gla, w_gate2_gla, b_gate_gla, g_norm_gla, w_out_gla, w_in_ssd, w_conv_ssd, b_conv_ssd, dt_bias_ssd, a_log_ssd, d_skip_ssd, g_norm_ssd, w_out_ssd, ln_mix_g, ln_mix_b, ln_ffn_g, ln_ffn_b, w_router, b_router, w_gate_up, b_gate_up, w_down, b_down):
    p = dict(locals())
    bsz, n, d = x_prompt.shape
    n_seq, n_dec, _ = x_sample.shape
    assert n_dec == 1, "the sample group decodes one token per sequence"
    depth = ln_mix_g.shape[0]
    alpha = (2 * depth) ** 0.25
    tp = bsz * n
    x = jnp.concatenate([x_prompt.reshape(tp, d), x_sample.reshape(n_seq, d)], axis=0)
    w_outs = (w_out_fox, w_out_short, w_out_gla, w_out_ssd)
    acc = {name + sfx: [] for name in _OUT_ORDER for sfx in ("_p", "_s")}
    layer_of = lambda i: (i % len(_LAYERS), i // len(_LAYERS))
    w0, segs0 = _PROJ_SPECS[0](0, p)
    proj = _proj(x, w0, segs0, _tile(tp + n_seq, (384, 256, 128)))
    for i in range(depth):
        kind, j = layer_of(i)
        y_mix, outs = _LAYERS[kind](proj, j, bsz, n, tp, p)
        rename = {"conv_p": ("conv_short_p" if kind == 1 else "conv_ssd_p"),
                  "conv_s": ("conv_short_s" if kind == 1 else "conv_ssd_s")}
        for key, val in outs.items():
            acc[rename.get(key, key)].append(val)
        if i + 1 < depth:
            nkind, nj = layer_of(i + 1)
            x, *proj = _post_blocks(x, y_mix, w_outs[kind][j], i, alpha, p, _PROJ_SPECS[nkind](nj, p))
        else:
            x = _post_blocks(x, y_mix, w_outs[kind][j], i, alpha, p)
    stack = lambda name: jnp.stack(acc[name])
    return (x[:tp].reshape(bsz, n, d), x[tp:].reshape(n_seq, n_dec, d),
            *[stack(name + "_p") for name in _OUT_ORDER],
            *[stack(name + "_s") for name in _OUT_ORDER])
```

```python
import functools
import math

import jax
import jax.numpy as jnp
from jax import lax
from jax.experimental import pallas as pl
from jax.experimental.pallas import tpu as pltpu

F32 = jnp.float32
BF16 = jnp.bfloat16
I32 = jnp.int32

LANES = 128
SUBLANES = 8
VMEM_LIMIT = 56 * 1024 * 1024

LN_EPS = 1e-5
RMS_EPS = 1e-6
TOP_K = 4
SWIGLU_LIMIT = 7.0
SWIGLU_ALPHA = 1.702
GLA_TAU = 16.0
GLA_CHUNK = 64
SSD_GROUPS = 4
MOE_BLOCK = 256
FFN_UP_PHASES = 2
FFN_DOWN_PHASES = 2
GATHER_AHEAD = 2


def _params(semantics):
    return pltpu.CompilerParams(dimension_semantics=semantics, vmem_limit_bytes=VMEM_LIMIT)


def _tile(n, candidates):
    for c in candidates:
        if n % c == 0:
            return c
    raise ValueError(f"no tile for {n} among {candidates}")


def _round_up(n, m):
    return (n + m - 1) // m * m


def _log_sigmoid(z):
    return jnp.minimum(z, 0.0) - jnp.log(1.0 + jnp.exp(-jnp.abs(z)))


def _softplus(z):
    return jnp.maximum(z, 0.0) + jnp.log(1.0 + jnp.exp(-jnp.abs(z)))


def _silu(z):
    return z * jax.nn.sigmoid(z)


def _split3(x):
    hi = x.astype(BF16)
    r = x - hi.astype(F32)
    mid = r.astype(BF16)
    lo = (r - mid.astype(F32)).astype(BF16)
    return hi, mid, lo


def _dot_exact_lhs(a01, x):
    return sum(jnp.dot(a01, p, preferred_element_type=F32) for p in _split3(x))


def _dot_exact_rhs(x, a01):
    return sum(jnp.dot(p, a01, preferred_element_type=F32) for p in _split3(x))


def _round_bf16(x):
    return x.astype(BF16).astype(F32)


def _nt(a, b):
    return lax.dot_general(a, b, (((1,), (1,)), ((), ())), preferred_element_type=F32)


def _tn(a, b):
    return lax.dot_general(a, b, (((0,), (0,)), ((), ())), preferred_element_type=F32)


def _proj_kernel(x_ref, w_ref, *out_refs, segs, chunk):
    xb = x_ref[...].astype(BF16)
    for o_ref, (start, width) in zip(out_refs, segs):
        for c in range(0, width, chunk):
            cw = min(chunk, width - c)
            o_ref[:, c:c + cw] = jnp.dot(
                xb, w_ref[:, start + c:start + c + cw], preferred_element_type=F32).astype(o_ref.dtype)


def _proj(x, w, segs, tm):
    t, d = x.shape
    n = w.shape[1]
    assert all(s % LANES == 0 and wd % LANES == 0 for s, wd in segs)
    return pl.pallas_call(
        functools.partial(_proj_kernel, segs=tuple(segs), chunk=512),
        grid=(t // tm,),
        in_specs=[pl.BlockSpec((tm, d), lambda i: (i, 0)),
                  pl.BlockSpec((d, n), lambda i: (0, 0))],
        out_specs=[pl.BlockSpec((tm, wd), lambda i: (i, 0)) for _, wd in segs],
        out_shape=[jax.ShapeDtypeStruct((t, wd), F32) for _, wd in segs],
        compiler_params=_params(("parallel",)),
        name="proj",
    )(x, w)


def _pad_cols(w, n):
    return jnp.pad(w, ((0, 0), (0, n - w.shape[1])))


def _layer_norm_rows(z, g, b):
    mu = jnp.mean(z, axis=-1, keepdims=True)
    zc = z - mu
    var = jnp.mean(zc * zc, axis=-1, keepdims=True)
    return zc * lax.rsqrt(var + LN_EPS) * g + b


def _mix_ln_router_kernel(y_ref, w_ref, x_ref, g_ref, b_ref, wr_ref, br_ref,
                          x1_ref, topi_ref, gate_ref, rank_ref, cnt_ref, *, alpha, n_experts):
    tm = x_ref.shape[0]
    mix = jnp.dot(y_ref[...].astype(BF16), w_ref[...], preferred_element_type=F32)
    x1 = _layer_norm_rows(alpha * x_ref[...] + mix, g_ref[...], b_ref[...])
    x1_ref[...] = x1
    logits = jnp.dot(x1.astype(BF16), wr_ref[...], preferred_element_type=F32) + br_ref[...]
    lane = lax.broadcasted_iota(I32, logits.shape, 1)
    neg_inf = jnp.float32(-jnp.inf)
    cur = jnp.where(lane < n_experts, logits, neg_inf)
    topi = jnp.zeros(logits.shape, I32)
    chosen = jnp.zeros(logits.shape, F32)
    vals, ids = [], []
    for k in range(TOP_K):
        m = jnp.max(cur, axis=1, keepdims=True)
        idx = jnp.min(jnp.where(cur == m, lane, LANES), axis=1, keepdims=True)
        vals.append(m)
        ids.append(idx)
        topi = jnp.where(lane == k, idx, topi)
        chosen = jnp.where(lane == idx, 1.0, chosen)
        cur = jnp.where(lane == idx, neg_inf, cur)
    es = [jnp.exp(v - vals[0]) for v in vals]
    inv = 1.0 / sum(es)
    gate = jnp.zeros(logits.shape, F32)
    for k in range(TOP_K):
        gate = jnp.where(lane == k, es[k] * inv, gate)
    topi_ref[...] = topi
    gate_ref[...] = gate
    earlier = jnp.where(lax.broadcasted_iota(I32, (tm, tm), 0) > lax.broadcasted_iota(I32, (tm, tm), 1),
                        1.0, 0.0).astype(BF16)
    before = jnp.dot(earlier, chosen.astype(BF16), preferred_element_type=F32)
    rank = jnp.zeros(logits.shape, I32)
    for k in range(TOP_K):
        r_k = jnp.sum(jnp.where(lane == ids[k], before, 0.0), axis=1, keepdims=True)
        rank = jnp.where(lane == k, r_k.astype(I32), rank)
    rank_ref[...] = rank
    cnt_ref[0] = jnp.broadcast_to(jnp.sum(chosen, axis=0, keepdims=True), (SUBLANES, LANES)).astype(I32)


def _mix_ln_router(y, w_out, x, ln_g, ln_b, w_r, b_r, alpha, n_experts, tm):
    t, d = x.shape
    kdim = y.shape[1]
    row = lambda i: (i, 0)
    full = lambda i: (0, 0)
    return pl.pallas_call(
        functools.partial(_mix_ln_router_kernel, alpha=alpha, n_experts=n_experts),
        grid=(t // tm,),
        in_specs=[pl.BlockSpec((tm, kdim), row), pl.BlockSpec((kdim, d), full),
                  pl.BlockSpec((tm, d), row), pl.BlockSpec((1, d), full), pl.BlockSpec((1, d), full),
                  pl.BlockSpec((d, LANES), full), pl.BlockSpec((1, LANES), full)],
        out_specs=[pl.BlockSpec((tm, d), row), pl.BlockSpec((tm, LANES), row),
                   pl.BlockSpec((tm, LANES), row), pl.BlockSpec((tm, LANES), row),
                   pl.BlockSpec((1, SUBLANES, LANES), lambda i: (i, 0, 0))],
        out_shape=[jax.ShapeDtypeStruct((t, d), F32), jax.ShapeDtypeStruct((t, LANES), I32),
                   jax.ShapeDtypeStruct((t, LANES), F32), jax.ShapeDtypeStruct((t, LANES), I32),
                   jax.ShapeDtypeStruct((t // tm, SUBLANES, LANES), I32)],
        compiler_params=_params(("parallel",)),
        name="mix_ln_router",
    )(y, w_out, x, ln_g, ln_b, w_r, b_r)


def _route_tables(topi, rank, tile_cnt, n_experts, bm):
    t = topi.shape[0]
    n_pairs = t * TOP_K
    tm = t // tile_cnt.shape[0]
    e_ids = jnp.arange(n_experts, dtype=I32)
    cnt = tile_cnt[:, 0, :n_experts]
    tile_off = jnp.cumsum(cnt, axis=0) - cnt
    counts = jnp.sum(cnt, axis=0)
    starts = jnp.cumsum(counts) - counts
    padded = (counts + bm - 1) // bm * bm
    pad_ends = jnp.cumsum(padded)
    pad_starts = pad_ends - padded
    expert = topi[:, :TOP_K]
    base = jnp.repeat(pad_starts[None, :] + tile_off, tm, axis=0)
    hit = expert[:, :, None] == e_ids[None, None, :]
    pos = (jnp.sum(jnp.where(hit, base[:, None, :], 0), axis=2) + rank[:, :TOP_K]).reshape(-1).astype(I32)
    n_blocks = (n_pairs + bm - 1) // bm + n_experts + GATHER_AHEAD
    block_start = jnp.arange(n_blocks, dtype=I32) * bm
    block_expert = jnp.minimum(jnp.sum((pad_ends[None, :] <= block_start[:, None]).astype(I32), axis=1),
                               n_experts - 1).astype(I32)
    n_used = (pad_ends[-1] // bm).astype(I32).reshape(1)
    pair_bits = max(1, (n_pairs - 1).bit_length())
    assert (n_experts << pair_bits) < 2 ** 31
    keyed = jnp.sort(expert.reshape(-1) * (1 << pair_bits) + jnp.arange(n_pairs, dtype=I32))
    order = keyed & ((1 << pair_bits) - 1)
    e_row = jnp.repeat(block_expert, bm)
    off = jnp.arange(n_blocks * bm, dtype=I32) - pad_starts[e_row]
    src = jnp.clip(starts[e_row] + off, 0, n_pairs - 1)
    row_tok = jnp.where(off < counts[e_row], order[src] // TOP_K, 0).astype(I32)
    return block_expert, row_tok, n_used, pos, n_blocks


def _moe_ffn_kernel(be_ref, rt_ref, nu_ref, x_hbm, wgu_ref, bgu_ref, wd_ref, bd_ref,
                    y_ref, xbuf, sem, wgu_b, wd_b, xb, act, *, bm, d_ff):
    i = pl.program_id(0)
    n_used = nu_ref[0]
    n_slots = GATHER_AHEAD + 1
    slot = i % n_slots
    ahead_slot = (i + GATHER_AHEAD) % n_slots

    def row_copy(blk, r, dst_slot):
        tok = rt_ref[blk * bm + r]
        return pltpu.make_async_copy(x_hbm.at[pl.ds(tok, 1), :],
                                     xbuf.at[dst_slot, pl.ds(r, 1), :], sem.at[dst_slot])

    def wait_rows(dst_slot):
        pltpu.make_async_copy(x_hbm.at[pl.ds(0, bm), :], xbuf.at[dst_slot], sem.at[dst_slot]).wait()

    @pl.when(i == 0)
    def _():
        for blk in range(GATHER_AHEAD):
            def body(r, carry):
                row_copy(blk, r, blk).start()
                return carry
            lax.fori_loop(0, bm, body, 0)

    expert_changed = jnp.logical_or(i == 0, be_ref[jnp.maximum(i - 1, 0)] != be_ref[i])

    @pl.when(jnp.logical_and(i < n_used, expert_changed))
    def _():
        wgu_b[...] = wgu_ref[0, 0].astype(BF16)
        wd_b[...] = wd_ref[0, 0].astype(BF16)

    @pl.when(jnp.logical_and(i >= n_used, i < n_used + GATHER_AHEAD))
    def _():
        wait_rows(slot)

    @pl.when(i < n_used)
    def _():
        wait_rows(slot)
        xb[...] = xbuf[slot].astype(BF16)
        d = xb.shape[1]
        phases = [("up", c) for c in range(FFN_UP_PHASES)] + [("down", c) for c in range(FFN_DOWN_PHASES)]
        per_group = bm // (len(phases) - 1)
        uw, dw = d_ff // FFN_UP_PHASES, d // FFN_DOWN_PHASES
        zero = None
        for n, (kind, c) in enumerate(phases):
            if kind == "up":
                hg = (jnp.dot(xb[...], wgu_b[:, c * uw:(c + 1) * uw], preferred_element_type=F32)
                      + bgu_ref[0, 0, :, c * uw:(c + 1) * uw])
                hl = (jnp.dot(xb[...], wgu_b[:, d_ff + c * uw:d_ff + (c + 1) * uw], preferred_element_type=F32)
                      + bgu_ref[0, 0, :, d_ff + c * uw:d_ff + (c + 1) * uw])
                if zero is not None:
                    hg = hg + zero
                glu = jnp.minimum(hg, SWIGLU_LIMIT)
                lin = jnp.clip(hl, -SWIGLU_LIMIT, SWIGLU_LIMIT)
                act[:, c * uw:(c + 1) * uw] = (glu * jax.nn.sigmoid(SWIGLU_ALPHA * glu) * (lin + 1.0)).astype(BF16)
            else:
                y = (jnp.dot(act[...], wd_b[:, c * dw:(c + 1) * dw], preferred_element_type=F32)
                     + bd_ref[0, 0, :, c * dw:(c + 1) * dw])
                y_ref[:, c * dw:(c + 1) * dw] = y + zero
            if n < len(phases) - 1:
                for r in range(n * per_group, bm if n == len(phases) - 2 else (n + 1) * per_group):
                    row_copy(i + GATHER_AHEAD, r, ahead_slot).start()
                zero = jnp.minimum(jnp.abs(xbuf[slot, 0:1, 0:LANES]), 0.0)[:, 0:1]

    @pl.when(i >= n_used)
    def _():
        y_ref[...] = jnp.zeros(y_ref.shape, y_ref.dtype)


def _moe_ffn(x1, block_expert, row_tok, n_used, n_blocks, layer, w_gate_up, b_gate_up, w_down, b_down, bm):
    t, d = x1.shape
    depth, n_e, _, two_f = w_gate_up.shape
    d_ff = two_f // 2
    wmap = lambda i, be, rt, nu: (layer, be[i], 0, 0)
    grid_spec = pltpu.PrefetchScalarGridSpec(
        num_scalar_prefetch=3,
        grid=(n_blocks,),
        in_specs=[pl.BlockSpec(memory_space=pl.ANY),
                  pl.BlockSpec((1, 1, d, two_f), wmap),
                  pl.BlockSpec((1, 1, 1, two_f), wmap),
                  pl.BlockSpec((1, 1, d_ff, d), wmap),
                  pl.BlockSpec((1, 1, 1, d), wmap)],
        out_specs=pl.BlockSpec((bm, d), lambda i, be, rt, nu: (i, 0)),
        scratch_shapes=[pltpu.VMEM((GATHER_AHEAD + 1, bm, d), F32), pltpu.SemaphoreType.DMA((GATHER_AHEAD + 1,)),
                        pltpu.VMEM((d, two_f), BF16), pltpu.VMEM((d_ff, d), BF16), pltpu.VMEM((bm, d), BF16),
                        pltpu.VMEM((bm, d_ff), BF16)],
    )
    return pl.pallas_call(
        functools.partial(_moe_ffn_kernel, bm=bm, d_ff=d_ff),
        grid_spec=grid_spec,
        out_shape=jax.ShapeDtypeStruct((n_blocks * bm, d), F32),
        compiler_params=_params(("arbitrary",)),
        name="moe_ffn",
    )(block_expert, row_tok, n_used, x1, w_gate_up, b_gate_up.reshape(depth, n_e, 1, two_f),
      w_down, b_down.reshape(depth, n_e, 1, d))


def _moe_combine_kernel(pos_ref, y_hbm, gate_ref, x_ref, g_ref, b_ref, *rest, tm, alpha, segs, chunk):
    if segs:
        w_ref, o_ref, *proj_refs = rest[:-2]
    else:
        (o_ref,), proj_refs = rest[:-2], []
    buf, sem = rest[-2:]
    i = pl.program_id(0)
    nb = pl.num_programs(0)
    n_slots = GATHER_AHEAD + 1
    slot = i % n_slots
    ahead_slot = (i + GATHER_AHEAD) % n_slots
    n_rows = TOP_K * tm

    def row_copy(tile, j, dst_slot):
        r, k = j // TOP_K, j % TOP_K
        p = pos_ref[(tile * tm + r) * TOP_K + k]
        return pltpu.make_async_copy(y_hbm.at[pl.ds(p, 1), :],
                                     buf.at[dst_slot, pl.ds(k * tm + r, 1), :], sem.at[dst_slot])

    def wait_rows(dst_slot):
        pltpu.make_async_copy(y_hbm.at[pl.ds(0, n_rows), :], buf.at[dst_slot], sem.at[dst_slot]).wait()

    @pl.when(i == 0)
    def _():
        for tile in range(GATHER_AHEAD):
            def body(r, carry):
                for k in range(TOP_K):
                    row_copy(jnp.minimum(tile, nb - 1), r * TOP_K + k, tile).start()
                return carry
            lax.fori_loop(0, tm, body, 0)

    wait_rows(slot)
    gate = gate_ref[...]
    ffn = gate[:, 0:1] * buf[slot, pl.ds(0, tm), :]
    for k in range(1, TOP_K):
        ffn = ffn + gate[:, k:k + 1] * buf[slot, pl.ds(k * tm, tm), :]
    x2 = _layer_norm_rows(alpha * x_ref[...] + ffn, g_ref[...], b_ref[...])
    o_ref[...] = x2

    chunks = [(o, s, c, min(chunk, wd - c)) for o, (s, wd) in zip(proj_refs, segs) for c in range(0, wd, chunk)]
    n_groups = max(1, len(chunks))
    per_group = -(-n_rows // n_groups)
    next_tile = jnp.minimum(i + GATHER_AHEAD, nb - 1)
    xb = x2.astype(BF16)
    zero = None
    for n in range(n_groups):
        if chunks:
            o, s, c, cw = chunks[n]
            res = jnp.dot(xb, w_ref[:, s + c:s + c + cw], preferred_element_type=F32)
            o[:, c:c + cw] = res if zero is None else res + zero
        for j in range(n * per_group, min(n_rows, (n + 1) * per_group)):
            row_copy(next_tile, j, ahead_slot).start()
        zero = jnp.minimum(jnp.abs(buf[slot, 0:1, 0:LANES]), 0.0)[:, 0:1]

    @pl.when(i == nb - 1)
    def _():
        for extra in range(1, GATHER_AHEAD + 1):
            wait_rows((i + extra) % n_slots)


def _moe_combine(y_rows, pos, gates, x1, ln_g, ln_b, alpha, tm, w_next=None, segs=None):
    t, d = x1.shape
    segs = tuple(segs or ())
    row = lambda i, p: (i, 0)
    full = lambda i, p: (0, 0)
    in_specs = [pl.BlockSpec(memory_space=pl.ANY), pl.BlockSpec((tm, LANES), row),
                pl.BlockSpec((tm, d), row), pl.BlockSpec((1, d), full), pl.BlockSpec((1, d), full)]
    args = [pos, y_rows, gates, x1, ln_g, ln_b]
    if segs:
        assert all(s % LANES == 0 and wd % LANES == 0 for s, wd in segs)
        in_specs.append(pl.BlockSpec(w_next.shape, full))
        args.append(w_next)
    grid_spec = pltpu.PrefetchScalarGridSpec(
        num_scalar_prefetch=1,
        grid=(t // tm,),
        in_specs=in_specs,
        out_specs=[pl.BlockSpec((tm, d), row)] + [pl.BlockSpec((tm, wd), row) for _, wd in segs],
        scratch_shapes=[pltpu.VMEM((GATHER_AHEAD + 1, TOP_K * tm, d), F32),
                        pltpu.SemaphoreType.DMA((GATHER_AHEAD + 1,))],
    )
    outs = pl.pallas_call(
        functools.partial(_moe_combine_kernel, tm=tm, alpha=alpha, segs=segs, chunk=512),
        grid_spec=grid_spec,
        out_shape=[jax.ShapeDtypeStruct((t, d), F32)] + [jax.ShapeDtypeStruct((t, wd), F32) for _, wd in segs],
        compiler_params=_params(("arbitrary",)),
        name="moe_combine",
    )(*args)
    return tuple(outs) if segs else outs[0]


def _post_blocks(x, y_mix, w_out, i, alpha, p, next_proj=None):
    n_experts = p["w_router"].shape[-1]
    d = x.shape[1]
    tm = _tile(x.shape[0], (384, 256, 128))
    w_r = _pad_cols(p["w_router"][i], LANES).astype(BF16)
    b_r = _pad_cols(p["b_router"][i][None, :].astype(F32), LANES)
    x1, topi, gates, rank, tile_cnt = _mix_ln_router(
        y_mix, w_out.astype(BF16), x, p["ln_mix_g"][i].reshape(1, d), p["ln_mix_b"][i].reshape(1, d),
        w_r, b_r, alpha, n_experts, tm)
    block_expert, row_tok, n_used, pos, n_blocks = _route_tables(topi, rank, tile_cnt, n_experts, MOE_BLOCK)
    y_rows = _moe_ffn(x1, block_expert, row_tok, n_used, n_blocks, i, p["w_gate_up"], p["b_gate_up"],
                      p["w_down"], p["b_down"], MOE_BLOCK)
    w_next, segs = next_proj if next_proj is not None else (None, None)
    return _moe_combine(y_rows, pos, gates, x1, p["ln_ffn_g"][i].reshape(1, d),
                        p["ln_ffn_b"][i].reshape(1, d), alpha, _tile(x.shape[0], (128,)), w_next, segs)


def _fox_gate_kernel(f_ref, bf_ref, lf_ref, c_ref, carry, *, tl):
    @pl.when(pl.program_id(1) == 0)
    def _():
        carry[...] = jnp.zeros(carry.shape, F32)

    lf = _log_sigmoid(f_ref[0] + bf_ref[...])
    lf_ref[0] = lf
    row = lax.broadcasted_iota(I32, (tl, tl), 0)
    col = lax.broadcasted_iota(I32, (tl, tl), 1)
    tri = jnp.where(row >= col, 1.0, 0.0).astype(BF16)
    c = _dot_exact_lhs(tri, lf) + carry[0:1, :]
    c_ref[0] = c
    carry[0:1, :] = c[tl - 1:tl, :]


def _fox_gate(f_raw, b_f, tl):
    bsz, n, _ = f_raw.shape
    blk = pl.BlockSpec((1, tl, LANES), lambda b, l: (b, l, 0))
    return pl.pallas_call(
        functools.partial(_fox_gate_kernel, tl=tl),
        grid=(bsz, n // tl),
        in_specs=[blk, pl.BlockSpec((1, LANES), lambda b, l: (0, 0))],
        out_specs=[blk, blk],
        out_shape=[jax.ShapeDtypeStruct(f_raw.shape, F32)] * 2,
        scratch_shapes=[pltpu.VMEM((SUBLANES, LANES), F32)],
        compiler_params=_params(("parallel", "arbitrary")),
        name="fox_gate",
    )(f_raw, b_f)


def _fox_attn_kernel(q_ref, k_ref, v_ref, cc_ref, cr_ref, o_ref, kb, vb, s_buf, m_s, l_s, acc, *, scale, tq):
    head = pl.program_id(1)
    qi = pl.program_id(2)

    @pl.when(qi == 0)
    def _():
        kb[...] = k_ref[...].astype(BF16)
        vb[...] = v_ref[...].astype(BF16)

    qs = (q_ref[...] * scale).astype(BF16)
    lane = lax.broadcasted_iota(I32, (tq, LANES), 1)
    cq = jnp.sum(jnp.where(lane == head, cc_ref[0], 0.0), axis=1, keepdims=True)
    n_lt = tq // LANES
    lane_tiles = lambda a: [a[:, c * LANES:(c + 1) * LANES] for c in range(n_lt)]

    def score_tile(kj, masked):
        rows = pl.ds(pl.multiple_of(kj * tq, tq), tq)
        s = _nt(qs, kb[rows, :]) + (cq - cr_ref[0, 0, pl.ds(kj, 1), :])
        if masked:
            s = jnp.where(lax.broadcasted_iota(I32, (tq, tq), 1) <= lax.broadcasted_iota(I32, (tq, tq), 0),
                          s, -jnp.inf)
        s_buf[kj] = s
        return functools.reduce(jnp.maximum, lane_tiles(s))

    m_s[...] = score_tile(qi, True)

    def scores(kj, carry):
        m_s[...] = jnp.maximum(m_s[...], score_tile(kj, False))
        return carry

    lax.fori_loop(0, qi, scores, 0)
    m_s[...] = jnp.broadcast_to(jnp.max(m_s[...], axis=1, keepdims=True), m_s.shape)

    l_s[...] = jnp.zeros(l_s.shape, F32)

    def exps(kj, carry):
        m_rep = m_s[...]
        e_tiles = [jnp.exp(t - m_rep) for t in lane_tiles(s_buf[kj])]
        s_buf[kj] = jnp.concatenate(e_tiles, axis=1)
        l_s[...] += functools.reduce(jnp.add, e_tiles)
        return carry

    lax.fori_loop(0, qi + 1, exps, 0)
    l_s[...] = jnp.broadcast_to(1.0 / jnp.sum(l_s[...], axis=1, keepdims=True), l_s.shape)

    acc[...] = jnp.zeros(acc.shape, F32)

    def values(kj, carry):
        rows = pl.ds(pl.multiple_of(kj * tq, tq), tq)
        inv_rep = l_s[...]
        p = jnp.concatenate([(t * inv_rep).astype(BF16) for t in lane_tiles(s_buf[kj])], axis=1)
        acc[...] += jnp.dot(p, vb[rows, :], preferred_element_type=F32)
        return carry

    lax.fori_loop(0, qi + 1, values, 0)
    o_ref[...] = acc[...].astype(o_ref.dtype)


def _fox_attn_prompt(q, k, v, c_col, c_row, bsz, n, heads, tq):
    dh = q.shape[1] // heads
    nq = n // tq
    qmap = lambda b, h, i: (b * nq + i, h)
    kmap = lambda b, h, i: (b, h)
    return pl.pallas_call(
        functools.partial(_fox_attn_kernel, scale=dh ** -0.5, tq=tq),
        grid=(bsz, heads, nq),
        in_specs=[pl.BlockSpec((tq, dh), qmap), pl.BlockSpec((n, dh), kmap), pl.BlockSpec((n, dh), kmap),
                  pl.BlockSpec((1, tq, LANES), lambda b, h, i: (b, i, 0)),
                  pl.BlockSpec((1, 1, nq, tq), lambda b, h, i: (b, h, 0, 0))],
        out_specs=pl.BlockSpec((tq, dh), qmap),
        out_shape=jax.ShapeDtypeStruct((bsz * n, heads * dh), BF16),
        scratch_shapes=[pltpu.VMEM((n, dh), BF16), pltpu.VMEM((n, dh), BF16), pltpu.VMEM((nq, tq, tq), F32),
                        pltpu.VMEM((tq, LANES), F32), pltpu.VMEM((tq, LANES), F32), pltpu.VMEM((tq, dh), F32)],
        compiler_params=_params(("parallel", "parallel", "arbitrary")),
        name="fox_attn_prompt",
    )(q, k, v, c_col, c_row)


DECODE_PAGES_PER_STEP = 16


def _fox_decode_kernel(pt_ref, q_ref, kn_ref, vn_ref, f_ref, bf_ref, *rest, scale, heads, dh, page, pps):
    kp_refs, vp_refs, lf_refs = rest[:pps], rest[pps:2 * pps], rest[2 * pps:3 * pps]
    o_ref, lfo_ref, carry, m_s, l_s, acc = rest[3 * pps:]
    g = pl.program_id(1)
    sub_s = lax.broadcasted_iota(I32, (heads, LANES), 0)
    lane_s = lax.broadcasted_iota(I32, (heads, LANES), 1)

    @pl.when(g == 0)
    def _():
        lf_row = _log_sigmoid(f_ref[0] + bf_ref[...])
        lfo_ref[0] = lf_row
        lf_col = jnp.sum(jnp.where(lane_s == sub_s, lf_row, 0.0), axis=1, keepdims=True)
        carry[...] = jnp.broadcast_to(lf_col, carry.shape)
        s_new = jnp.sum(q_ref[0] * scale * kn_ref[0], axis=1, keepdims=True)
        m_s[...] = jnp.broadcast_to(s_new, m_s.shape)
        l_s[...] = jnp.ones(l_s.shape, F32)
        acc[...] = vn_ref[0]

    q3 = (q_ref[0] * scale)[None]
    r_i = lax.broadcasted_iota(I32, (page, page), 0)
    c_i = lax.broadcasted_iota(I32, (page, page), 1)
    newer = jnp.where(r_i > c_i, 1.0, 0.0).astype(BF16)
    tok3 = lax.broadcasted_iota(I32, (page, heads, dh), 0)
    lane3 = lax.broadcasted_iota(I32, (page, heads, dh), 2)
    on_diag = lane3 == tok3
    for kp_ref, vp_ref, lf_ref in zip(kp_refs, vp_refs, lf_refs):
        lf = lf_ref[0, 0]
        bias = carry[...] + _dot_exact_rhs(lf, newer)
        carry[...] = carry[...] + jnp.sum(lf, axis=1, keepdims=True)
        prod = kp_ref[0, 0] * q3 + jnp.where(on_diag, bias[None], 0.0)
        s3 = jnp.broadcast_to(jnp.sum(prod, axis=2, keepdims=True), prod.shape)
        m_prev = m_s[...]
        m_new = jnp.maximum(m_prev, jnp.max(s3, axis=0))
        alpha = jnp.exp(m_prev - m_new)
        p3 = jnp.exp(s3 - m_new[None])
        l_s[...] = alpha * l_s[...] + jnp.sum(p3, axis=0)
        acc[...] = alpha * acc[...] + jnp.sum(p3 * vp_ref[0, 0], axis=0)
        m_s[...] = m_new

    @pl.when(g == pl.num_programs(1) - 1)
    def _():
        o_ref[0] = (acc[...] / l_s[...]).astype(o_ref.dtype)


def _fox_decode(q, k_new, v_new, f_raw, b_f, k_pool, v_pool, lf_pool_t, page_table, layer, heads):
    n_seq, _, dh = q.shape
    page = k_pool.shape[2]
    n_pages = page_table.shape[1]
    pps = math.gcd(DECODE_PAGES_PER_STEP, n_pages)
    assert page == LANES and dh == LANES, "one vreg per cached token; token index doubles as a lane index"
    row = lambda b, g, pt: (b, 0, 0)

    def page_map(j, ndim):
        def index(b, g, pt):
            return (layer, pt[b * n_pages + (n_pages - 1 - (g * pps + j))]) + (0,) * (ndim - 2)
        return index

    kv_specs = [pl.BlockSpec((1, 1, page, heads, dh), page_map(j, 5)) for j in range(pps)]
    lf_specs = [pl.BlockSpec((1, 1, heads, page), page_map(j, 4)) for j in range(pps)]
    hblk = pl.BlockSpec((1, heads, dh), row)
    grid_spec = pltpu.PrefetchScalarGridSpec(
        num_scalar_prefetch=1,
        grid=(n_seq, n_pages // pps),
        in_specs=[hblk, hblk, hblk, pl.BlockSpec((1, 1, LANES), row),
                  pl.BlockSpec((1, LANES), lambda b, g, pt: (0, 0))] + kv_specs + kv_specs + lf_specs,
        out_specs=[hblk, pl.BlockSpec((1, 1, LANES), row)],
        scratch_shapes=[pltpu.VMEM((heads, LANES), F32), pltpu.VMEM((heads, dh), F32),
                        pltpu.VMEM((heads, dh), F32), pltpu.VMEM((heads, dh), F32)],
    )
    return pl.pallas_call(
        functools.partial(_fox_decode_kernel, scale=dh ** -0.5, heads=heads, dh=dh, page=page, pps=pps),
        grid_spec=grid_spec,
        out_shape=[jax.ShapeDtypeStruct((n_seq, heads, dh), BF16), jax.ShapeDtypeStruct((n_seq, 1, LANES), F32)],
        compiler_params=_params(("parallel", "arbitrary")),
        name="fox_decode",
    )(page_table.reshape(-1), q, k_new, v_new, f_raw, b_f, *([k_pool] * pps), *([v_pool] * pps),
      *([lf_pool_t] * pps))


def _fox_proj_spec(j, p):
    hd = p["cache_k_fox"].shape[3] * p["cache_k_fox"].shape[4]
    w = _pad_cols(p["w_in_fox"][j], 3 * hd + LANES).astype(BF16)
    return w, [(0, hd), (hd, hd), (2 * hd, hd), (3 * hd, LANES)]


def _fox_layer(proj, j, bsz, n, tp, p):
    q, k, v, f_raw = proj
    t_all = q.shape[0]
    heads = p["cache_k_fox"].shape[3]
    dh = p["cache_k_fox"].shape[4]
    hd = heads * dh
    b_f = _pad_cols(p["b_forget_fox"][j][None, :], LANES)
    lf_p, c_p = _fox_gate(f_raw[:tp].reshape(bsz, n, LANES), b_f, _tile(n, (512, 256, 128)))
    tq = _tile(n, (512, 256, 128))
    c_row = c_p[:, :, :heads].transpose(0, 2, 1).reshape(bsz, heads, n // tq, tq)
    o_p = _fox_attn_prompt(q, k, v, c_p, c_row, bsz, n, heads, tq)
    n_seq = t_all - tp
    s3 = lambda a: a[tp:].reshape(n_seq, heads, dh)
    lf_pool_t = p["cache_logf_fox"].transpose(0, 1, 3, 2)
    o_s, lf_s = _fox_decode(s3(q), s3(k), s3(v), f_raw[tp:].reshape(n_seq, 1, LANES), b_f,
                            p["cache_k_fox"], p["cache_v_fox"], lf_pool_t, p["page_table"], j, heads)
    o = jnp.concatenate([o_p, o_s.reshape(n_seq, hd)], axis=0)
    outs = dict(
        k_p=k[:tp].reshape(bsz, n, heads, dh), v_p=v[:tp].reshape(bsz, n, heads, dh),
        lf_p=lf_p[:, :, :heads],
        k_s=k[tp:].reshape(n_seq, 1, heads, dh), v_s=v[tp:].reshape(n_seq, 1, heads, dh),
        lf_s=lf_s[:, :, :heads])
    return o, outs


HALO = SUBLANES


def _causal_taps(hist, u, w_ref, first_tile, tl):
    width = w_ref.shape[0]
    u = _round_bf16(u)

    @pl.when(first_tile)
    def _():
        hist[0:HALO, :] = jnp.zeros((HALO, hist.shape[1]), F32)

    @pl.when(jnp.logical_not(first_tile))
    def _():
        hist[0:HALO, :] = hist[tl:tl + HALO, :]

    hist[HALO:HALO + tl, :] = u
    out = w_ref[width - 1:width, :] * u
    for k in range(width - 1):
        shift = width - 1 - k
        out = out + w_ref[k:k + 1, :] * hist[HALO - shift:HALO - shift + tl, :]
    return out


def _short_prompt_kernel(b_ref, c_ref, h_ref, w_ref, y_ref, st_ref, hist, *, tl):
    u = c_ref[...] * h_ref[...]
    conv = _causal_taps(hist, u, w_ref, pl.program_id(1) == 0, tl)
    y_ref[...] = (b_ref[...] * conv).astype(y_ref.dtype)
    keep = w_ref.shape[0] - 1
    st_ref[0] = u[tl - keep:tl, :]


def _short_prompt(bg, cg, hg, w_conv, bsz, n, t_all, tl):
    d = bg.shape[1]
    width = w_conv.shape[0]
    nl = n // tl
    row = lambda b, l: (b * nl + l, 0)
    return pl.pallas_call(
        functools.partial(_short_prompt_kernel, tl=tl),
        grid=(bsz, nl),
        in_specs=[pl.BlockSpec((tl, d), row)] * 3 + [pl.BlockSpec((width, d), lambda b, l: (0, 0))],
        out_specs=[pl.BlockSpec((tl, d), row), pl.BlockSpec((1, width - 1, d), lambda b, l: (b, 0, 0))],
        out_shape=[jax.ShapeDtypeStruct((t_all, d), BF16), jax.ShapeDtypeStruct((bsz, width - 1, d), F32)],
        scratch_shapes=[pltpu.VMEM((HALO + tl, d), F32)],
        compiler_params=_params(("parallel", "arbitrary")),
        name="short_prompt",
    )(bg, cg, hg, w_conv)


def _short_decode_kernel(b_ref, c_ref, h_ref, s_ref, w_ref, y_ref, u_ref):
    u = c_ref[...] * h_ref[...]
    width = w_ref.shape[0]
    conv = w_ref[width - 1:width, :] * _round_bf16(u)
    for k in range(width - 1):
        conv = conv + w_ref[k:k + 1, :] * _round_bf16(s_ref[k])
    y_ref[...] = (b_ref[...] * conv).astype(y_ref.dtype)
    u_ref[...] = u


def _short_decode(bg, cg, hg, state_t, w_conv):
    n_seq, d = bg.shape
    return pl.pallas_call(
        _short_decode_kernel,
        out_shape=[jax.ShapeDtypeStruct((n_seq, d), BF16), jax.ShapeDtypeStruct((n_seq, d), F32)],
        compiler_params=pltpu.CompilerParams(vmem_limit_bytes=VMEM_LIMIT),
        name="short_decode",
    )(bg, cg, hg, state_t, w_conv)


def _short_proj_spec(j, p):
    d = p["w_in_short"].shape[1]
    return p["w_in_short"][j].astype(BF16), [(0, d), (d, d), (2 * d, d)]


def _short_layer(proj, j, bsz, n, tp, p):
    bg, cg, hg = proj
    w_conv = p["w_conv_short"][j]
    y_p, st_p = _short_prompt(bg, cg, hg, w_conv, bsz, n, tp, _tile(n, (512, 256, 128)))
    state = p["state_conv_short"][j]
    y_s, u_s = _short_decode(bg[tp:], cg[tp:], hg[tp:], state.transpose(1, 0, 2), w_conv)
    y = jnp.concatenate([y_p, y_s], axis=0)
    st_s = jnp.concatenate([state[:, 1:], u_s[:, None, :]], axis=1)
    return y, dict(conv_p=st_p, conv_s=st_s)


def _gla_gate_kernel(r_ref, w_ref, b_ref, o_ref):
    z = jnp.dot(r_ref[...].astype(BF16), w_ref[...], preferred_element_type=F32) + b_ref[...]
    o_ref[...] = _log_sigmoid(z) * (1.0 / GLA_TAU)


def _gla_gate(r, w_gate, b_gate, tm):
    t = r.shape[0]
    dk = w_gate.shape[1]
    row = lambda i: (i, 0)
    full = lambda i: (0, 0)
    return pl.pallas_call(
        _gla_gate_kernel,
        grid=(t // tm,),
        in_specs=[pl.BlockSpec((tm, LANES), row), pl.BlockSpec((LANES, dk), full), pl.BlockSpec((1, dk), full)],
        out_specs=pl.BlockSpec((tm, dk), row),
        out_shape=jax.ShapeDtypeStruct((t, dk), F32),
        compiler_params=_params(("parallel",)),
        name="gla_gate",
    )(r, w_gate, b_gate)


def _rms_gate(o, g_norm, gate):
    on = o * lax.rsqrt(jnp.mean(o * o, axis=-1, keepdims=True) + RMS_EPS) * g_norm
    return on * _silu(gate)


def _gla_prompt_kernel(q_ref, k_ref, v_ref, g_ref, la_ref, gn_ref, o_ref, st_ref, state,
                       *, tl, heads, dk, dv, scale):
    l_idx = pl.program_id(1)

    @pl.when(l_idx == 0)
    def _():
        state[...] = jnp.zeros(state.shape, F32)

    cs = GLA_CHUNK
    row = lax.broadcasted_iota(I32, (cs, cs), 0)
    col = lax.broadcasted_iota(I32, (cs, cs), 1)
    causal = row >= col
    tri = jnp.where(causal, 1.0, 0.0).astype(BF16)
    for c in range(tl // cs):
        rows = slice(c * cs, (c + 1) * cs)
        for h in range(heads):
            kcols = slice(h * dk, (h + 1) * dk)
            vcols = slice(h * dv, (h + 1) * dv)
            b = _dot_exact_lhs(tri, la_ref[rows, kcols])
            b_last = b[cs - 1:cs, :]
            kh = k_ref[rows, kcols]
            q_dec = (q_ref[rows, kcols] * scale * jnp.exp(b)).astype(BF16)
            k_inv = (kh * jnp.exp(-b)).astype(BF16)
            k_end = (kh * jnp.exp(b_last - b)).astype(BF16)
            vh = v_ref[rows, vcols].astype(BF16)
            scores = jnp.where(causal, _nt(q_dec, k_inv), 0.0).astype(BF16)
            s_t = state[h]
            o = jnp.dot(scores, vh, preferred_element_type=F32) + _nt(q_dec, s_t.astype(BF16))
            state[h] = s_t * jnp.exp(b_last) + _tn(vh, k_end)
            o_ref[rows, vcols] = _rms_gate(o, gn_ref[...], g_ref[rows, vcols]).astype(o_ref.dtype)

    @pl.when(l_idx == pl.num_programs(1) - 1)
    def _():
        st_ref[0] = state[...]


def _gla_prompt(q, k, v, g, log_a, g_norm, bsz, n, heads, t_all, tl):
    dk = q.shape[1] // heads
    dv = v.shape[1] // heads
    nl = n // tl
    row = lambda b, l: (b * nl + l, 0)
    return pl.pallas_call(
        functools.partial(_gla_prompt_kernel, tl=tl, heads=heads, dk=dk, dv=dv, scale=dk ** -0.5),
        grid=(bsz, nl),
        in_specs=[pl.BlockSpec((tl, heads * dk), row), pl.BlockSpec((tl, heads * dk), row),
                  pl.BlockSpec((tl, heads * dv), row), pl.BlockSpec((tl, heads * dv), row),
                  pl.BlockSpec((tl, heads * dk), row), pl.BlockSpec((1, dv), lambda b, l: (0, 0))],
        out_specs=[pl.BlockSpec((tl, heads * dv), row),
                   pl.BlockSpec((1, heads, dv, dk), lambda b, l: (b, 0, 0, 0))],
        out_shape=[jax.ShapeDtypeStruct((t_all, heads * dv), BF16),
                   jax.ShapeDtypeStruct((bsz, heads, dv, dk), F32)],
        scratch_shapes=[pltpu.VMEM((heads, dv, dk), F32)],
        compiler_params=_params(("parallel", "arbitrary")),
        name="gla_prompt",
    )(q, k, v, g, log_a, g_norm)


def _gla_decode_kernel(s_ref, cp_ref, v_ref, g_ref, gn_ref, so_ref, o_ref, *, heads, dv, scale):
    for h in range(heads):
        cols = cp_ref[0, h]
        decay = jnp.exp(cols[:, 0:1])
        vcols = slice(h * dv, (h + 1) * dv)
        s_new = s_ref[0, h] * decay + cols[:, 1:2] * v_ref[0, :, vcols]
        so_ref[0, h] = s_new
        o = jnp.sum(_round_bf16(s_new) * _round_bf16(cols[:, 2:3] * scale), axis=0, keepdims=True)
        o_ref[0, :, vcols] = _rms_gate(o, gn_ref[...], g_ref[0, :, vcols]).astype(o_ref.dtype)


def _gla_decode(state, colpack, v, g, g_norm):
    n_seq, heads, dk, dv = state.shape
    sblk = pl.BlockSpec((1, heads, dk, dv), lambda b: (b, 0, 0, 0))
    vblk = pl.BlockSpec((1, 1, heads * dv), lambda b: (b, 0, 0))
    return pl.pallas_call(
        functools.partial(_gla_decode_kernel, heads=heads, dv=dv, scale=dk ** -0.5),
        grid=(n_seq,),
        in_specs=[sblk, pl.BlockSpec((1, heads, dk, SUBLANES), lambda b: (b, 0, 0, 0)), vblk, vblk,
                  pl.BlockSpec((1, dv), lambda b: (0, 0))],
        out_specs=[sblk, vblk],
        out_shape=[jax.ShapeDtypeStruct(state.shape, F32), jax.ShapeDtypeStruct((n_seq, 1, heads * dv), BF16)],
        compiler_params=_params(("parallel",)),
        name="gla_decode",
    )(state, colpack, v, g, g_norm)


def _gla_proj_spec(j, p):
    heads, dk, dv = p["state_gla"].shape[2:]
    hk, hv = heads * dk, heads * dv
    w = _pad_cols(p["w_in_gla"][j], 2 * hk + 2 * hv + LANES).astype(BF16)
    return w, [(0, hk), (hk, hk), (2 * hk, hv), (2 * hk + hv, hv), (2 * hk + 2 * hv, LANES)]


def _gla_layer(proj, j, bsz, n, tp, p):
    q, k, v, g, r = proj
    t_all = q.shape[0]
    n_seq, heads, dk, dv = p["state_gla"].shape[1:]
    hk, hv = heads * dk, heads * dv
    rank = p["w_gate2_gla"].shape[1]
    tm = _tile(t_all, (384, 256, 128))
    w_gate = jnp.pad(p["w_gate2_gla"][j], ((0, LANES - rank), (0, 0))).astype(BF16)
    log_a = _gla_gate(r, w_gate, p["b_gate_gla"][j][None, :], tm)
    g_norm = p["g_norm_gla"][j][None, :]
    o_p, st_p = _gla_prompt(q, k, v, g, log_a, g_norm, bsz, n, heads, tp, _tile(n, (256, 128, 64)))
    colpack = jnp.stack([log_a[tp:], k[tp:], q[tp:]], axis=-1).reshape(n_seq, heads, dk, 3)
    colpack = jnp.pad(colpack, ((0, 0), (0, 0), (0, 0), (0, SUBLANES - 3)))
    st_s, o_s = _gla_decode(p["state_gla"][j], colpack, v[tp:].reshape(n_seq, 1, hv),
                            g[tp:].reshape(n_seq, 1, hv), g_norm)
    o = jnp.concatenate([o_p, o_s.reshape(n_seq, hv)], axis=0)
    return o, dict(gla_p=st_p.transpose(0, 1, 3, 2), gla_s=st_s)


def _ssd_conv_prompt_kernel(x_ref, w_ref, b_ref, a_ref, st_ref, hist, *, tl):
    xbc = x_ref[...]
    conv = _causal_taps(hist, xbc, w_ref, pl.program_id(1) == 0, tl)
    a_ref[...] = _silu(conv + b_ref[...])
    keep = w_ref.shape[0] - 1
    st_ref[0] = xbc[tl - keep:tl, :]


def _ssd_conv_prompt(xbc, w_conv, b_conv, bsz, n, t_all, tl):
    ch = xbc.shape[1]
    width = w_conv.shape[0]
    nl = n // tl
    row = lambda b, l: (b * nl + l, 0)
    full = lambda b, l: (0, 0)
    return pl.pallas_call(
        functools.partial(_ssd_conv_prompt_kernel, tl=tl),
        grid=(bsz, nl),
        in_specs=[pl.BlockSpec((tl, ch), row), pl.BlockSpec((width, ch), full), pl.BlockSpec((1, ch), full)],
        out_specs=[pl.BlockSpec((tl, ch), row), pl.BlockSpec((1, width - 1, ch), lambda b, l: (b, 0, 0))],
        out_shape=[jax.ShapeDtypeStruct((t_all, ch), F32), jax.ShapeDtypeStruct((bsz, width - 1, ch), F32)],
        scratch_shapes=[pltpu.VMEM((HALO + tl, ch), F32)],
        compiler_params=_params(("parallel", "arbitrary")),
        name="ssd_conv_prompt",
    )(xbc, w_conv, b_conv)


def _ssd_conv_decode_kernel(x_ref, s_ref, w_ref, b_ref, a_ref):
    width = w_ref.shape[0]
    conv = w_ref[width - 1:width, :] * _round_bf16(x_ref[...])
    for k in range(width - 1):
        conv = conv + w_ref[k:k + 1, :] * _round_bf16(s_ref[k])
    a_ref[...] = _silu(conv + b_ref[...])


def _ssd_conv_decode(xbc, state_t, w_conv, b_conv):
    return pl.pallas_call(
        _ssd_conv_decode_kernel,
        out_shape=jax.ShapeDtypeStruct(xbc.shape, F32),
        compiler_params=pltpu.CompilerParams(vmem_limit_bytes=VMEM_LIMIT),
        name="ssd_conv_decode",
    )(xbc, state_t, w_conv, b_conv)


def _group_rms_gate(y, z, g_norm):
    yg = y * _silu(z)
    return yg * lax.rsqrt(jnp.mean(yg * yg, axis=-1, keepdims=True) + RMS_EPS) * g_norm


def _ssd_prompt_kernel(xs_ref, bm_ref, cm_ref, dt_ref, z_ref, dtb_ref, alog_ref, dsk_ref, gn_ref,
                       y_ref, st_ref, state, ybuf, *, q, heads, hd, ns, groups):
    l_idx = pl.program_id(1)

    @pl.when(l_idx == 0)
    def _():
        state[...] = jnp.zeros(state.shape, F32)

    pw = 2 * hd
    dt = _softplus(dt_ref[...] + dtb_ref[...])
    a = -jnp.exp(alog_ref[...])
    row = lax.broadcasted_iota(I32, (q, q), 0)
    col = lax.broadcasted_iota(I32, (q, q), 1)
    causal = row >= col
    tri = jnp.where(causal, 1.0, 0.0).astype(BF16)
    cum = _dot_exact_lhs(tri, dt * a)
    cum_t = cum.T
    e_cum = jnp.exp(cum)
    w_end = jnp.exp(cum[q - 1:q, :] - cum)
    dsk = dsk_ref[...]
    lo = lax.broadcasted_iota(I32, (q, pw), 1) < hd
    lo_rows = lax.broadcasted_iota(I32, (pw, ns), 0) < hd
    pairs_per_group = heads // groups // 2

    def pick(mat, h0):
        return jnp.where(lo, mat[:, h0:h0 + 1], mat[:, h0 + 1:h0 + 2])

    for g in range(groups):
        cmb = cm_ref[:, g * ns:(g + 1) * ns].astype(BF16)
        bmb = bm_ref[:, g * ns:(g + 1) * ns].astype(BF16)
        cb = _nt(cmb, bmb)
        for pp in range(pairs_per_group):
            pr = g * pairs_per_group + pp
            h0 = 2 * pr
            cols = slice(pr * pw, (pr + 1) * pw)
            x_pair = xs_ref[:, cols]
            xdt = x_pair * pick(dt, h0)

            def decay_mix(h):
                return (cb * jnp.exp(jnp.where(causal, cum[:, h:h + 1] - cum_t[h:h + 1, :], -jnp.inf))).astype(BF16)

            y = (jnp.dot(decay_mix(h0), jnp.where(lo, xdt, 0.0).astype(BF16), preferred_element_type=F32)
                 + jnp.dot(decay_mix(h0 + 1), jnp.where(lo, 0.0, xdt).astype(BF16), preferred_element_type=F32))
            s_pair = state[pr]
            y = y + _nt(cmb, s_pair.astype(BF16)) * pick(e_cum, h0)
            dec = jnp.where(lo_rows, jnp.exp(cum_t[h0:h0 + 1, q - 1:q]), jnp.exp(cum_t[h0 + 1:h0 + 2, q - 1:q]))
            state[pr] = s_pair * dec + _tn((xdt * pick(w_end, h0)).astype(BF16), bmb)
            ybuf[:, cols] = y + x_pair * pick(dsk, h0)

    gw = heads * hd // groups
    for g in range(groups):
        cols = slice(g * gw, (g + 1) * gw)
        y_ref[:, cols] = _group_rms_gate(ybuf[:, cols], z_ref[:, cols], gn_ref[:, cols]).astype(y_ref.dtype)

    @pl.when(l_idx == pl.num_programs(1) - 1)
    def _():
        st_ref[0] = state[...]


def _ssd_prompt(act, dt_raw, z, dt_bias, a_log, d_skip, g_norm, bsz, n, heads, hd, ns, groups, t_all, q):
    inner = heads * hd
    gn = groups * ns
    nl = n // q
    row = lambda b, l: (b * nl + l, 0)
    full = lambda b, l: (0, 0)
    n_pairs = heads // 2
    return pl.pallas_call(
        functools.partial(_ssd_prompt_kernel, q=q, heads=heads, hd=hd, ns=ns, groups=groups),
        grid=(bsz, nl),
        in_specs=[pl.BlockSpec((q, inner), row),
                  pl.BlockSpec((q, gn), lambda b, l: (b * nl + l, inner // gn)),
                  pl.BlockSpec((q, gn), lambda b, l: (b * nl + l, inner // gn + 1)),
                  pl.BlockSpec((q, LANES), row), pl.BlockSpec((q, inner), row),
                  pl.BlockSpec((1, LANES), full), pl.BlockSpec((1, LANES), full), pl.BlockSpec((1, LANES), full),
                  pl.BlockSpec((1, inner), full)],
        out_specs=[pl.BlockSpec((q, inner), row),
                   pl.BlockSpec((1, n_pairs, 2 * hd, ns), lambda b, l: (b, 0, 0, 0))],
        out_shape=[jax.ShapeDtypeStruct((t_all, inner), BF16),
                   jax.ShapeDtypeStruct((bsz, n_pairs, 2 * hd, ns), F32)],
        scratch_shapes=[pltpu.VMEM((n_pairs, 2 * hd, ns), F32), pltpu.VMEM((q, inner), F32)],
        compiler_params=_params(("parallel", "arbitrary")),
        name="ssd_prompt",
    )(act, act, act, dt_raw, z, dt_bias, a_log, d_skip, g_norm)


SSD_PACK_ROWS = 16


def _ssd_decode_kernel(s_ref, pack_ref, b_ref, c_ref, z_ref, gn_ref, so_ref, y_ref, *, n_pairs, ns, groups):
    pt = pack_ref[0].T
    pw = s_ref.shape[2]
    lane = lax.broadcasted_iota(I32, (pw, LANES), 1)
    pairs_per_group = n_pairs // groups
    quantity = lambda k: pt[:, k * SSD_PACK_ROWS:k * SSD_PACK_ROWS + n_pairs]
    x_all = quantity(0)
    dt_all = _softplus(quantity(1) + quantity(2))
    dec_all = jnp.exp(dt_all * -jnp.exp(quantity(3)))
    xdt_all = x_all * dt_all
    skip_all = x_all * quantity(4)
    y_cols = jnp.zeros((pw, LANES), F32)
    for pr in range(n_pairs):
        g = pr // pairs_per_group
        b_row = b_ref[0, :, g * ns:(g + 1) * ns]
        c_row = c_ref[0, :, g * ns:(g + 1) * ns]
        s_new = s_ref[0, pr] * dec_all[:, pr:pr + 1] + xdt_all[:, pr:pr + 1] * b_row
        so_ref[0, pr] = s_new
        y_col = jnp.sum(_round_bf16(s_new) * _round_bf16(c_row), axis=1, keepdims=True)
        y_cols = jnp.where(lane == pr, y_col, y_cols)
    y_cols = y_cols + jnp.pad(skip_all, ((0, 0), (0, LANES - n_pairs)))
    y_rows = y_cols.T[0:n_pairs, :]
    yg = y_rows * _silu(z_ref[0])
    sq = jnp.sum(yg * yg, axis=1, keepdims=True)
    out = jnp.zeros(yg.shape, F32)
    sub = lax.broadcasted_iota(I32, yg.shape, 0)
    sub1 = lax.broadcasted_iota(I32, sq.shape, 0)
    for g in range(groups):
        lo_r, hi_r = g * pairs_per_group, (g + 1) * pairs_per_group
        in_g1 = jnp.logical_and(sub1 >= lo_r, sub1 < hi_r)
        ms = jnp.sum(jnp.where(in_g1, sq, 0.0), axis=0, keepdims=True) / (pairs_per_group * pw)
        out = jnp.where(jnp.logical_and(sub >= lo_r, sub < hi_r), yg * lax.rsqrt(ms + RMS_EPS), out)
    y_ref[0] = (out * gn_ref[...]).astype(y_ref.dtype)


def _ssd_decode(state, pack, b_rows, c_rows, z, g_norm, groups):
    n_seq, n_pairs, pw, ns = state.shape
    sblk = pl.BlockSpec((1, n_pairs, pw, ns), lambda b: (b, 0, 0, 0))
    rblk = pl.BlockSpec((1, 1, groups * ns), lambda b: (b, 0, 0))
    zblk = pl.BlockSpec((1, n_pairs, pw), lambda b: (b, 0, 0))
    return pl.pallas_call(
        functools.partial(_ssd_decode_kernel, n_pairs=n_pairs, ns=ns, groups=groups),
        grid=(n_seq,),
        in_specs=[sblk, pl.BlockSpec((1, LANES, LANES), lambda b: (b, 0, 0)), rblk, rblk, zblk,
                  pl.BlockSpec((n_pairs, pw), lambda b: (0, 0))],
        out_specs=[sblk, zblk],
        out_shape=[jax.ShapeDtypeStruct(state.shape, F32), jax.ShapeDtypeStruct((n_seq, n_pairs, pw), BF16)],
        compiler_params=_params(("parallel",)),
        name="ssd_decode",
    )(state, pack, b_rows, c_rows, z, g_norm)


def _ssd_proj_spec(j, p):
    heads, hd, ns = p["state_ssm"].shape[2:]
    inner = heads * hd
    ch = inner + 2 * SSD_GROUPS * ns
    w = _pad_cols(p["w_in_ssd"][j], inner + ch + LANES).astype(BF16)
    return w, [(0, inner), (inner, ch), (inner + ch, LANES)]


def _ssd_layer(proj, j, bsz, n, tp, p):
    z, xbc, dt_raw = proj
    n_seq, heads, hd, ns = p["state_ssm"].shape[1:]
    groups = SSD_GROUPS
    inner = heads * hd
    gn = groups * ns
    ch = inner + 2 * gn
    n_pairs = heads // 2
    pw = 2 * hd
    assert pw == LANES and ns == LANES and n_pairs == SSD_PACK_ROWS and inner % gn == 0
    w_conv = p["w_conv_ssd"][j]
    b_conv = p["b_conv_ssd"][j][None, :]
    lane_row = lambda v: _pad_cols(v[None, :], LANES)
    g_norm = p["g_norm_ssd"][j]
    act, conv_p = _ssd_conv_prompt(xbc, w_conv, b_conv, bsz, n, tp, _tile(n, (256, 128)))
    y_p, ssm_p = _ssd_prompt(act, dt_raw, z, lane_row(p["dt_bias_ssd"][j]), lane_row(p["a_log_ssd"][j]),
                           lane_row(p["d_skip_ssd"][j]), g_norm[None, :], bsz, n, heads, hd, ns, groups,
                           tp, _tile(n, (128,)))
    conv_state = p["state_conv_ssd"][j]
    xbc_s = xbc[tp:]
    act_s = _ssd_conv_decode(xbc_s, conv_state.transpose(1, 0, 2), w_conv, b_conv)
    per_row = lambda v: jnp.broadcast_to(jnp.repeat(v, hd, axis=-1).reshape(-1, n_pairs, pw), (n_seq, n_pairs, pw))
    pack = jnp.concatenate([
        act_s[:, :inner].reshape(n_seq, n_pairs, pw),
        per_row(dt_raw[tp:, :heads]), per_row(p["dt_bias_ssd"][j][None, :]),
        per_row(p["a_log_ssd"][j][None, :]), per_row(p["d_skip_ssd"][j][None, :])], axis=1)
    pack = jnp.pad(pack, ((0, 0), (0, LANES - pack.shape[1]), (0, 0)))
    ssm_s, y_s = _ssd_decode(p["state_ssm"][j].reshape(n_seq, n_pairs, pw, ns), pack,
                             act_s[:, inner:inner + gn].reshape(n_seq, 1, gn),
                             act_s[:, inner + gn:].reshape(n_seq, 1, gn),
                             z[tp:].reshape(n_seq, n_pairs, pw), g_norm.reshape(n_pairs, pw), groups)
    y = jnp.concatenate([y_p, y_s.reshape(n_seq, inner)], axis=0)
    conv_s = jnp.concatenate([conv_state[:, 1:], xbc_s[:, None, :]], axis=1)
    return y, dict(ssm_p=ssm_p.reshape(bsz, heads, hd, ns), ssm_s=ssm_s.reshape(n_seq, heads, hd, ns),
                   conv_p=conv_p, conv_s=conv_s)


_LAYERS = (_fox_layer, _short_layer, _gla_layer, _ssd_layer)
_PROJ_SPECS = (_fox_proj_spec, _short_proj_spec, _gla_proj_spec, _ssd_proj_spec)
_OUT_ORDER = ("k", "v", "lf", "conv_short", "gla", "ssm", "conv_ssd")


def kernel(x_prompt, x_sample, cache_k_fox, cache_v_fox, cache_logf_fox, page_table, state_conv_short, state_gla, state_ssm, state_conv_ssd, w_in_fox, b_forget_fox, w_out_fox, w_in_short, w_conv_short, w_out_short, w_in_gla, w_gate2_gla, b_gate_gla, g_norm_gla, w_out_gla, w_in_ssd, w_conv_ssd, b_conv_ssd, dt_bias_ssd, a_log_ssd, d_skip_ssd, g_norm_ssd, w_out_ssd, ln_mix_g, ln_mix_b, ln_ffn_g, ln_ffn_b, w_router, b_router, w_gate_up, b_gate_up, w_down, b_down):
    p = dict(locals())
    bsz, n, d = x_prompt.shape
    n_seq, n_dec, _ = x_sample.shape
    assert n_dec == 1, "the sample group decodes one token per sequence"
    depth = ln_mix_g.shape[0]
    alpha = (2 * depth) ** 0.25
    tp = bsz * n
    x = jnp.concatenate([x_prompt.reshape(tp, d), x_sample.reshape(n_seq, d)], axis=0)
    w_outs = (w_out_fox, w_out_short, w_out_gla, w_out_ssd)
    acc = {name + sfx: [] for name in _OUT_ORDER for sfx in ("_p", "_s")}
    layer_of = lambda i: (i % len(_LAYERS), i // len(_LAYERS))
    w0, segs0 = _PROJ_SPECS[0](0, p)
    proj = _proj(x, w0, segs0, _tile(tp + n_seq, (384, 256, 128)))
    for i in range(depth):
        kind, j = layer_of(i)
        y_mix, outs = _LAYERS[kind](proj, j, bsz, n, tp, p)
        rename = {"conv_p": ("conv_short_p" if kind == 1 else "conv_ssd_p"),
                  "conv_s": ("conv_short_s" if kind == 1 else "conv_ssd_s")}
        for key, val in outs.items():
            acc[rename.get(key, key)].append(val)
        if i + 1 < depth:
            nkind, nj = layer_of(i + 1)
            x, *proj = _post_blocks(x, y_mix, w_outs[kind][j], i, alpha, p, _PROJ_SPECS[nkind](nj, p))
        else:
            x = _post_blocks(x, y_mix, w_outs[kind][j], i, alpha, p)
    stack = lambda name: jnp.stack(acc[name])
    return (x[:tp].reshape(bsz, n, d), x[tp:].reshape(n_seq, n_dec, d),
            *[stack(name + "_p") for name in _OUT_ORDER],
            *[stack(name + "_s") for name in _OUT_ORDER])
```
